```python
import functools
import jax, jax.numpy as jnp
from jax import lax
import numpy as np

D_MODEL = 1024
BATCH = 2
SEQ = 8192
DEPTH = 4
DEC_BATCH = 32
DEC_SEQ = 1
PAST_LEN = 8192
PAGE_SIZE = 128

SSD_HEAD_DIM = 64
SSD_INNER = D_MODEL
SSD_HEADS = SSD_INNER // SSD_HEAD_DIM
SSD_GROUPS = 2
SSD_REP = SSD_HEADS // SSD_GROUPS
SSD_STATE = 128
SSD_CONV = 4
SSD_CONV_DIM = SSD_INNER + 2 * SSD_GROUPS * SSD_STATE
SSD_CHUNK = 128
CF_DIM = D_MODEL
CF_KERNEL = 31
ATTN_HEADS = 16
ATTN_HEAD_DIM = D_MODEL // ATTN_HEADS
ATTN_KV_HEADS = 4
ATTN_REP = ATTN_HEADS // ATTN_KV_HEADS
ATTN_WIDTH = ATTN_HEADS * ATTN_HEAD_DIM
KV_WIDTH = ATTN_KV_HEADS * ATTN_HEAD_DIM
IDX_HEADS = 8
IDX_DIM = 64
TOPK_MAX = 256
Q_BLOCK = 128

EVEN_SPLITS = (SSD_INNER, SSD_INNER + SSD_CONV_DIM, SSD_INNER + SSD_CONV_DIM + SSD_HEADS,
               SSD_INNER + SSD_CONV_DIM + SSD_HEADS + 2 * CF_DIM)
EVEN_IN = SSD_INNER + SSD_CONV_DIM + SSD_HEADS + 3 * CF_DIM
ODD_SPLITS = (ATTN_WIDTH, ATTN_WIDTH + KV_WIDTH, ATTN_WIDTH + 2 * KV_WIDTH,
              ATTN_WIDTH + 2 * KV_WIDTH + IDX_HEADS * IDX_DIM,
              ATTN_WIDTH + 2 * KV_WIDTH + IDX_HEADS * IDX_DIM + IDX_DIM,
              ATTN_WIDTH + 2 * KV_WIDTH + IDX_HEADS * IDX_DIM + IDX_DIM + IDX_HEADS)
ODD_IN = 2 * ATTN_WIDTH + 2 * KV_WIDTH + IDX_HEADS * IDX_DIM + IDX_DIM + IDX_HEADS
N_EVEN = (DEPTH + 1) // 2
N_ODD = DEPTH // 2

ALPHA = (2 * DEPTH) ** 0.25
BETA = (8 * DEPTH) ** -0.25
EPS = 1e-5

kernel_name = 'hybrid_ssd_conformer_dsa_step'


def layer_norm(x, g, b):
    xf = x.astype(jnp.float32)
    mu = jnp.mean(xf, -1, keepdims=True)
    var = jnp.mean(jnp.square(xf - mu), -1, keepdims=True)
    return ((xf - mu) * lax.rsqrt(var + EPS) * g + b).astype(x.dtype)


def causal_dwconv(x_ext, w, b):
    c = x_ext.shape[-1]
    y = lax.conv_general_dilated(x_ext, w.astype(x_ext.dtype)[:, None, :], window_strides=(1,),
                                 padding='VALID', dimension_numbers=('NWC', 'WIO', 'NWC'),
                                 feature_group_count=c)
    return y + b.astype(x_ext.dtype)


def ssd_chunked(xs, dt, a, bm, cm):
    f32 = jnp.float32
    b, l, g, r, p = xs.shape
    n = bm.shape[-1]
    q = SSD_CHUNK
    c = l // q
    X = (xs.astype(f32) * dt[..., None]).reshape(b, c, q, g, r, p)
    a_cum = jnp.cumsum((dt * a).reshape(b, c, q, g, r), axis=2)
    Bc = bm.astype(f32).reshape(b, c, q, g, n)
    Cc = cm.astype(f32).reshape(b, c, q, g, n)
    diff = a_cum[:, :, :, None] - a_cum[:, :, None, :]
    causal = jnp.tril(jnp.ones((q, q), bool))[:, :, None, None]
    decay = jnp.exp(jnp.where(causal, diff, -jnp.inf))
    cb = jnp.einsum('bctgn,bcsgn->bctsg', Cc, Bc)
    y_diag = jnp.einsum('bctsg,bctsgr,bcsgrp->bctgrp', cb, decay, X)
    to_end = jnp.exp(a_cum[:, :, -1:] - a_cum)
    chunk_states = jnp.einsum('bcsgn,bcsgr,bcsgrp->bcgrpn', Bc, to_end, X)
    chunk_decay = jnp.exp(a_cum[:, :, -1])

    def step(state, inp):
        dec, st = inp
        return state * dec[..., None, None] + st, state

    init = jnp.zeros((b, g, r, p, n), f32)
    final, prev = lax.scan(step, init, (jnp.moveaxis(chunk_decay, 1, 0), jnp.moveaxis(chunk_states, 1, 0)))
    prev = jnp.moveaxis(prev, 0, 1)
    y_off = jnp.einsum('bctgn,bcgrpn,bctgr->bctgrp', Cc, prev, jnp.exp(a_cum))
    return (y_diag + y_off).reshape(b, l, g, r, p), final


def ssd_recurrent(xs, dt, a, bm, cm, state):
    f32 = jnp.float32
    b = xs.shape[0]
    s0 = state.astype(f32).reshape(b, SSD_GROUPS, SSD_REP, SSD_HEAD_DIM, SSD_STATE)

    def step(s, inp):
        x_t, dt_t, b_t, c_t = inp
        s = s * jnp.exp(dt_t * a)[..., None, None] + jnp.einsum('bgrp,bgn->bgrpn', x_t * dt_t[..., None], b_t)
        return s, jnp.einsum('bgrpn,bgn->bgrp', s, c_t)

    seq_first = lambda t: jnp.moveaxis(t.astype(f32), 1, 0)
    final, y = lax.scan(step, s0, (seq_first(xs), seq_first(dt), seq_first(bm), seq_first(cm)))
    return jnp.moveaxis(y, 0, 1), final


def even_layer(x, ssd_ctx, cf_ctx, ssd_scan, w_in, conv_w, conv_b, dt_bias, a_log, d_skip, norm_g,
               cf_w, cf_b, cf_g, cf_beta, w_out, ln_g, ln_b):
    f32 = jnp.float32
    b, l, _ = x.shape
    z_a, xbc, dt, glu_in, z_b = jnp.split(x @ w_in, EVEN_SPLITS, axis=-1)
    xbc_ext = jnp.concatenate([ssd_ctx.astype(x.dtype), xbc], axis=1)
    xbc_c = jax.nn.silu(causal_dwconv(xbc_ext, conv_w, conv_b))
    xs, bm, cm = jnp.split(xbc_c, (SSD_INNER, SSD_INNER + SSD_GROUPS * SSD_STATE), axis=-1)
    xs = xs.reshape(b, l, SSD_GROUPS, SSD_REP, SSD_HEAD_DIM)
    bm = bm.reshape(b, l, SSD_GROUPS, SSD_STATE)
    cm = cm.reshape(b, l, SSD_GROUPS, SSD_STATE)
    dt = jax.nn.softplus((dt + dt_bias).astype(f32)).reshape(b, l, SSD_GROUPS, SSD_REP)
    a = -jnp.exp(a_log.astype(f32)).reshape(SSD_GROUPS, SSD_REP)
    y, ssm_final = ssd_scan(xs, dt, a, bm, cm)
    y = y + d_skip.astype(f32).reshape(SSD_GROUPS, SSD_REP)[..., None] * xs.astype(f32)
    y = y.reshape(b, l, SSD_INNER) * jax.nn.silu(z_a.astype(f32))
    y = y.reshape(b, l, SSD_GROUPS, SSD_INNER // SSD_GROUPS)
    y = y * lax.rsqrt(jnp.mean(jnp.square(y), -1, keepdims=True) + EPS)
    y_a = (y.reshape(b, l, SSD_INNER) * norm_g).astype(x.dtype)
    u_val, u_gate = jnp.split(glu_in, 2, axis=-1)
    u = u_val * jax.nn.sigmoid(u_gate)
    u_ext = jnp.concatenate([cf_ctx.astype(x.dtype), u], axis=1)
    v = jax.nn.silu(layer_norm(causal_dwconv(u_ext, cf_w, cf_b), cf_g, cf_beta))
    y_b = v * jax.nn.silu(z_b)
    out = jnp.concatenate([y_a, y_b], axis=-1) @ w_out
    x_new = layer_norm(ALPHA * x + out, ln_g, ln_b)
    return (x_new, ssm_final.reshape(b, SSD_HEADS, SSD_HEAD_DIM, SSD_STATE).astype(x.dtype),
            xbc_ext[:, -(SSD_CONV - 1):], u_ext[:, -(CF_KERNEL - 1):])


def odd_project(x, w_in):
    b, l, _ = x.shape
    q, k, v, qi, ki, wi, z = jnp.split(x @ w_in, ODD_SPLITS, axis=-1)
    q = q.reshape(b, l, ATTN_KV_HEADS, ATTN_REP, ATTN_HEAD_DIM)
    k = k.reshape(b, l, ATTN_KV_HEADS, ATTN_HEAD_DIM)
    v = v.reshape(b, l, ATTN_KV_HEADS, ATTN_HEAD_DIM)
    qi = qi.reshape(b, l, IDX_HEADS, IDX_DIM)
    return q, k, v, qi, ki, wi, z


def index_scores(qi, wi, ki):
    s = jnp.einsum('bthd,bsd->bths', qi, ki, preferred_element_type=jnp.float32) * IDX_DIM ** -0.5
    return jnp.einsum('bths,bth->bts', jax.nn.relu(s), wi.astype(jnp.float32) * IDX_HEADS ** -0.5)


def dsa_attend(q, qi, wi, q_pos, ki_all, gather_kv, topk):
    scores = index_scores(qi, wi, ki_all)
    key_pos = jnp.arange(ki_all.shape[1])
    admissible = key_pos[None, :] <= q_pos[:, None]
    scores = jnp.where(admissible[None], scores, -jnp.inf)
    _, idx = lax.top_k(scores, topk)
    valid = idx <= q_pos[None, :, None]
    k_sel, v_sel = gather_kv(idx)
    logits = jnp.einsum('btgrd,btkgd->btgrk', q, k_sel, preferred_element_type=jnp.float32) * ATTN_HEAD_DIM ** -0.5
    logits = jnp.where(valid[:, :, None, None, :], logits, -jnp.inf)
    p = jax.nn.softmax(logits, axis=-1)
    return jnp.einsum('btgrk,btkgd->btgrd', p.astype(v_sel.dtype), v_sel)


def odd_out(x, o, z, w_out, ln_g, ln_b):
    b, l, _ = x.shape
    out = (o.reshape(b, l, ATTN_WIDTH) * jax.nn.silu(z)) @ w_out
    return layer_norm(ALPHA * x + out, ln_g, ln_b)


def odd_layer_prompt(x, w_in, w_out, ln_g, ln_b):
    b, l, _ = x.shape
    q, k, v, qi, ki, wi, z = odd_project(x, w_in)
    topk = min(TOPK_MAX, l // 4)
    take = jax.vmap(lambda arr, i: arr[i])

    def gather_kv(idx):
        return take(k, idx), take(v, idx)

    def block(i):
        start = i * Q_BLOCK
        sl = lambda t: lax.dynamic_slice_in_dim(t, start, Q_BLOCK, axis=1)
        pos = start + jnp.arange(Q_BLOCK)
        return dsa_attend(sl(q), sl(qi), sl(wi), pos, ki, gather_kv, topk)

    o = lax.map(block, jnp.arange(l // Q_BLOCK))
    o = jnp.moveaxis(o, 0, 1)
    return odd_out(x, o, z, w_out, ln_g, ln_b), k, v, ki


def odd_layer_sample(x, cache_k, cache_v, cache_ki, page_table, w_in, w_out, ln_g, ln_b):
    b, t, _ = x.shape
    q, k, v, qi, ki, wi, z = odd_project(x, w_in)
    past = page_table.shape[1] * PAGE_SIZE
    topk = min(TOPK_MAX, (past + t) // 4)
    ki_past = cache_ki[page_table].reshape(b, past, IDX_DIM)
    ki_all = jnp.concatenate([ki_past.astype(ki.dtype), ki], axis=1)
    take = jax.vmap(lambda arr, i: arr[i])

    def gather_kv(idx):
        in_past = idx < past
        pidx = jnp.minimum(idx, past - 1)
        phys = jnp.take_along_axis(page_table, pidx.reshape(b, -1) // PAGE_SIZE, axis=1).reshape(idx.shape)
        off = pidx % PAGE_SIZE
        nidx = jnp.clip(idx - past, 0, t - 1)

        def sel(cache, new):
            return jnp.where(in_past[..., None, None], cache[phys, off].astype(new.dtype), take(new, nidx))

        return sel(cache_k, k), sel(cache_v, v)

    pos = past + jnp.arange(t)
    o = dsa_attend(q, qi, wi, pos, ki_all, gather_kv, topk)
    return odd_out(x, o, z, w_out, ln_g, ln_b), k, v, ki


def setup_inputs(seed: int = 0) -> dict:
    key = jax.random.key(seed)
    keys = iter(jax.random.split(key, 40))

    def nrm(shape, scale=1.0):
        return jax.random.normal(next(keys), shape, jnp.float32) * scale

    n_pages = PAST_LEN // PAGE_SIZE
    n_pool = (DEC_BATCH * n_pages * 5) // 4
    ssm_shape = (DEC_BATCH, SSD_HEADS, SSD_HEAD_DIM, SSD_STATE)
    ssdconv_shape = (DEC_BATCH, SSD_CONV - 1, SSD_CONV_DIM)
    cfconv_shape = (DEC_BATCH, CF_KERNEL - 1, CF_DIM)
    kv_shape = (n_pool, PAGE_SIZE, ATTN_KV_HEADS, ATTN_HEAD_DIM)
    kidx_shape = (n_pool, PAGE_SIZE, IDX_DIM)
    inp = {}
    inp['x_prompt'] = nrm((BATCH, SEQ, D_MODEL))
    inp['x_sample'] = nrm((DEC_BATCH, DEC_SEQ, D_MODEL))
    inp['state_ssm_l0'] = nrm(ssm_shape, 0.5)
    inp['state_ssdconv_l0'] = nrm(ssdconv_shape)
    inp['state_cfconv_l0'] = nrm(cfconv_shape, 0.6)
    inp['cache_k_l1'] = nrm(kv_shape)
    inp['cache_v_l1'] = nrm(kv_shape)
    inp['cache_kidx_l1'] = nrm(kidx_shape)
    inp['state_ssm_l2'] = nrm(ssm_shape, 0.5)
    inp['state_ssdconv_l2'] = nrm(ssdconv_shape)
    inp['state_cfconv_l2'] = nrm(cfconv_shape, 0.6)
    inp['cache_k_l3'] = nrm(kv_shape)
    inp['cache_v_l3'] = nrm(kv_shape)
    inp['cache_kidx_l3'] = nrm(kidx_shape)
    perm = jax.random.permutation(next(keys), n_pool)[:DEC_BATCH * n_pages]
    inp['page_table'] = perm.reshape(DEC_BATCH, n_pages).astype(jnp.int32)
    inp['w_in_even'] = nrm((N_EVEN, D_MODEL, EVEN_IN), D_MODEL ** -0.5)
    inp['ssd_conv_w'] = nrm((N_EVEN, SSD_CONV, SSD_CONV_DIM), SSD_CONV ** -0.5)
    inp['ssd_conv_b'] = nrm((N_EVEN, SSD_CONV_DIM), 0.02)
    dt0 = jnp.exp(jax.random.uniform(next(keys), (N_EVEN, SSD_HEADS), jnp.float32,
                                     float(np.log(1e-3)), float(np.log(1e-1))))
    inp['ssd_dt_bias'] = dt0 + jnp.log(-jnp.expm1(-dt0))
    inp['ssd_a_log'] = jnp.log(jax.random.uniform(next(keys), (N_EVEN, SSD_HEADS), jnp.float32, 1.0, 16.0))
    inp['ssd_d'] = 1.0 + nrm((N_EVEN, SSD_HEADS), 0.1)
    inp['ssd_norm_g'] = 1.0 + nrm((N_EVEN, SSD_INNER), 0.05)
    inp['cf_dw_w'] = nrm((N_EVEN, CF_KERNEL, CF_DIM), CF_KERNEL ** -0.5)
    inp['cf_dw_b'] = nrm((N_EVEN, CF_DIM), 0.02)
    inp['cf_ln_g'] = 1.0 + nrm((N_EVEN, CF_DIM), 0.05)
    inp['cf_ln_b'] = nrm((N_EVEN, CF_DIM), 0.02)
    inp['w_out_even'] = nrm((N_EVEN, SSD_INNER + CF_DIM, D_MODEL), BETA * (SSD_INNER + CF_DIM) ** -0.5)
    inp['w_in_odd'] = nrm((N_ODD, D_MODEL, ODD_IN), D_MODEL ** -0.5)
    inp['w_out_odd'] = nrm((N_ODD, ATTN_WIDTH, D_MODEL), BETA * ATTN_WIDTH ** -0.5)
    inp['ln_g'] = 1.0 + nrm((DEPTH, D_MODEL), 0.05)
    inp['ln_b'] = nrm((DEPTH, D_MODEL), 0.02)
    return inp


def reference(x_prompt, x_sample,
              state_ssm_l0, state_ssdconv_l0, state_cfconv_l0,
              cache_k_l1, cache_v_l1, cache_kidx_l1,
              state_ssm_l2, state_ssdconv_l2, state_cfconv_l2,
              cache_k_l3, cache_v_l3, cache_kidx_l3,
              page_table,
              w_in_even, ssd_conv_w, ssd_conv_b, ssd_dt_bias, ssd_a_log, ssd_d, ssd_norm_g,
              cf_dw_w, cf_dw_b, cf_ln_g, cf_ln_b, w_out_even,
              w_in_odd, w_out_odd, ln_g, ln_b):
    ssm_states = (state_ssm_l0, state_ssm_l2)
    ssdconv_states = (state_ssdconv_l0, state_ssdconv_l2)
    cfconv_states = (state_cfconv_l0, state_cfconv_l2)
    k_caches = (cache_k_l1, cache_k_l3)
    v_caches = (cache_v_l1, cache_v_l3)
    kidx_caches = (cache_kidx_l1, cache_kidx_l3)
    yp, ys = x_prompt, x_sample
    new_state = []
    for layer in range(DEPTH):
        j = layer // 2
        if layer % 2 == 0:
            params = (w_in_even[j], ssd_conv_w[j], ssd_conv_b[j], ssd_dt_bias[j], ssd_a_log[j], ssd_d[j],
                      ssd_norm_g[j], cf_dw_w[j], cf_dw_b[j], cf_ln_g[j], cf_ln_b[j], w_out_even[j],
                      ln_g[layer], ln_b[layer])
            bp = yp.shape[0]
            ssd_ctx0 = jnp.zeros((bp, SSD_CONV - 1, SSD_CONV_DIM), yp.dtype)
            cf_ctx0 = jnp.zeros((bp, CF_KERNEL - 1, CF_DIM), yp.dtype)
            yp, ssm_p, sc_p, cc_p = even_layer(yp, ssd_ctx0, cf_ctx0, ssd_chunked, *params)
            ys, ssm_s, sc_s, cc_s = even_layer(ys, ssdconv_states[j], cfconv_states[j],
                                               functools.partial(ssd_recurrent, state=ssm_states[j]), *params)
            new_state += [ssm_p, ssm_s, sc_p, sc_s, cc_p, cc_s]
        else:
            params = (w_in_odd[j], w_out_odd[j], ln_g[layer], ln_b[layer])
            yp, k_p, v_p, ki_p = odd_layer_prompt(yp, *params)
            ys, k_s, v_s, ki_s = odd_layer_sample(ys, k_caches[j], v_caches[j], kidx_caches[j], page_table, *params)
            new_state += [k_p, k_s, v_p, v_s, ki_p, ki_s]
    return (yp, ys, *new_state)
```

```python
import functools

import jax
import jax.numpy as jnp
from jax import lax
from jax.experimental import pallas as pl
from jax.experimental.pallas import tpu as pltpu

_BF = jnp.bfloat16
_F32 = jnp.float32
_HI = lax.Precision.HIGHEST

_SSD_HEADS = 16
_SSD_HEAD_DIM = 64
_SSD_GROUPS = 2
_SSD_STATE = 128
_SSD_CONV = 4
_CF_KERNEL = 31
_ATTN_HEADS = 16
_ATTN_KV_HEADS = 4
_ATTN_HEAD_DIM = 64
_IDX_HEADS = 8
_IDX_DIM = 64
_TOPK_MAX = 256
_PAGE = 128
_DEPTH = 4
_ALPHA = (2 * _DEPTH) ** 0.25
_EPS = 1e-5

_LANES = 128
_VMEM_LIMIT = 56 * 1024 * 1024
_NEG = -1e30
_MAX_BISECT = 200


def _dot(a, b):
    return jnp.dot(a, b, preferred_element_type=_F32)


def _dot_nt(a, b):
    return lax.dot_general(a, b, (((1,), (1,)), ((), ())), preferred_element_type=_F32)


def _dot_hi(a, b):
    return jnp.dot(a, b, precision=_HI, preferred_element_type=_F32)


def _dot_nt_hi(a, b):
    return lax.dot_general(a, b, (((1,), (1,)), ((), ())), precision=_HI,
                           preferred_element_type=_F32)


def _sigmoid(x):
    return 1.0 / (1.0 + jnp.exp(-x))


def _silu(x):
    return x * _sigmoid(x)


def _softplus(x):
    return jnp.maximum(x, 0.0) + jnp.log1p(jnp.exp(-jnp.abs(x)))


def _layer_norm(x, g, b):
    mu = jnp.mean(x, axis=-1, keepdims=True)
    xc = x - mu
    var = jnp.mean(xc * xc, axis=-1, keepdims=True)
    return xc * lax.rsqrt(var + _EPS) * g + b


def _cparams(*sem):
    return pltpu.CompilerParams(dimension_semantics=sem, vmem_limit_bytes=_VMEM_LIMIT)


def _const_spec(shape):
    return pl.BlockSpec(shape, lambda *_: (0,) * len(shape), pipeline_mode=pl.Buffered(1))


def _even_proj_kernel(x_ref, wza, wxbc, wdt, wglu, wzb, za_o, xbc_o, dt_o, glu_o, zb_o):
    xb = x_ref[...].astype(_BF)
    za_o[...] = _dot(xb, wza[...])
    xbc_o[...] = _dot(xb, wxbc[...])
    dt_o[...] = _dot(xb, wdt[...])
    glu_o[...] = _dot(xb, wglu[...])
    zb_o[...] = _dot(xb, wzb[...])


def _even_proj(x, ws):
    t, d = x.shape
    tm = min(256, t)
    widths = [w.shape[1] for w in ws]
    row = lambda n: pl.BlockSpec((tm, n), lambda i: (i, 0))
    return pl.pallas_call(
        _even_proj_kernel,
        grid=(t // tm,),
        in_specs=[row(d)] + [_const_spec(w.shape) for w in ws],
        out_specs=[row(n) for n in widths],
        out_shape=[jax.ShapeDtypeStruct((t, n), _F32) for n in widths],
        compiler_params=_cparams("arbitrary"),
        name="even_proj",
    )(x, *ws)


def _odd_proj_prompt_kernel(x_ref, wq, wk, wv, wqi, wkw, wz, wkT, wkiT,
                            q_o, k_o, v_o, qi_o, ki_o, wi_o, z_o, kT_o, kiT_o, vg_o):
    xb = x_ref[...].astype(_BF)
    q_o[...] = (_dot(xb, wq[...]) * (_ATTN_HEAD_DIM ** -0.5)).astype(_BF)
    k_o[...] = _dot(xb, wk[...])
    v = _dot(xb, wv[...])
    v_o[...] = v
    for g in range(_ATTN_KV_HEADS):
        vg_o[g] = v[:, g * _ATTN_HEAD_DIM:(g + 1) * _ATTN_HEAD_DIM].astype(_BF)
    qi_o[...] = _dot(xb, wqi[...]).astype(_BF)
    kw = _dot(xb, wkw[...])
    ki_o[...] = kw[:, :_IDX_DIM]
    wi_o[...] = kw
    z_o[...] = _dot(xb, wz[...])
    kT_o[...] = _dot_nt(wkT[...], xb).astype(_BF)
    kiT_o[...] = _dot_nt(wkiT[...], xb).astype(_BF)


def _odd_proj_prompt(x, ws, wts, bsz, seq, tkb):
    t, d = x.shape
    tm = tkb
    nb = seq // tm
    wq, wk, wv, wqi, wkw, wz = ws
    wkT, wkiT = wts
    kvw = wk.shape[1]
    row = lambda n: pl.BlockSpec((tm, n), lambda i: (i, 0))
    outs = [
        (jax.ShapeDtypeStruct((t, wq.shape[1]), _BF), row(wq.shape[1])),
        (jax.ShapeDtypeStruct((t, kvw), _F32), row(kvw)),
        (jax.ShapeDtypeStruct((t, kvw), _F32), row(kvw)),
        (jax.ShapeDtypeStruct((t, wqi.shape[1]), _BF), row(wqi.shape[1])),
        (jax.ShapeDtypeStruct((t, _IDX_DIM), _F32), row(_IDX_DIM)),
        (jax.ShapeDtypeStruct((t, _LANES), _F32), row(_LANES)),
        (jax.ShapeDtypeStruct((t, wz.shape[1]), _F32), row(wz.shape[1])),
        (jax.ShapeDtypeStruct((bsz, nb, kvw, tm), _BF),
         pl.BlockSpec((None, None, kvw, tm), lambda i: (i // nb, i % nb, 0, 0))),
        (jax.ShapeDtypeStruct((bsz, nb, _IDX_DIM, tm), _BF),
         pl.BlockSpec((None, None, _IDX_DIM, tm), lambda i: (i // nb, i % nb, 0, 0))),
        (jax.ShapeDtypeStruct((bsz, _ATTN_KV_HEADS, seq, _ATTN_HEAD_DIM), _BF),
         pl.BlockSpec((None, _ATTN_KV_HEADS, tm, _ATTN_HEAD_DIM), lambda i: (i // nb, 0, i % nb, 0))),
    ]
    return pl.pallas_call(
        _odd_proj_prompt_kernel,
        grid=(t // tm,),
        in_specs=[row(d)] + [_const_spec(w.shape) for w in (*ws, *wts)],
        out_specs=[o[1] for o in outs],
        out_shape=[o[0] for o in outs],
        compiler_params=_cparams("arbitrary"),
        name="odd_proj_prompt",
    )(x, *ws, *wts)


def _odd_proj_sample_kernel(x_ref, wq, wk, wv, wqi, wkw, wz, q_o, k_o, v_o, qi_o, ki_o, wi_o, z_o):
    xb = x_ref[...].astype(_BF)
    q_o[...] = _dot(xb, wq[...]) * (_ATTN_HEAD_DIM ** -0.5)
    k_o[...] = _dot(xb, wk[...])
    v_o[...] = _dot(xb, wv[...])
    qi_o[...] = _dot(xb, wqi[...])
    kw = _dot(xb, wkw[...])
    ki_o[...] = kw[:, :_IDX_DIM]
    wi_o[...] = kw
    z_o[...] = _dot(xb, wz[...])


def _odd_proj_sample(x, ws):
    t, d = x.shape
    wq, wk, wv, wqi, wkw, wz = ws
    widths = [wq.shape[1], wk.shape[1], wv.shape[1], wqi.shape[1], _IDX_DIM, _LANES, wz.shape[1]]
    full = lambda n: pl.BlockSpec((t, n), lambda i: (0, 0))
    return pl.pallas_call(
        _odd_proj_sample_kernel,
        grid=(1,),
        in_specs=[full(d)] + [_const_spec(w.shape) for w in ws],
        out_specs=[full(n) for n in widths],
        out_shape=[jax.ShapeDtypeStruct((t, n), _F32) for n in widths],
        compiler_params=_cparams("arbitrary"),
        name="odd_proj_sample",
    )(x, *ws)


def _outproj_kernel(*refs, n_in, gated):
    x_ref = refs[0]
    a_refs = refs[1:1 + n_in]
    pos = 1 + n_in
    z_ref = refs[pos] if gated else None
    pos += 1 if gated else 0
    w_refs = refs[pos:pos + n_in]
    g_ref, b_ref, o_ref = refs[pos + n_in:pos + n_in + 3]
    acc = _ALPHA * x_ref[...]
    for a_ref, w_ref in zip(a_refs, w_refs):
        a = a_ref[...]
        if gated:
            a = a.astype(_F32) * _silu(z_ref[...])
        acc = acc + _dot(a.astype(_BF), w_ref[...])
    o_ref[...] = _layer_norm(acc, g_ref[...], b_ref[...])


def _outproj_ln(x, a_list, w_list, g, b, z=None):
    t, d = x.shape
    tm = min(512, t)
    row = lambda n: pl.BlockSpec((tm, n), lambda i: (i, 0))
    gated = z is not None
    ins = [x, *a_list] + ([z] if gated else []) + [*w_list, g, b]
    specs = ([row(d)] + [row(a.shape[1]) for a in a_list] + ([row(z.shape[1])] if gated else [])
             + [_const_spec(w.shape) for w in w_list] + [_const_spec(g.shape), _const_spec(b.shape)])
    return pl.pallas_call(
        functools.partial(_outproj_kernel, n_in=len(a_list), gated=gated),
        grid=(t // tm,),
        in_specs=specs,
        out_specs=row(d),
        out_shape=jax.ShapeDtypeStruct((t, d), _F32),
        compiler_params=_cparams("arbitrary"),
        name="outproj_ln",
    )(*ins)


def _group_rmsnorm_gate(y, za, ng):
    y = y * _silu(za)
    gw = y.shape[1] // _SSD_GROUPS
    parts = []
    for g in range(_SSD_GROUPS):
        yg = y[:, g * gw:(g + 1) * gw]
        ms = jnp.mean(yg * yg, axis=-1, keepdims=True)
        parts.append(yg * lax.rsqrt(ms + _EPS))
    return jnp.concatenate(parts, axis=-1) * ng


def _even_mix_kernel(za_ref, xbc_ref, dt_ref, glu_ref, zb_ref,
                     cw_ref, cb_ref, dtb_ref, alog_ref, d_ref, ng_ref,
                     fw_ref, fb_ref, fg_ref, fbeta_ref,
                     ya_o, yb_o, ssm_o, sc_o, cc_o,
                     xbuf, ubuf, state, *, q, nc):
    c = pl.program_id(1)
    inner = _SSD_HEADS * _SSD_HEAD_DIM
    gn = _SSD_GROUPS * _SSD_STATE
    xhalo = 8
    uhalo = 32

    @pl.when(c == 0)
    def _():
        xbuf[0:xhalo, :] = jnp.zeros((xhalo, xbuf.shape[1]), _F32)
        ubuf[0:uhalo, :] = jnp.zeros((uhalo, ubuf.shape[1]), _F32)
        state[...] = jnp.zeros(state.shape, _F32)

    xbuf[xhalo:xhalo + q, :] = xbc_ref[...]
    acc = jnp.broadcast_to(cb_ref[...], (q, xbuf.shape[1]))
    for k in range(_SSD_CONV):
        off = xhalo - (_SSD_CONV - 1) + k
        acc = acc + cw_ref[k:k + 1, :] * xbuf[off:off + q, :]
    xc = _silu(acc)
    xs = xc[:, :inner]
    bm = [xc[:, inner + g * _SSD_STATE: inner + (g + 1) * _SSD_STATE].astype(_BF) for g in range(_SSD_GROUPS)]
    cm = [xc[:, inner + gn + g * _SSD_STATE: inner + gn + (g + 1) * _SSD_STATE].astype(_BF)
          for g in range(_SSD_GROUPS)]

    @pl.when(c == nc - 1)
    def _():
        sc_o[...] = xbuf[xhalo + q - (_SSD_CONV - 1):xhalo + q, :]

    xbuf[0:xhalo, :] = xbuf[q:q + xhalo, :]

    lane = lax.broadcasted_iota(jnp.int32, (1, _LANES), 1)
    hmask = lane < _SSD_HEADS
    dt = jnp.where(hmask, _softplus(dt_ref[...] + dtb_ref[...]), 0.0)
    a = jnp.where(hmask, -jnp.exp(alog_ref[...]), 0.0)
    ri = lax.broadcasted_iota(jnp.int32, (q, q), 0)
    ci = lax.broadcasted_iota(jnp.int32, (q, q), 1)
    trilb = ri >= ci
    a_cum = _dot_hi(trilb.astype(_F32), dt * a)
    a_cum_t = a_cum.T
    dt_t = dt.T
    a_last_t = a_cum_t[:, q - 1:q]
    w_t = dt_t * jnp.exp(a_last_t - a_cum_t)
    ea = jnp.exp(a_cum)
    cd_t = jnp.exp(a_last_t)
    lo_lane = lane < _SSD_HEAD_DIM
    lo_sub = lax.broadcasted_iota(jnp.int32, (_LANES, 1), 0) < _SSD_HEAD_DIM

    cb = [_dot_nt(cm[g], bm[g]) for g in range(_SSD_GROUPS)]
    heads_per_group = _SSD_HEADS // _SSD_GROUPS
    ys = []
    for j in range(_SSD_HEADS // 2):
        h0, h1 = 2 * j, 2 * j + 1
        g = h0 // heads_per_group
        sl = slice(j * _LANES, (j + 1) * _LANES)
        xs2 = xs[:, sl]
        x2 = (xs2 * jnp.where(lo_lane, dt[:, h0:h0 + 1], dt[:, h1:h1 + 1])).astype(_BF)
        l0 = jnp.where(trilb, jnp.exp(a_cum[:, h0:h0 + 1] - a_cum_t[h0:h0 + 1, :]), 0.0)
        l1 = jnp.where(trilb, jnp.exp(a_cum[:, h1:h1 + 1] - a_cum_t[h1:h1 + 1, :]), 0.0)
        y0 = _dot((cb[g] * l0).astype(_BF), x2)
        y1 = _dot((cb[g] * l1).astype(_BF), x2)
        sp = state[sl, :]
        yoff = _dot_nt(cm[g], sp.astype(_BF)) * jnp.where(lo_lane, ea[:, h0:h0 + 1], ea[:, h1:h1 + 1])
        w2 = jnp.where(lo_sub, w_t[h0:h0 + 1, :], w_t[h1:h1 + 1, :])
        s_chunk = _dot((xs2.T * w2).astype(_BF), bm[g])
        cd2 = jnp.where(lo_sub, cd_t[h0:h0 + 1, :], cd_t[h1:h1 + 1, :])
        state[sl, :] = sp * cd2 + s_chunk
        ys.append(jnp.where(lo_lane, y0, y1) + yoff + d_ref[:, sl] * xs2)
    y = jnp.concatenate(ys, axis=-1)
    ya_o[...] = _group_rmsnorm_gate(y, za_ref[...], ng_ref[...]).astype(_BF)

    @pl.when(c == nc - 1)
    def _():
        ssm_o[...] = state[...]

    cf = ubuf.shape[1]
    u = glu_ref[:, :cf] * _sigmoid(glu_ref[:, cf:])
    ubuf[uhalo:uhalo + q, :] = u
    acc = jnp.broadcast_to(fb_ref[...], (q, cf))
    for k in range(_CF_KERNEL):
        off = uhalo - (_CF_KERNEL - 1) + k
        acc = acc + fw_ref[k:k + 1, :] * ubuf[off:off + q, :]
    v = _silu(_layer_norm(acc, fg_ref[...], fbeta_ref[...]))
    yb_o[...] = (v * _silu(zb_ref[...])).astype(_BF)

    @pl.when(c == nc - 1)
    def _():
        cc_o[...] = ubuf[uhalo + q - (_CF_KERNEL - 1):uhalo + q, :]

    ubuf[0:uhalo, :] = ubuf[q:q + uhalo, :]


def _even_mix_prompt(za, xbc, dt, glu, zb, params, bsz, seq):
    q = 128
    nc = seq // q
    inner = _SSD_HEADS * _SSD_HEAD_DIM
    conv_dim = xbc.shape[1]
    cf = zb.shape[1]
    row = lambda n: pl.BlockSpec((q, n), lambda b, c: (b * nc + c, 0))
    per_b = lambda r, n: pl.BlockSpec((None, r, n), lambda b, c: (b, 0, 0))
    t = bsz * seq
    return pl.pallas_call(
        functools.partial(_even_mix_kernel, q=q, nc=nc),
        grid=(bsz, nc),
        in_specs=[row(inner), row(conv_dim), row(_LANES), row(2 * cf), row(cf)]
                 + [_const_spec(p.shape) for p in params],
        out_specs=[row(inner), row(cf), per_b(inner, _SSD_STATE),
                   per_b(_SSD_CONV - 1, conv_dim), per_b(_CF_KERNEL - 1, cf)],
        out_shape=[jax.ShapeDtypeStruct((t, inner), _BF), jax.ShapeDtypeStruct((t, cf), _BF),
                   jax.ShapeDtypeStruct((bsz, inner, _SSD_STATE), _F32),
                   jax.ShapeDtypeStruct((bsz, _SSD_CONV - 1, conv_dim), _F32),
                   jax.ShapeDtypeStruct((bsz, _CF_KERNEL - 1, cf), _F32)],
        scratch_shapes=[pltpu.VMEM((8 + q, conv_dim), _F32), pltpu.VMEM((32 + q, cf), _F32),
                        pltpu.VMEM((inner, _SSD_STATE), _F32)],
        compiler_params=_cparams("arbitrary", "arbitrary"),
        name="even_mix_prompt",
    )(za, xbc, dt, glu, zb, *params)


def _sample_even_pre_kernel(xbc_ref, dt_ref, glu_ref, zb_ref, sctx_ref, cctx_ref,
                            cw_ref, cb_ref, dtb_ref, alog_ref, fw_ref, fb_ref, fg_ref, fbeta_ref,
                            xs_o, bm_o, cm_o, dect_o, xdtt_o, yb_o, sctx_o, cctx_o, pad):
    db = xbc_ref.shape[0]
    inner = _SSD_HEADS * _SSD_HEAD_DIM
    gn = _SSD_GROUPS * _SSD_STATE
    xbc = xbc_ref[...]
    acc = cb_ref[...] + cw_ref[_SSD_CONV - 1:_SSD_CONV, :] * xbc
    for k in range(_SSD_CONV - 1):
        acc = acc + cw_ref[k:k + 1, :] * sctx_ref[k]
    xc = _silu(acc)
    xs = xc[:, :inner]
    xs_o[...] = xs
    bm_o[...] = xc[:, inner:inner + gn]
    cm_o[...] = xc[:, inner + gn:inner + 2 * gn]
    for k in range(_SSD_CONV - 2):
        sctx_o[k] = sctx_ref[k + 1]
    sctx_o[_SSD_CONV - 2] = xbc

    lane = lax.broadcasted_iota(jnp.int32, (1, _LANES), 1)
    hmask = lane < _SSD_HEADS
    dt = jnp.where(hmask, _softplus(dt_ref[...] + dtb_ref[...]), 0.0)
    a = jnp.where(hmask, -jnp.exp(alog_ref[...]), 0.0)
    dec = jnp.where(hmask, jnp.exp(dt * a), 0.0)
    er = lax.broadcasted_iota(jnp.int32, (_LANES, inner), 0)
    ec = lax.broadcasted_iota(jnp.int32, (_LANES, inner), 1)
    expand = jnp.where((ec // _SSD_HEAD_DIM) == er, 1.0, 0.0)
    dec_x = _dot_hi(dec, expand)
    xdt = xs * _dot_hi(dt, expand)
    for src, dst in ((dec_x, dect_o), (xdt, xdtt_o)):
        pad[...] = jnp.zeros(pad.shape, _F32)
        pad[0:db, :] = src
        for j in range(inner // _LANES):
            dst[j * _LANES:(j + 1) * _LANES, :] = pad[:, j * _LANES:(j + 1) * _LANES].T

    cf = zb_ref.shape[1]
    u = glu_ref[:, :cf] * _sigmoid(glu_ref[:, cf:])
    acc = fb_ref[...] + fw_ref[_CF_KERNEL - 1:_CF_KERNEL, :] * u
    for k in range(_CF_KERNEL - 1):
        acc = acc + fw_ref[k:k + 1, :] * cctx_ref[k]
    v = _silu(_layer_norm(acc, fg_ref[...], fbeta_ref[...]))
    yb_o[...] = (v * _silu(zb_ref[...])).astype(_BF)
    for k in range(_CF_KERNEL - 2):
        cctx_o[k] = cctx_ref[k + 1]
    cctx_o[_CF_KERNEL - 2] = u


def _sample_even_pre(xbc, dt, glu, zb, sctx_t, cctx_t, params):
    db = xbc.shape[0]
    inner = _SSD_HEADS * _SSD_HEAD_DIM
    gn = _SSD_GROUPS * _SSD_STATE
    cf = zb.shape[1]
    ins = [xbc, dt, glu, zb, sctx_t, cctx_t, *params]
    full = lambda s: pl.BlockSpec(s, lambda i: (0,) * len(s))
    out_shapes = [(db, inner), (db, gn), (db, gn), (inner, _LANES), (inner, _LANES), (db, cf),
                  sctx_t.shape, cctx_t.shape]
    out_dtypes = [_F32, _F32, _F32, _F32, _F32, _BF, _F32, _F32]
    return pl.pallas_call(
        _sample_even_pre_kernel,
        grid=(1,),
        in_specs=[full(a.shape) for a in ins],
        out_specs=[full(s) for s in out_shapes],
        out_shape=[jax.ShapeDtypeStruct(s, d) for s, d in zip(out_shapes, out_dtypes)],
        scratch_shapes=[pltpu.VMEM((_LANES, inner), _F32)],
        compiler_params=_cparams("arbitrary"),
        name="sample_even_pre",
    )(*ins)


def _sample_even_rec_kernel(st_ref, dect_ref, xdtt_ref, bm_ref, cm_ref, xs_ref, za_ref, d_ref, ng_ref,
                            st_o, ya_o):
    b = pl.program_id(0)
    inner = st_ref.shape[0]
    half = inner // _SSD_GROUPS
    sel = jnp.where(lax.broadcasted_iota(jnp.int32, (_LANES, _LANES), 0) == b, 1.0, 0.0)
    dec = _dot_hi(dect_ref[...], sel)
    xd = _dot_hi(xdtt_ref[...], sel)
    grp0 = lax.broadcasted_iota(jnp.int32, (inner, 1), 0) < half
    brow = jnp.where(grp0, bm_ref[:, :_SSD_STATE], bm_ref[:, _SSD_STATE:])
    s_new = st_ref[...] * dec + xd * brow
    st_o[...] = s_new
    r8 = lax.broadcasted_iota(jnp.int32, (8, 1), 0)
    c8 = jnp.where(r8 == 0, cm_ref[:, :_SSD_STATE], jnp.where(r8 == 1, cm_ref[:, _SSD_STATE:], 0.0))
    yr = _dot_nt_hi(c8, s_new)
    lane = lax.broadcasted_iota(jnp.int32, (1, inner), 1)
    xs = xs_ref[...]
    y = jnp.where(lane < half, yr[0:1, :], yr[1:2, :]) + d_ref[...] * xs
    ya_o[...] = _group_rmsnorm_gate(y, za_ref[...], ng_ref[...]).astype(_BF)


def _sample_even_rec(state, dect, xdtt, bm, cm, xs, za, d_x, ng):
    db, inner, n = state.shape
    per_b = lambda s: pl.BlockSpec((None,) + s, lambda b: (b, 0, 0))
    gn = bm.shape[1]
    r3 = lambda a: a.reshape(db, 1, a.shape[1])
    return pl.pallas_call(
        _sample_even_rec_kernel,
        grid=(db,),
        in_specs=[per_b((inner, n)), _const_spec(dect.shape), _const_spec(xdtt.shape),
                  per_b((1, gn)), per_b((1, gn)), per_b((1, inner)), per_b((1, inner)),
                  _const_spec(d_x.shape), _const_spec(ng.shape)],
        out_specs=[per_b((inner, n)), per_b((1, inner))],
        out_shape=[jax.ShapeDtypeStruct((db, inner, n), _F32), jax.ShapeDtypeStruct((db, 1, inner), _BF)],
        compiler_params=_cparams("arbitrary"),
        name="sample_even_rec",
    )(state, dect, xdtt, r3(bm), r3(cm), r3(xs), r3(za), d_x, ng)


def _bisect_threshold(count_ge, rmin, rmax, n_adm, k):
    lo0 = rmin
    hi0 = rmax + (rmax - rmin)
    done0 = n_adm <= k

    def cond(st):
        it, _, _, _, done = st
        return jnp.logical_and(it < _MAX_BISECT, jnp.min(done) < 0.5)

    def body(st):
        it, lo, hi, c_lo, done = st
        mid = lo + (hi - lo) * 0.5
        stuck = jnp.logical_or(mid <= lo, mid >= hi)
        c = count_ge(mid)
        ge = c >= k
        live = done < 0.5
        upd_lo = jnp.logical_and(live, ge)
        upd_hi = jnp.logical_and(live, jnp.logical_not(ge))
        lo_n = jnp.where(upd_lo, mid, lo)
        c_n = jnp.where(upd_lo, c, c_lo)
        hi_n = jnp.where(upd_hi, mid, hi)
        fin = jnp.logical_or(stuck, jnp.logical_and(ge, c <= k))
        done_n = jnp.where(fin, 1.0, done)
        return it + 1, lo_n, hi_n, c_n, done_n

    st = (jnp.int32(0), lo0, hi0, n_adm, jnp.where(done0, 1.0, 0.0))
    _, lo, _, c_lo, _ = lax.while_loop(cond, body, st)
    return lo, c_lo


def _attn_prompt_kernel(qi_ref, wi_ref, q_ref, z_ref, kit_ref, kt_ref, vg_ref, o_ref,
                        sc_ref, qh_ref, acc_ref, m_ref, l_ref, ob_ref, *, tq, tkb, topk):
    i = pl.program_id(1)
    nkb = (i * tq + tq + tkb - 1) // tkb
    row = i * tq + lax.broadcasted_iota(jnp.int32, (tq, 1), 0)
    col0 = lax.broadcasted_iota(jnp.int32, (1, tkb), 1)
    kf = jnp.float32(topk)
    inf = jnp.float32(jnp.inf)

    qi = qi_ref[...]
    w = wi_ref[...] * (_IDX_HEADS ** -0.5 * _IDX_DIM ** -0.5)
    qis = [qi[:, h * _IDX_DIM:(h + 1) * _IDX_DIM] for h in range(_IDX_HEADS)]
    wcs = [w[:, _IDX_DIM + h:_IDX_DIM + h + 1] for h in range(_IDX_HEADS)]

    def score_body(kb, carry):
        rmin, rmax = carry
        kit = kit_ref[kb]
        acc = jnp.zeros((tq, tkb), _F32)
        for h in range(_IDX_HEADS):
            acc = acc + jnp.maximum(_dot(qis[h], kit), 0.0) * wcs[h]
        adm = (kb * tkb + col0) <= row
        sc_ref[kb] = jnp.where(adm, acc, -inf)
        rmax = jnp.maximum(rmax, jnp.max(jnp.where(adm, acc, -inf), axis=1, keepdims=True))
        rmin = jnp.minimum(rmin, jnp.min(jnp.where(adm, acc, inf), axis=1, keepdims=True))
        return rmin, rmax

    rmin, rmax = lax.fori_loop(0, nkb, score_body,
                               (jnp.full((tq, 1), inf, _F32), jnp.full((tq, 1), -inf, _F32)))

    def count_ge(x):
        def cbody(kb, cnt):
            return cnt + jnp.sum(jnp.where(sc_ref[kb] >= x, 1.0, 0.0), axis=1, keepdims=True)
        return lax.fori_loop(0, nkb, cbody, jnp.zeros((tq, 1), _F32))

    n_adm = (row + 1).astype(_F32)
    lo, c_lo = _bisect_threshold(count_ge, rmin, rmax, n_adm, kf)

    @pl.when(jnp.max(c_lo) > kf)
    def _():
        def gbody(kb, cnt):
            return cnt + jnp.sum(jnp.where(sc_ref[kb] > lo, 1.0, 0.0), axis=1, keepdims=True)
        need = kf - lax.fori_loop(0, nkb, gbody, jnp.zeros((tq, 1), _F32))
        tri = jnp.where(lax.broadcasted_iota(jnp.int32, (_LANES, _LANES), 0)
                        <= lax.broadcasted_iota(jnp.int32, (_LANES, _LANES), 1), 1.0, 0.0).astype(_BF)

        def tbody(kb, carry):
            for jb in range(tkb // _LANES):
                s = sc_ref[kb, :, jb * _LANES:(jb + 1) * _LANES]
                eq = s == lo
                pre = _dot(jnp.where(eq, 1.0, 0.0).astype(_BF), tri)
                drop = jnp.logical_and(eq, (carry + pre) > need)
                sc_ref[kb, :, jb * _LANES:(jb + 1) * _LANES] = jnp.where(drop, -inf, s)
                carry = carry + pre[:, _LANES - 1:_LANES]
            return carry
        lax.fori_loop(0, nkb, tbody, jnp.zeros((tq, 1), _F32))

    q = q_ref[...]
    for h in range(_ATTN_HEADS):
        qh_ref[h] = q[:, h * _ATTN_HEAD_DIM:(h + 1) * _ATTN_HEAD_DIM]
    m_ref[...] = jnp.full(m_ref.shape, _NEG, _F32)
    l_ref[...] = jnp.zeros(l_ref.shape, _F32)
    acc_ref[...] = jnp.zeros(acc_ref.shape, _F32)
    rep = _ATTN_HEADS // _ATTN_KV_HEADS

    def attn_body(kb, carry):
        bias = jnp.where(sc_ref[kb] >= lo, 0.0, _NEG)
        start = pl.multiple_of(kb * tkb, tkb)
        for g in range(_ATTN_KV_HEADS):
            kg = kt_ref[kb, g * _ATTN_HEAD_DIM:(g + 1) * _ATTN_HEAD_DIM, :]
            vg = vg_ref[g, pl.ds(start, tkb), :]
            for r in range(rep):
                h = g * rep + r
                s = _dot(qh_ref[h], kg) + bias
                m_old = m_ref[h]
                m_new = jnp.maximum(m_old, jnp.max(s, axis=1, keepdims=True))
                alpha = jnp.exp(m_old - m_new)
                p = jnp.exp(s - m_new)
                l_ref[h] = l_ref[h] * alpha + jnp.sum(p, axis=1, keepdims=True)
                acc_ref[h] = acc_ref[h] * alpha + _dot(p.astype(_BF), vg)
                m_ref[h] = m_new
        return carry

    lax.fori_loop(0, nkb, attn_body, 0)
    for h in range(_ATTN_HEADS):
        ob_ref[:, h * _ATTN_HEAD_DIM:(h + 1) * _ATTN_HEAD_DIM] = acc_ref[h] / l_ref[h]
    o_ref[...] = (ob_ref[...] * _silu(z_ref[...])).astype(_BF)


def _attn_prompt(qi, wi, q, z, kit, kt, vg, bsz, seq, tkb):
    tq = 128
    nq = seq // tq
    nkbt = seq // tkb
    t = bsz * seq
    topk = min(_TOPK_MAX, seq // 4)
    width = q.shape[1]
    row = lambda n: pl.BlockSpec((tq, n), lambda b, i: (b * nq + i, 0))
    per_b = lambda s: pl.BlockSpec((None,) + s, lambda b, i: (b,) + (0,) * len(s),
                                   pipeline_mode=pl.Buffered(1))
    return pl.pallas_call(
        functools.partial(_attn_prompt_kernel, tq=tq, tkb=tkb, topk=topk),
        grid=(bsz, nq),
        in_specs=[row(qi.shape[1]), row(_LANES), row(width), row(width),
                  per_b(kit.shape[1:]), per_b(kt.shape[1:]), per_b(vg.shape[1:])],
        out_specs=row(width),
        out_shape=jax.ShapeDtypeStruct((t, width), _BF),
        scratch_shapes=[pltpu.VMEM((nkbt, tq, tkb), _F32),
                        pltpu.VMEM((_ATTN_HEADS, tq, _ATTN_HEAD_DIM), _BF),
                        pltpu.VMEM((_ATTN_HEADS, tq, _ATTN_HEAD_DIM), _F32),
                        pltpu.VMEM((_ATTN_HEADS, tq, 1), _F32),
                        pltpu.VMEM((_ATTN_HEADS, tq, 1), _F32),
                        pltpu.VMEM((tq, width), _F32)],
        compiler_params=_cparams("arbitrary", "arbitrary"),
        name="attn_prompt",
    )(qi, wi, q, z, kit, kt, vg)


def _attn_sample_kernel(pt_ref, qi_ref, w_ref, kin_ref, q_ref, kn_ref, vn_ref,
                        cki_ref, ck_ref, cv_ref, o_ref,
                        kibuf, kvbuf, sc_ref, sn_ref, lg_ref, sem, *, n_pages, pages_per_chunk, topk):
    b = pl.program_id(0)
    past = n_pages * _PAGE
    cw = pages_per_chunk * _PAGE
    n_chunks = n_pages // pages_per_chunk
    kf = jnp.float32(topk)
    inf = jnp.float32(jnp.inf)

    def page_copy(cache_ref, page, dst_ref, slot):
        return pltpu.make_async_copy(cache_ref.at[pt_ref[b, page]],
                                     dst_ref.at[pl.ds(slot * _PAGE, _PAGE)], sem.at[0])

    def fetch(cache_ref, first_page, count, dst_ref):
        for p in range(count):
            page_copy(cache_ref, first_page + p, dst_ref, p).start()
        for p in range(count):
            page_copy(cache_ref, first_page + p, dst_ref, p).wait()

    fetch(cki_ref, 0, n_pages, kibuf)
    qi8 = qi_ref[...].astype(_BF)
    w8 = w_ref[...] * (_IDX_HEADS ** -0.5 * _IDX_DIM ** -0.5)
    s8 = _dot_nt(qi8, kibuf[...].astype(_BF))
    sc_ref[...] = jnp.sum(jnp.maximum(s8, 0.0) * w8, axis=0, keepdims=True)
    kin = kin_ref[...].astype(_BF).astype(_F32)
    sn8 = jnp.sum(qi8.astype(_F32) * kin, axis=1, keepdims=True)
    sn_ref[...] = jnp.sum(jnp.maximum(sn8, 0.0) * w8, axis=0, keepdims=True)
    snew = sn_ref[...]

    if past + 1 > topk:
        sc = sc_ref[...]
        rmax = jnp.maximum(jnp.max(sc, axis=1, keepdims=True), snew)
        rmin = jnp.minimum(jnp.min(sc, axis=1, keepdims=True), snew)

        def count_ge(x):
            return (jnp.sum(jnp.where(sc_ref[...] >= x, 1.0, 0.0), axis=1, keepdims=True)
                    + jnp.where(snew >= x, 1.0, 0.0))

        lo, c_lo = _bisect_threshold(count_ge, rmin, rmax, jnp.full((1, 1), past + 1.0, _F32), kf)

        @pl.when(jnp.max(c_lo) > kf)
        def _():
            need = kf - (jnp.sum(jnp.where(sc_ref[...] > lo, 1.0, 0.0), axis=1, keepdims=True)
                         + jnp.where(snew > lo, 1.0, 0.0))
            tri = jnp.where(lax.broadcasted_iota(jnp.int32, (_LANES, _LANES), 0)
                            <= lax.broadcasted_iota(jnp.int32, (_LANES, _LANES), 1), 1.0, 0.0).astype(_BF)
            carry = jnp.zeros((1, 1), _F32)
            for jb in range(past // _LANES):
                s = sc_ref[:, jb * _LANES:(jb + 1) * _LANES]
                eq = s == lo
                eq8 = jnp.broadcast_to(jnp.where(eq, 1.0, 0.0), (8, _LANES)).astype(_BF)
                pre = _dot(eq8, tri)[0:1, :]
                drop = jnp.logical_and(eq, (carry + pre) > need)
                sc_ref[:, jb * _LANES:(jb + 1) * _LANES] = jnp.where(drop, -inf, s)
                carry = carry + pre[:, _LANES - 1:_LANES]
            drop_new = jnp.logical_and(snew == lo, (carry + 1.0) > need)
            sn_ref[...] = jnp.where(drop_new, -inf, snew)
    else:
        lo = jnp.full((1, 1), -inf, _F32)

    kvw = _ATTN_KV_HEADS * _ATTN_HEAD_DIM
    rep = _ATTN_HEADS // _ATTN_KV_HEADS
    rr = lax.broadcasted_iota(jnp.int32, (_ATTN_HEAD_DIM, kvw), 0)
    rc = lax.broadcasted_iota(jnp.int32, (_ATTN_HEAD_DIM, kvw), 1)
    spread = jnp.where((rc % _ATTN_HEAD_DIM) == rr, 1.0, 0.0)
    hr = lax.broadcasted_iota(jnp.int32, (_ATTN_HEADS, kvw), 0)
    hc = lax.broadcasted_iota(jnp.int32, (_ATTN_HEADS, kvw), 1)
    own = (hc // _ATTN_HEAD_DIM) == (hr // rep)
    qbd = jnp.where(own, _dot_hi(q_ref[...], spread), 0.0).astype(_BF)
    for c in range(n_chunks):
        fetch(ck_ref, c * pages_per_chunk, pages_per_chunk, kvbuf)
        lg_ref[:, c * cw:(c + 1) * cw] = _dot_nt(qbd, kvbuf[...].astype(_BF))
    kn = kn_ref[...].astype(_BF).astype(_F32)
    lnew = jnp.sum(qbd.astype(_F32) * kn, axis=1, keepdims=True)

    bias = jnp.where(sc_ref[...] >= lo, 0.0, _NEG)
    lgm = lg_ref[...] + bias
    lnew = lnew + jnp.where(sn_ref[...] >= lo, 0.0, _NEG)
    m = jnp.maximum(jnp.max(lgm, axis=1, keepdims=True), lnew)
    p = jnp.exp(lgm - m)
    pn = jnp.exp(lnew - m)
    den = jnp.sum(p, axis=1, keepdims=True) + pn
    lg_ref[...] = p
    vn = vn_ref[...].astype(_BF).astype(_F32)
    out = pn.astype(_BF).astype(_F32) * vn
    for c in range(n_chunks):
        fetch(cv_ref, c * pages_per_chunk, pages_per_chunk, kvbuf)
        out = out + _dot(lg_ref[:, c * cw:(c + 1) * cw].astype(_BF), kvbuf[...].astype(_BF))
    out = jnp.where(own, out / den, 0.0)
    gather = jnp.where((lax.broadcasted_iota(jnp.int32, (kvw, _ATTN_HEAD_DIM), 0) % _ATTN_HEAD_DIM)
                       == lax.broadcasted_iota(jnp.int32, (kvw, _ATTN_HEAD_DIM), 1), 1.0, 0.0)
    o_ref[...] = _dot_hi(out, gather)


def _attn_sample(page_table, qi, wi, ki_new, q, k_new, v_new, cache_ki, cache_k, cache_v):
    db, n_pages = page_table.shape
    past = n_pages * _PAGE
    topk = min(_TOPK_MAX, (past + 1) // 4)
    kvw = _ATTN_KV_HEADS * _ATTN_HEAD_DIM
    pages_per_chunk = min(16, n_pages)
    per_b = lambda s: pl.BlockSpec((None,) + s, lambda b, pt: (b,) + (0,) * len(s))
    anyspec = pl.BlockSpec(memory_space=pl.ANY)
    ins = [qi.reshape(db, _IDX_HEADS, _IDX_DIM),
           wi[:, _IDX_DIM:_IDX_DIM + _IDX_HEADS].reshape(db, _IDX_HEADS, 1),
           ki_new.reshape(db, 1, _IDX_DIM),
           q.reshape(db, _ATTN_HEADS, _ATTN_HEAD_DIM),
           k_new.reshape(db, 1, kvw), v_new.reshape(db, 1, kvw),
           cache_ki, cache_k.reshape(cache_k.shape[0], _PAGE, kvw), cache_v.reshape(cache_v.shape[0], _PAGE, kvw)]
    grid_spec = pltpu.PrefetchScalarGridSpec(
        num_scalar_prefetch=1,
        grid=(db,),
        in_specs=[per_b((_IDX_HEADS, _IDX_DIM)), per_b((_IDX_HEADS, 1)), per_b((1, _IDX_DIM)),
                  per_b((_ATTN_HEADS, _ATTN_HEAD_DIM)), per_b((1, kvw)), per_b((1, kvw)),
                  anyspec, anyspec, anyspec],
        out_specs=per_b((_ATTN_HEADS, _ATTN_HEAD_DIM)),
        scratch_shapes=[pltpu.VMEM((past, _IDX_DIM), _F32),
                        pltpu.VMEM((pages_per_chunk * _PAGE, kvw), _F32),
                        pltpu.VMEM((1, past), _F32), pltpu.VMEM((1, 1), _F32),
                        pltpu.VMEM((_ATTN_HEADS, past), _F32),
                        pltpu.SemaphoreType.DMA((1,))],
    )
    o = pl.pallas_call(
        functools.partial(_attn_sample_kernel, n_pages=n_pages, pages_per_chunk=pages_per_chunk, topk=topk),
        grid_spec=grid_spec,
        out_shape=jax.ShapeDtypeStruct((db, _ATTN_HEADS, _ATTN_HEAD_DIM), _F32),
        compiler_params=_cparams("arbitrary"),
        name="attn_sample",
    )(page_table, *ins)
    return o.reshape(db, _ATTN_HEADS * _ATTN_HEAD_DIM)


def _pad_lanes(a, n=_LANES):
    return jnp.pad(a, [(0, 0)] * (a.ndim - 1) + [(0, n - a.shape[-1])])


def _even_weights(w_in, conv_w, conv_b, dt_bias, a_log, d_skip, norm_g, cf_w, cf_b, cf_g, cf_beta, w_out):
    inner = _SSD_HEADS * _SSD_HEAD_DIM
    conv_dim = conv_w.shape[1]
    cf = cf_w.shape[1]
    o1, o2, o3, o4 = inner, inner + conv_dim, inner + conv_dim + _SSD_HEADS, inner + conv_dim + _SSD_HEADS + 2 * cf
    wb = w_in.astype(_BF)
    proj = (wb[:, :o1], wb[:, o1:o2], _pad_lanes(wb[:, o2:o3]), wb[:, o3:o4], wb[:, o4:])
    row = lambda v: v.reshape(1, -1)
    ssd = (conv_w, row(conv_b), _pad_lanes(row(dt_bias)), _pad_lanes(row(a_log)))
    d_x = row(jnp.repeat(d_skip, _SSD_HEAD_DIM))
    cfp = (cf_w, row(cf_b), row(cf_g), row(cf_beta))
    wo = w_out.astype(_BF)
    return proj, ssd, d_x, row(norm_g), cfp, (wo[:inner], wo[inner:])


def _even_layer_prompt(x, wts, ln_g, ln_b, bsz, seq):
    proj, ssd, d_x, ng, cfp, wo = wts
    za, xbc, dt, glu, zb = _even_proj(x, proj)
    ya, yb, ssm, sc, cc = _even_mix_prompt(za, xbc, dt, glu, zb, (*ssd, d_x, ng, *cfp), bsz, seq)
    x_new = _outproj_ln(x, [ya, yb], wo, ln_g, ln_b)
    return x_new, ssm.reshape(bsz, _SSD_HEADS, _SSD_HEAD_DIM, _SSD_STATE), sc, cc


def _even_layer_sample(x, st_ssm, st_sconv, st_cconv, wts, ln_g, ln_b):
    proj, ssd, d_x, ng, cfp, wo = wts
    db = x.shape[0]
    inner = _SSD_HEADS * _SSD_HEAD_DIM
    za, xbc, dt, glu, zb = _even_proj(x, proj)
    xs, bm, cm, dect, xdtt, yb, sctx_n, cctx_n = _sample_even_pre(
        xbc, dt, glu, zb, jnp.swapaxes(st_sconv, 0, 1), jnp.swapaxes(st_cconv, 0, 1), (*ssd, *cfp))
    st_new, ya = _sample_even_rec(st_ssm.reshape(db, inner, _SSD_STATE), dect, xdtt, bm, cm, xs, za, d_x, ng)
    x_new = _outproj_ln(x, [ya.reshape(db, inner), yb], wo, ln_g, ln_b)
    return (x_new, st_new.reshape(st_ssm.shape), jnp.swapaxes(sctx_n, 0, 1), jnp.swapaxes(cctx_n, 0, 1))


def _odd_weights(w_in, w_out):
    aw = _ATTN_HEADS * _ATTN_HEAD_DIM
    kvw = _ATTN_KV_HEADS * _ATTN_HEAD_DIM
    iw = _IDX_HEADS * _IDX_DIM
    o1, o2, o3, o4 = aw, aw + kvw, aw + 2 * kvw, aw + 2 * kvw + iw
    o5 = o4 + _IDX_DIM + _IDX_HEADS
    wb = w_in.astype(_BF)
    ws = (wb[:, :o1], wb[:, o1:o2], wb[:, o2:o3], wb[:, o3:o4], _pad_lanes(wb[:, o4:o5]), wb[:, o5:])
    wts = (wb[:, o1:o2].T, wb[:, o4:o4 + _IDX_DIM].T)
    return ws, wts, w_out.astype(_BF)


def _odd_layer_prompt(x, wts, ln_g, ln_b, bsz, seq):
    ws, wtr, wo = wts
    tkb = min(512, seq)
    q, k, v, qi, ki, wi, z, kt, kit, vg = _odd_proj_prompt(x, ws, wtr, bsz, seq, tkb)
    o = _attn_prompt(qi, wi, q, z, kit, kt, vg, bsz, seq, tkb)
    x_new = _outproj_ln(x, [o], [wo], ln_g, ln_b)
    return (x_new, k.reshape(bsz, seq, _ATTN_KV_HEADS, _ATTN_HEAD_DIM),
            v.reshape(bsz, seq, _ATTN_KV_HEADS, _ATTN_HEAD_DIM), ki.reshape(bsz, seq, _IDX_DIM))


def _odd_layer_sample(x, cache_k, cache_v, cache_ki, page_table, wts, ln_g, ln_b):
    ws, _, wo = wts
    db = x.shape[0]
    q, k, v, qi, ki, wi, z = _odd_proj_sample(x, ws)
    o = _attn_sample(page_table, qi, wi, ki, q, k, v, cache_ki, cache_k, cache_v)
    x_new = _outproj_ln(x, [o], [wo], ln_g, ln_b, z=z)
    return (x_new, k.reshape(db, 1, _ATTN_KV_HEADS, _ATTN_HEAD_DIM),
            v.reshape(db, 1, _ATTN_KV_HEADS, _ATTN_HEAD_DIM), ki.reshape(db, 1, _IDX_DIM))


def kernel(x_prompt, x_sample, state_ssm_l0, state_ssdconv_l0, state_cfconv_l0, cache_k_l1, cache_v_l1, cache_kidx_l1, state_ssm_l2, state_ssdconv_l2, state_cfconv_l2, cache_k_l3, cache_v_l3, cache_kidx_l3, page_table, w_in_even, ssd_conv_w, ssd_conv_b, ssd_dt_bias, ssd_a_log, ssd_d, ssd_norm_g, cf_dw_w, cf_dw_b, cf_ln_g, cf_ln_b, w_out_even, w_in_odd, w_out_odd, ln_g, ln_b):
    bsz, seq, d = x_prompt.shape
    db = x_sample.shape[0]
    ssm_states = (state_ssm_l0, state_ssm_l2)
    sconv_states = (state_ssdconv_l0, state_ssdconv_l2)
    cconv_states = (state_cfconv_l0, state_cfconv_l2)
    k_caches = (cache_k_l1, cache_k_l3)
    v_caches = (cache_v_l1, cache_v_l3)
    ki_caches = (cache_kidx_l1, cache_kidx_l3)
    yp = x_prompt.reshape(bsz * seq, d)
    ys = x_sample.reshape(db, d)
    new_state = []
    for layer in range(_DEPTH):
        j = layer // 2
        g, b = ln_g[layer].reshape(1, d), ln_b[layer].reshape(1, d)
        if layer % 2 == 0:
            wts = _even_weights(w_in_even[j], ssd_conv_w[j], ssd_conv_b[j], ssd_dt_bias[j], ssd_a_log[j],
                                ssd_d[j], ssd_norm_g[j], cf_dw_w[j], cf_dw_b[j], cf_ln_g[j], cf_ln_b[j],
                                w_out_even[j])
            yp, ssm_p, sc_p, cc_p = _even_layer_prompt(yp, wts, g, b, bsz, seq)
            ys, ssm_s, sc_s, cc_s = _even_layer_sample(ys, ssm_states[j], sconv_states[j], cconv_states[j],
                                                       wts, g, b)
            new_state += [ssm_p, ssm_s, sc_p, sc_s, cc_p, cc_s]
        else:
            wts = _odd_weights(w_in_odd[j], w_out_odd[j])
            yp, k_p, v_p, ki_p = _odd_layer_prompt(yp, wts, g, b, bsz, seq)
            ys, k_s, v_s, ki_s = _odd_layer_sample(ys, k_caches[j], v_caches[j], ki_caches[j], page_table,
                                                   wts, g, b)
            new_state += [k_p, k_s, v_p, v_s, ki_p, ki_s]
    return (yp.reshape(bsz, seq, d), ys.reshape(db, 1, d), *new_state)
```

```python
import functools

import jax
import jax.numpy as jnp
from jax import lax
from jax.experimental import pallas as pl
from jax.experimental.pallas import tpu as pltpu

_BF = jnp.bfloat16
_F32 = jnp.float32
_HI = lax.Precision.HIGHEST

_SSD_HEADS = 16
_SSD_HEAD_DIM = 64
_SSD_GROUPS = 2
_SSD_STATE = 128
_SSD_CONV = 4
_CF_KERNEL = 31
_ATTN_HEADS = 16
_ATTN_KV_HEADS = 4
_ATTN_HEAD_DIM = 64
_IDX_HEADS = 8
_IDX_DIM = 64
_TOPK_MAX = 256
_PAGE = 128
_DEPTH = 4
_ALPHA = (2 * _DEPTH) ** 0.25
_EPS = 1e-5

_LANES = 128
_VMEM_LIMIT = 56 * 1024 * 1024
_NEG = -1e30
_MAX_BISECT = 200


def _dot(a, b):
    return jnp.dot(a, b, preferred_element_type=_F32)


def _dot_nt(a, b):
    return lax.dot_general(a, b, (((1,), (1,)), ((), ())), preferred_element_type=_F32)


def _dot_hi(a, b):
    return jnp.dot(a, b, precision=_HI, preferred_element_type=_F32)


def _dot_nt_hi(a, b):
    return lax.dot_general(a, b, (((1,), (1,)), ((), ())), precision=_HI,
                           preferred_element_type=_F32)


def _sigmoid(x):
    return 1.0 / (1.0 + jnp.exp(-x))


def _silu(x):
    return x * _sigmoid(x)


def _softplus(x):
    return jnp.maximum(x, 0.0) + jnp.log1p(jnp.exp(-jnp.abs(x)))


def _layer_norm(x, g, b):
    mu = jnp.mean(x, axis=-1, keepdims=True)
    xc = x - mu
    var = jnp.mean(xc * xc, axis=-1, keepdims=True)
    return xc * lax.rsqrt(var + _EPS) * g + b


def _cparams(*sem):
    return pltpu.CompilerParams(dimension_semantics=sem, vmem_limit_bytes=_VMEM_LIMIT)


def _const_spec(shape):
    return pl.BlockSpec(shape, lambda *_: (0,) * len(shape), pipeline_mode=pl.Buffered(1))


def _even_proj_kernel(x_ref, wza, wxbc, wdt, wglu, wzb, za_o, xbc_o, dt_o, glu_o, zb_o):
    xb = x_ref[...].astype(_BF)
    za_o[...] = _dot(xb, wza[...])
    xbc_o[...] = _dot(xb, wxbc[...])
    dt_o[...] = _dot(xb, wdt[...])
    glu_o[...] = _dot(xb, wglu[...])
    zb_o[...] = _dot(xb, wzb[...])


def _even_proj(x, ws):
    t, d = x.shape
    tm = min(256, t)
    widths = [w.shape[1] for w in ws]
    row = lambda n: pl.BlockSpec((tm, n), lambda i: (i, 0))
    return pl.pallas_call(
        _even_proj_kernel,
        grid=(t // tm,),
        in_specs=[row(d)] + [_const_spec(w.shape) for w in ws],
        out_specs=[row(n) for n in widths],
        out_shape=[jax.ShapeDtypeStruct((t, n), _F32) for n in widths],
        compiler_params=_cparams("arbitrary"),
        name="even_proj",
    )(x, *ws)


def _odd_proj_prompt_kernel(x_ref, wq, wk, wv, wqi, wkw, wz, wkT, wkiT,
                            q_o, k_o, v_o, qi_o, ki_o, wi_o, z_o, kT_o, kiT_o, vg_o):
    xb = x_ref[...].astype(_BF)
    q_o[...] = (_dot(xb, wq[...]) * (_ATTN_HEAD_DIM ** -0.5)).astype(_BF)
    k_o[...] = _dot(xb, wk[...])
    v = _dot(xb, wv[...])
    v_o[...] = v
    for g in range(_ATTN_KV_HEADS):
        vg_o[g] = v[:, g * _ATTN_HEAD_DIM:(g + 1) * _ATTN_HEAD_DIM].astype(_BF)
    qi_o[...] = _dot(xb, wqi[...]).astype(_BF)
    kw = _dot(xb, wkw[...])
    ki_o[...] = kw[:, :_IDX_DIM]
    wi_o[...] = kw
    z_o[...] = _dot(xb, wz[...])
    kT_o[...] = _dot_nt(wkT[...], xb).astype(_BF)
    kiT_o[...] = _dot_nt(wkiT[...], xb).astype(_BF)


def _odd_proj_prompt(x, ws, wts, bsz, seq, tkb):
    t, d = x.shape
    tm = tkb
    nb = seq // tm
    wq, wk, wv, wqi, wkw, wz = ws
    wkT, wkiT = wts
    kvw = wk.shape[1]
    row = lambda n: pl.BlockSpec((tm, n), lambda i: (i, 0))
    outs = [
        (jax.ShapeDtypeStruct((t, wq.shape[1]), _BF), row(wq.shape[1])),
        (jax.ShapeDtypeStruct((t, kvw), _F32), row(kvw)),
        (jax.ShapeDtypeStruct((t, kvw), _F32), row(kvw)),
        (jax.ShapeDtypeStruct((t, wqi.shape[1]), _BF), row(wqi.shape[1])),
        (jax.ShapeDtypeStruct((t, _IDX_DIM), _F32), row(_IDX_DIM)),
        (jax.ShapeDtypeStruct((t, _LANES), _F32), row(_LANES)),
        (jax.ShapeDtypeStruct((t, wz.shape[1]), _F32), row(wz.shape[1])),
        (jax.ShapeDtypeStruct((bsz, nb, kvw, tm), _BF),
         pl.BlockSpec((None, None, kvw, tm), lambda i: (i // nb, i % nb, 0, 0))),
        (jax.ShapeDtypeStruct((bsz, nb, _IDX_DIM, tm), _BF),
         pl.BlockSpec((None, None, _IDX_DIM, tm), lambda i: (i // nb, i % nb, 0, 0))),
        (jax.ShapeDtypeStruct((bsz, _ATTN_KV_HEADS, seq, _ATTN_HEAD_DIM), _BF),
         pl.BlockSpec((None, _ATTN_KV_HEADS, tm, _ATTN_HEAD_DIM), lambda i: (i // nb, 0, i % nb, 0))),
    ]
    return pl.pallas_call(
        _odd_proj_prompt_kernel,
        grid=(t // tm,),
        in_specs=[row(d)] + [_const_spec(w.shape) for w in (*ws, *wts)],
        out_specs=[o[1] for o in outs],
        out_shape=[o[0] for o in outs],
        compiler_params=_cparams("arbitrary"),
        name="odd_proj_prompt",
    )(x, *ws, *wts)


def _odd_proj_sample_kernel(x_ref, wq, wk, wv, wqi, wkw, wz, q_o, k_o, v_o, qi_o, ki_o, wi_o, z_o):
    xb = x_ref[...].astype(_BF)
    q_o[...] = _dot(xb, wq[...]) * (_ATTN_HEAD_DIM ** -0.5)
    k_o[...] = _dot(xb, wk[...])
    v_o[...] = _dot(xb, wv[...])
    qi_o[...] = _dot(xb, wqi[...])
    kw = _dot(xb, wkw[...])
    ki_o[...] = kw[:, :_IDX_DIM]
    wi_o[...] = kw
    z_o[...] = _dot(xb, wz[...])


def _odd_proj_sample(x, ws):
    t, d = x.shape
    wq, wk, wv, wqi, wkw, wz = ws
    widths = [wq.shape[1], wk.shape[1], wv.shape[1], wqi.shape[1], _IDX_DIM, _LANES, wz.shape[1]]
    full = lambda n: pl.BlockSpec((t, n), lambda i: (0, 0))
    return pl.pallas_call(
        _odd_proj_sample_kernel,
        grid=(1,),
        in_specs=[full(d)] + [_const_spec(w.shape) for w in ws],
        out_specs=[full(n) for n in widths],
        out_shape=[jax.ShapeDtypeStruct((t, n), _F32) for n in widths],
        compiler_params=_cparams("arbitrary"),
        name="odd_proj_sample",
    )(x, *ws)


def _outproj_kernel(*refs, n_in, gated):
    x_ref = refs[0]
    a_refs = refs[1:1 + n_in]
    pos = 1 + n_in
    z_ref = refs[pos] if gated else None
    pos += 1 if gated else 0
    w_refs = refs[pos:pos + n_in]
    g_ref, b_ref, o_ref = refs[pos + n_in:pos + n_in + 3]
    acc = _ALPHA * x_ref[...]
    for a_ref, w_ref in zip(a_refs, w_refs):
        a = a_ref[...]
        if gated:
            a = a.astype(_F32) * _silu(z_ref[...])
        acc = acc + _dot(a.astype(_BF), w_ref[...])
    o_ref[...] = _layer_norm(acc, g_ref[...], b_ref[...])


def _outproj_ln(x, a_list, w_list, g, b, z=None):
    t, d = x.shape
    tm = min(512, t)
    row = lambda n: pl.BlockSpec((tm, n), lambda i: (i, 0))
    gated = z is not None
    ins = [x, *a_list] + ([z] if gated else []) + [*w_list, g, b]
    specs = ([row(d)] + [row(a.shape[1]) for a in a_list] + ([row(z.shape[1])] if gated else [])
             + [_const_spec(w.shape) for w in w_list] + [_const_spec(g.shape), _const_spec(b.shape)])
    return pl.pallas_call(
        functools.partial(_outproj_kernel, n_in=len(a_list), gated=gated),
        grid=(t // tm,),
        in_specs=specs,
        out_specs=row(d),
        out_shape=jax.ShapeDtypeStruct((t, d), _F32),
        compiler_params=_cparams("arbitrary"),
        name="outproj_ln",
    )(*ins)


def _group_rmsnorm_gate(y, za, ng):
    y = y * _silu(za)
    gw = y.shape[1] // _SSD_GROUPS
    parts = []
    for g in range(_SSD_GROUPS):
        yg = y[:, g * gw:(g + 1) * gw]
        ms = jnp.mean(yg * yg, axis=-1, keepdims=True)
        parts.append(yg * lax.rsqrt(ms + _EPS))
    return jnp.concatenate(parts, axis=-1) * ng


def _even_mix_kernel(za_ref, xbc_ref, dt_ref, glu_ref, zb_ref,
                     cw_ref, cb_ref, dtb_ref, alog_ref, d_ref, ng_ref,
                     fw_ref, fb_ref, fg_ref, fbeta_ref,
                     ya_o, yb_o, ssm_o, sc_o, cc_o,
                     xbuf, ubuf, state, *, q, nc):
    c = pl.program_id(1)
    inner = _SSD_HEADS * _SSD_HEAD_DIM
    gn = _SSD_GROUPS * _SSD_STATE
    xhalo = 8
    uhalo = 32

    @pl.when(c == 0)
    def _():
        xbuf[0:xhalo, :] = jnp.zeros((xhalo, xbuf.shape[1]), _F32)
        ubuf[0:uhalo, :] = jnp.zeros((uhalo, ubuf.shape[1]), _F32)
        state[...] = jnp.zeros(state.shape, _F32)

    xbuf[xhalo:xhalo + q, :] = xbc_ref[...]
    acc = jnp.broadcast_to(cb_ref[...], (q, xbuf.shape[1]))
    for k in range(_SSD_CONV):
        off = xhalo - (_SSD_CONV - 1) + k
        acc = acc + cw_ref[k:k + 1, :] * xbuf[off:off + q, :]
    xc = _silu(acc)
    xs = xc[:, :inner]
    bm = [xc[:, inner + g * _SSD_STATE: inner + (g + 1) * _SSD_STATE].astype(_BF) for g in range(_SSD_GROUPS)]
    cm = [xc[:, inner + gn + g * _SSD_STATE: inner + gn + (g + 1) * _SSD_STATE].astype(_BF)
          for g in range(_SSD_GROUPS)]

    @pl.when(c == nc - 1)
    def _():
        sc_o[...] = xbuf[xhalo + q - (_SSD_CONV - 1):xhalo + q, :]

    xbuf[0:xhalo, :] = xbuf[q:q + xhalo, :]

    lane = lax.broadcasted_iota(jnp.int32, (1, _LANES), 1)
    hmask = lane < _SSD_HEADS
    dt = jnp.where(hmask, _softplus(dt_ref[...] + dtb_ref[...]), 0.0)
    a = jnp.where(hmask, -jnp.exp(alog_ref[...]), 0.0)
    ri = lax.broadcasted_iota(jnp.int32, (q, q), 0)
    ci = lax.broadcasted_iota(jnp.int32, (q, q), 1)
    trilb = ri >= ci
    a_cum = _dot_hi(trilb.astype(_F32), dt * a)
    a_cum_t = a_cum.T
    dt_t = dt.T
    a_last_t = a_cum_t[:, q - 1:q]
    w_t = dt_t * jnp.exp(a_last_t - a_cum_t)
    ea = jnp.exp(a_cum)
    cd_t = jnp.exp(a_last_t)
    lo_lane = lane < _SSD_HEAD_DIM
    lo_sub = lax.broadcasted_iota(jnp.int32, (_LANES, 1), 0) < _SSD_HEAD_DIM

    cb = [_dot_nt(cm[g], bm[g]) for g in range(_SSD_GROUPS)]
    heads_per_group = _SSD_HEADS // _SSD_GROUPS
    ys = []
    for j in range(_SSD_HEADS // 2):
        h0, h1 = 2 * j, 2 * j + 1
        g = h0 // heads_per_group
        sl = slice(j * _LANES, (j + 1) * _LANES)
        xs2 = xs[:, sl]
        x2 = (xs2 * jnp.where(lo_lane, dt[:, h0:h0 + 1], dt[:, h1:h1 + 1])).astype(_BF)
        l0 = jnp.where(trilb, jnp.exp(a_cum[:, h0:h0 + 1] - a_cum_t[h0:h0 + 1, :]), 0.0)
        l1 = jnp.where(trilb, jnp.exp(a_cum[:, h1:h1 + 1] - a_cum_t[h1:h1 + 1, :]), 0.0)
        y0 = _dot((cb[g] * l0).astype(_BF), x2)
        y1 = _dot((cb[g] * l1).astype(_BF), x2)
        sp = state[sl, :]
        yoff = _dot_nt(cm[g], sp.astype(_BF)) * jnp.where(lo_lane, ea[:, h0:h0 + 1], ea[:, h1:h1 + 1])
        w2 = jnp.where(lo_sub, w_t[h0:h0 + 1, :], w_t[h1:h1 + 1, :])
        s_chunk = _dot((xs2.T * w2).astype(_BF), bm[g])
        cd2 = jnp.where(lo_sub, cd_t[h0:h0 + 1, :], cd_t[h1:h1 + 1, :])
        state[sl, :] = sp * cd2 + s_chunk
        ys.append(jnp.where(lo_lane, y0, y1) + yoff + d_ref[:, sl] * xs2)
    y = jnp.concatenate(ys, axis=-1)
    ya_o[...] = _group_rmsnorm_gate(y, za_ref[...], ng_ref[...]).astype(_BF)

    @pl.when(c == nc - 1)
    def _():
        ssm_o[...] = state[...]

    cf = ubuf.shape[1]
    u = glu_ref[:, :cf] * _sigmoid(glu_ref[:, cf:])
    ubuf[uhalo:uhalo + q, :] = u
    acc = jnp.broadcast_to(fb_ref[...], (q, cf))
    for k in range(_CF_KERNEL):
        off = uhalo - (_CF_KERNEL - 1) + k
        acc = acc + fw_ref[k:k + 1, :] * ubuf[off:off + q, :]
    v = _silu(_layer_norm(acc, fg_ref[...], fbeta_ref[...]))
    yb_o[...] = (v * _silu(zb_ref[...])).astype(_BF)

    @pl.when(c == nc - 1)
    def _():
        cc_o[...] = ubuf[uhalo + q - (_CF_KERNEL - 1):uhalo + q, :]

    ubuf[0:uhalo, :] = ubuf[q:q + uhalo, :]


def _even_mix_prompt(za, xbc, dt, glu, zb, params, bsz, seq):
    q = 128
    nc = seq // q
    inner = _SSD_HEADS * _SSD_HEAD_DIM
    conv_dim = xbc.shape[1]
    cf = zb.shape[1]
    row = lambda n: pl.BlockSpec((q, n), lambda b, c: (b * nc + c, 0))
    per_b = lambda r, n: pl.BlockSpec((None, r, n), lambda b, c: (b, 0, 0))
    t = bsz * seq
    return pl.pallas_call(
        functools.partial(_even_mix_kernel, q=q, nc=nc),
        grid=(bsz, nc),
        in_specs=[row(inner), row(conv_dim), row(_LANES), row(2 * cf), row(cf)]
                 + [_const_spec(p.shape) for p in params],
        out_specs=[row(inner), row(cf), per_b(inner, _SSD_STATE),
                   per_b(_SSD_CONV - 1, conv_dim), per_b(_CF_KERNEL - 1, cf)],
        out_shape=[jax.ShapeDtypeStruct((t, inner), _BF), jax.ShapeDtypeStruct((t, cf), _BF),
                   jax.ShapeDtypeStruct((bsz, inner, _SSD_STATE), _F32),
                   jax.ShapeDtypeStruct((bsz, _SSD_CONV - 1, conv_dim), _F32),
                   jax.ShapeDtypeStruct((bsz, _CF_KERNEL - 1, cf), _F32)],
        scratch_shapes=[pltpu.VMEM((8 + q, conv_dim), _F32), pltpu.VMEM((32 + q, cf), _F32),
                        pltpu.VMEM((inner, _SSD_STATE), _F32)],
        compiler_params=_cparams("arbitrary", "arbitrary"),
        name="even_mix_prompt",
    )(za, xbc, dt, glu, zb, *params)


def _sample_even_pre_kernel(xbc_ref, dt_ref, glu_ref, zb_ref, sctx_ref, cctx_ref,
                            cw_ref, cb_ref, dtb_ref, alog_ref, fw_ref, fb_ref, fg_ref, fbeta_ref,
                            xs_o, bm_o, cm_o, dect_o, xdtt_o, yb_o, sctx_o, cctx_o, pad):
    db = xbc_ref.shape[0]
    inner = _SSD_HEADS * _SSD_HEAD_DIM
    gn = _SSD_GROUPS * _SSD_STATE
    xbc = xbc_ref[...]
    acc = cb_ref[...] + cw_ref[_SSD_CONV - 1:_SSD_CONV, :] * xbc
    for k in range(_SSD_CONV - 1):
        acc = acc + cw_ref[k:k + 1, :] * sctx_ref[k]
    xc = _silu(acc)
    xs = xc[:, :inner]
    xs_o[...] = xs
    bm_o[...] = xc[:, inner:inner + gn]
    cm_o[...] = xc[:, inner + gn:inner + 2 * gn]
    for k in range(_SSD_CONV - 2):
        sctx_o[k] = sctx_ref[k + 1]
    sctx_o[_SSD_CONV - 2] = xbc

    lane = lax.broadcasted_iota(jnp.int32, (1, _LANES), 1)
    hmask = lane < _SSD_HEADS
    dt = jnp.where(hmask, _softplus(dt_ref[...] + dtb_ref[...]), 0.0)
    a = jnp.where(hmask, -jnp.exp(alog_ref[...]), 0.0)
    dec = jnp.where(hmask, jnp.exp(dt * a), 0.0)
    er = lax.broadcasted_iota(jnp.int32, (_LANES, inner), 0)
    ec = lax.broadcasted_iota(jnp.int32, (_LANES, inner), 1)
    expand = jnp.where((ec // _SSD_HEAD_DIM) == er, 1.0, 0.0)
    dec_x = _dot_hi(dec, expand)
    xdt = xs * _dot_hi(dt, expand)
    for src, dst in ((dec_x, dect_o), (xdt, xdtt_o)):
        pad[...] = jnp.zeros(pad.shape, _F32)
        pad[0:db, :] = src
        for j in range(inner // _LANES):
            dst[j * _LANES:(j + 1) * _LANES, :] = pad[:, j * _LANES:(j + 1) * _LANES].T

    cf = zb_ref.shape[1]
    u = glu_ref[:, :cf] * _sigmoid(glu_ref[:, cf:])
    acc = fb_ref[...] + fw_ref[_CF_KERNEL - 1:_CF_KERNEL, :] * u
    for k in range(_CF_KERNEL - 1):
        acc = acc + fw_ref[k:k + 1, :] * cctx_ref[k]
    v = _silu(_layer_norm(acc, fg_ref[...], fbeta_ref[...]))
    yb_o[...] = (v * _silu(zb_ref[...])).astype(_BF)
    for k in range(_CF_KERNEL - 2):
        cctx_o[k] = cctx_ref[k + 1]
    cctx_o[_CF_KERNEL - 2] = u


def _sample_even_pre(xbc, dt, glu, zb, sctx_t, cctx_t, params):
    db = xbc.shape[0]
    inner = _SSD_HEADS * _SSD_HEAD_DIM
    gn = _SSD_GROUPS * _SSD_STATE
    cf = zb.shape[1]
    ins = [xbc, dt, glu, zb, sctx_t, cctx_t, *params]
    full = lambda s: pl.BlockSpec(s, lambda i: (0,) * len(s))
    out_shapes = [(db, inner), (db, gn), (db, gn), (inner, _LANES), (inner, _LANES), (db, cf),
                  sctx_t.shape, cctx_t.shape]
    out_dtypes = [_F32, _F32, _F32, _F32, _F32, _BF, _F32, _F32]
    return pl.pallas_call(
        _sample_even_pre_kernel,
        grid=(1,),
        in_specs=[full(a.shape) for a in ins],
        out_specs=[full(s) for s in out_shapes],
        out_shape=[jax.ShapeDtypeStruct(s, d) for s, d in zip(out_shapes, out_dtypes)],
        scratch_shapes=[pltpu.VMEM((_LANES, inner), _F32)],
        compiler_params=_cparams("arbitrary"),
        name="sample_even_pre",
    )(*ins)


def _sample_even_rec_kernel(st_ref, dect_ref, xdtt_ref, bm_ref, cm_ref, xs_ref, za_ref, d_ref, ng_ref,
                            st_o, ya_o):
    b = pl.program_id(0)
    inner = st_ref.shape[0]
    half = inner // _SSD_GROUPS
    sel = jnp.where(lax.broadcasted_iota(jnp.int32, (_LANES, _LANES), 0) == b, 1.0, 0.0)
    dec = _dot_hi(dect_ref[...], sel)
    xd = _dot_hi(xdtt_ref[...], sel)
    grp0 = lax.broadcasted_iota(jnp.int32, (inner, 1), 0) < half
    brow = jnp.where(grp0, bm_ref[:, :_SSD_STATE], bm_ref[:, _SSD_STATE:])
    s_new = st_ref[...] * dec + xd * brow
    st_o[...] = s_new
    r8 = lax.broadcasted_iota(jnp.int32, (8, 1), 0)
    c8 = jnp.where(r8 == 0, cm_ref[:, :_SSD_STATE], jnp.where(r8 == 1, cm_ref[:, _SSD_STATE:], 0.0))
    yr = _dot_nt_hi(c8, s_new)
    lane = lax.broadcasted_iota(jnp.int32, (1, inner), 1)
    xs = xs_ref[...]
    y = jnp.where(lane < half, yr[0:1, :], yr[1:2, :]) + d_ref[...] * xs
    ya_o[...] = _group_rmsnorm_gate(y, za_ref[...], ng_ref[...]).astype(_BF)


def _sample_even_rec(state, dect, xdtt, bm, cm, xs, za, d_x, ng):
    db, inner, n = state.shape
    per_b = lambda s: pl.BlockSpec((None,) + s, lambda b: (b, 0, 0))
    gn = bm.shape[1]
    r3 = lambda a: a.reshape(db, 1, a.shape[1])
    return pl.pallas_call(
        _sample_even_rec_kernel,
        grid=(db,),
        in_specs=[per_b((inner, n)), _const_spec(dect.shape), _const_spec(xdtt.shape),
                  per_b((1, gn)), per_b((1, gn)), per_b((1, inner)), per_b((1, inner)),
                  _const_spec(d_x.shape), _const_spec(ng.shape)],
        out_specs=[per_b((inner, n)), per_b((1, inner))],
        out_shape=[jax.ShapeDtypeStruct((db, inner, n), _F32), jax.ShapeDtypeStruct((db, 1, inner), _BF)],
        compiler_params=_cparams("arbitrary"),
        name="sample_even_rec",
    )(state, dect, xdtt, r3(bm), r3(cm), r3(xs), r3(za), d_x, ng)


def _bisect_threshold(count_ge, rmin, rmax, n_adm, c_ge0, c_gt0, k):
    all_sel = n_adm <= k
    pos = c_gt0 >= k
    zero = jnp.logical_and(jnp.logical_not(pos), c_ge0 >= k)
    lo0 = jnp.where(all_sel, rmin, jnp.where(jnp.logical_or(pos, zero), 0.0, rmin))
    c0 = jnp.where(all_sel, n_adm, jnp.where(jnp.logical_or(pos, zero), c_ge0, n_adm))
    neg = jnp.logical_and(jnp.logical_not(all_sel), c_ge0 < k)
    hi0 = jnp.where(neg, 0.0, rmax + (rmax - lo0))
    done0 = jnp.logical_or(all_sel, zero)

    def cond(st):
        it, _, _, _, done = st
        return jnp.logical_and(it < _MAX_BISECT, jnp.min(done) < 0.5)

    def body(st):
        it, lo, hi, c_lo, done = st
        mid = lo + (hi - lo) * 0.5
        stuck = jnp.logical_or(mid <= lo, mid >= hi)
        c = count_ge(mid)
        ge = c >= k
        live = done < 0.5
        upd_lo = jnp.logical_and(live, ge)
        upd_hi = jnp.logical_and(live, jnp.logical_not(ge))
        lo_n = jnp.where(upd_lo, mid, lo)
        c_n = jnp.where(upd_lo, c, c_lo)
        hi_n = jnp.where(upd_hi, mid, hi)
        fin = jnp.logical_or(stuck, jnp.logical_and(ge, c <= k))
        done_n = jnp.where(fin, 1.0, done)
        return it + 1, lo_n, hi_n, c_n, done_n

    st = (jnp.int32(0), lo0, hi0, c0, jnp.where(done0, 1.0, 0.0))
    _, lo, _, c_lo, _ = lax.while_loop(cond, body, st)
    return lo, c_lo


def _attn_prompt_kernel(qi_ref, wi_ref, q_ref, z_ref, kit_ref, kt_ref, vg_ref, o_ref,
                        sc_ref, qh_ref, acc_ref, m_ref, l_ref, al_ref, s_ref, p_ref, ob_ref,
                        *, tq, tkb, topk):
    i = pl.program_id(1)
    nkb = (i * tq + tq + tkb - 1) // tkb
    row = i * tq + lax.broadcasted_iota(jnp.int32, (tq, 1), 0)
    col0 = lax.broadcasted_iota(jnp.int32, (1, tkb), 1)
    kf = jnp.float32(topk)
    inf = jnp.float32(jnp.inf)

    qi = qi_ref[...]
    w = wi_ref[...] * (_IDX_HEADS ** -0.5 * _IDX_DIM ** -0.5)
    qis = [qi[:, h * _IDX_DIM:(h + 1) * _IDX_DIM] for h in range(_IDX_HEADS)]
    wcs = [w[:, _IDX_DIM + h:_IDX_DIM + h + 1] for h in range(_IDX_HEADS)]

    sw = 2 * _LANES
    nlb = tkb // _LANES

    def score_body(kb, carry):
        rmin, rmax, cge0, cgt0 = carry
        for jb in range(tkb // sw):
            kit = kit_ref[kb, :, jb * sw:(jb + 1) * sw]
            acc = jnp.zeros((tq, sw), _F32)
            for h in range(_IDX_HEADS):
                acc = acc + jnp.maximum(_dot(qis[h], kit), 0.0) * wcs[h]
            adm = (kb * tkb + jb * sw + col0[:, :sw]) <= row
            sc = jnp.where(adm, acc, -inf)
            sc_ref[kb, :, jb * sw:(jb + 1) * sw] = sc
            for lb in range(sw // _LANES):
                a = sc[:, lb * _LANES:(lb + 1) * _LANES]
                rmax = jnp.maximum(rmax, a)
                rmin = jnp.minimum(rmin, jnp.where(adm[:, lb * _LANES:(lb + 1) * _LANES], a, inf))
                cge0 = cge0 + jnp.where(a >= 0.0, 1.0, 0.0)
                cgt0 = cgt0 + jnp.where(a > 0.0, 1.0, 0.0)
        return rmin, rmax, cge0, cgt0

    zeros_l = jnp.zeros((tq, _LANES), _F32)
    rmin, rmax, cge0, cgt0 = lax.fori_loop(
        0, nkb, score_body,
        (jnp.full((tq, _LANES), inf, _F32), jnp.full((tq, _LANES), -inf, _F32), zeros_l, zeros_l))
    rmin = jnp.min(rmin, axis=1, keepdims=True)
    rmax = jnp.max(rmax, axis=1, keepdims=True)
    cge0 = jnp.sum(cge0, axis=1, keepdims=True)
    cgt0 = jnp.sum(cgt0, axis=1, keepdims=True)

    def count_ge(x):
        xb = jnp.broadcast_to(x, (tq, _LANES))

        def cbody(kb, cnt):
            for lb in range(nlb):
                cnt = cnt + jnp.where(sc_ref[kb, :, lb * _LANES:(lb + 1) * _LANES] >= xb, 1.0, 0.0)
            return cnt
        cnt = lax.fori_loop(0, nkb, cbody, jnp.zeros((tq, _LANES), _F32))
        return jnp.sum(cnt, axis=1, keepdims=True)

    n_adm = (row + 1).astype(_F32)
    lo, c_lo = _bisect_threshold(count_ge, rmin, rmax, n_adm, cge0, cgt0, kf)

    @pl.when(jnp.max(c_lo) > kf)
    def _():
        def gbody(kb, cnt):
            return cnt + jnp.sum(jnp.where(sc_ref[kb] > lo, 1.0, 0.0), axis=1, keepdims=True)
        need = kf - lax.fori_loop(0, nkb, gbody, jnp.zeros((tq, 1), _F32))
        tri = jnp.where(lax.broadcasted_iota(jnp.int32, (_LANES, _LANES), 0)
                        <= lax.broadcasted_iota(jnp.int32, (_LANES, _LANES), 1), 1.0, 0.0).astype(_BF)

        def tbody(kb, carry):
            for jb in range(tkb // _LANES):
                s = sc_ref[kb, :, jb * _LANES:(jb + 1) * _LANES]
                eq = s == lo
                pre = _dot(jnp.where(eq, 1.0, 0.0).astype(_BF), tri)
                drop = jnp.logical_and(eq, (carry + pre) > need)
                sc_ref[kb, :, jb * _LANES:(jb + 1) * _LANES] = jnp.where(drop, -inf, s)
                carry = carry + pre[:, _LANES - 1:_LANES]
            return carry
        lax.fori_loop(0, nkb, tbody, jnp.zeros((tq, 1), _F32))

    q = q_ref[...]
    for h in range(_ATTN_HEADS):
        qh_ref[h] = q[:, h * _ATTN_HEAD_DIM:(h + 1) * _ATTN_HEAD_DIM]
    m_ref[...] = jnp.full(m_ref.shape, _NEG, _F32)
    l_ref[...] = jnp.zeros(l_ref.shape, _F32)
    acc_ref[...] = jnp.zeros(acc_ref.shape, _F32)
    rep = _ATTN_HEADS // _ATTN_KV_HEADS

    def bias_body(kb, carry):
        sc_ref[kb] = jnp.where(sc_ref[kb] >= lo, 0.0, _NEG)
        return carry
    lax.fori_loop(0, nkb, bias_body, 0)

    def attn_body(kb, carry):
        start = pl.multiple_of(kb * tkb, tkb)
        for g in range(_ATTN_KV_HEADS):
            kg = kt_ref[kb, g * _ATTN_HEAD_DIM:(g + 1) * _ATTN_HEAD_DIM, :]
            for r in range(rep):
                h = g * rep + r
                s = _dot(qh_ref[h], kg) + sc_ref[kb]
                s_ref[h] = s
                m_old = m_ref[h]
                m_new = jnp.maximum(m_old, jnp.max(s, axis=1, keepdims=True))
                al_ref[h] = jnp.exp(m_old - m_new)
                m_ref[h] = m_new
        for h in range(_ATTN_HEADS):
            m_new = m_ref[h]
            psum = jnp.zeros((tq, _LANES), _F32)
            for lb in range(nlb):
                p = jnp.exp(s_ref[h, :, lb * _LANES:(lb + 1) * _LANES] - m_new)
                psum = psum + p
                p_ref[h, :, lb * _LANES:(lb + 1) * _LANES] = p.astype(_BF)
            l_ref[h] = l_ref[h] * al_ref[h] + jnp.sum(psum, axis=1, keepdims=True)
        for g in range(_ATTN_KV_HEADS):
            vg = vg_ref[g, pl.ds(start, tkb), :]
            for r in range(rep):
                h = g * rep + r
                acc_ref[h] = acc_ref[h] * al_ref[h, :, :_ATTN_HEAD_DIM] + _dot(p_ref[h], vg)
        return carry

    lax.fori_loop(0, nkb, attn_body, 0)
    for h in range(_ATTN_HEADS):
        ob_ref[:, h * _ATTN_HEAD_DIM:(h + 1) * _ATTN_HEAD_DIM] = acc_ref[h] / l_ref[h, :, :_ATTN_HEAD_DIM]
    o_ref[...] = (ob_ref[...] * _silu(z_ref[...])).astype(_BF)


def _attn_prompt(qi, wi, q, z, kit, kt, vg, bsz, seq, tkb):
    tq = 128
    nq = seq // tq
    nkbt = seq // tkb
    t = bsz * seq
    topk = min(_TOPK_MAX, seq // 4)
    width = q.shape[1]
    row = lambda n: pl.BlockSpec((tq, n), lambda b, i: (b * nq + i, 0))
    per_b = lambda s: pl.BlockSpec((None,) + s, lambda b, i: (b,) + (0,) * len(s),
                                   pipeline_mode=pl.Buffered(1))
    return pl.pallas_call(
        functools.partial(_attn_prompt_kernel, tq=tq, tkb=tkb, topk=topk),
        grid=(bsz, nq),
        in_specs=[row(qi.shape[1]), row(_LANES), row(width), row(width),
                  per_b(kit.shape[1:]), per_b(kt.shape[1:]), per_b(vg.shape[1:])],
        out_specs=row(width),
        out_shape=jax.ShapeDtypeStruct((t, width), _BF),
        scratch_shapes=[pltpu.VMEM((nkbt, tq, tkb), _F32),
                        pltpu.VMEM((_ATTN_HEADS, tq, _ATTN_HEAD_DIM), _BF),
                        pltpu.VMEM((_ATTN_HEADS, tq, _ATTN_HEAD_DIM), _F32),
                        pltpu.VMEM((_ATTN_HEADS, tq, _LANES), _F32),
                        pltpu.VMEM((_ATTN_HEADS, tq, _LANES), _F32),
                        pltpu.VMEM((_ATTN_HEADS, tq, _LANES), _F32),
                        pltpu.VMEM((_ATTN_HEADS, tq, tkb), _F32),
                        pltpu.VMEM((_ATTN_HEADS, tq, tkb), _BF),
                        pltpu.VMEM((tq, width), _F32)],
        compiler_params=_cparams("arbitrary", "arbitrary"),
        name="attn_prompt",
    )(qi, wi, q, z, kit, kt, vg)


def _attn_sample_kernel(pt_ref, qi_ref, w_ref, kin_ref, q_ref, kn_ref, vn_ref,
                        ckit_ref, ckt_ref, cvt_ref, o_ref,
                        kibuf, kbuf, vbuf, sc_ref, sn_ref, lg_ref, sem, *, n_pages, topk):
    b = pl.program_id(0)
    past = n_pages * _PAGE
    kf = jnp.float32(topk)
    inf = jnp.float32(jnp.inf)
    caches = ((ckit_ref, kibuf), (ckt_ref, kbuf), (cvt_ref, vbuf))

    def page_copy(which, page):
        cache_ref, buf = caches[which]
        return pltpu.make_async_copy(cache_ref.at[pt_ref[b, page]], buf.at[page], sem.at[which])

    def start_all(which):
        lax.fori_loop(0, n_pages, lambda p, c: (page_copy(which, p).start(), c)[1], 0)

    def wait_all(which):
        lax.fori_loop(0, n_pages, lambda p, c: (page_copy(which, p).wait(), c)[1], 0)

    for which in range(3):
        start_all(which)

    def total(x):
        return jnp.sum(jnp.sum(x, axis=1, keepdims=True), axis=0, keepdims=True)

    qi8 = qi_ref[...].astype(_BF)
    w8 = w_ref[...] * (_IDX_HEADS ** -0.5 * _IDX_DIM ** -0.5)
    wait_all(0)

    def score_body(p, c):
        s8 = _dot(qi8, kibuf[p].astype(_BF))
        sc_ref[pl.ds(p, 1), :] = jnp.sum(jnp.maximum(s8, 0.0) * w8, axis=0, keepdims=True)
        return c
    lax.fori_loop(0, n_pages, score_body, 0)
    kin = kin_ref[...].astype(_BF).astype(_F32)
    sn8 = jnp.sum(qi8.astype(_F32) * kin, axis=1, keepdims=True)
    sn_ref[...] = jnp.sum(jnp.maximum(sn8, 0.0) * w8, axis=0, keepdims=True)
    snew = sn_ref[...]

    if past + 1 > topk:
        sc = sc_ref[...]
        rmax = jnp.maximum(jnp.max(jnp.max(sc, axis=1, keepdims=True), axis=0, keepdims=True), snew)
        rmin = jnp.minimum(jnp.min(jnp.min(sc, axis=1, keepdims=True), axis=0, keepdims=True), snew)
        one = lambda cnd: jnp.where(cnd, 1.0, 0.0)

        def count_ge(x):
            return total(one(sc_ref[...] >= x)) + one(snew >= x)

        lo, c_lo = _bisect_threshold(count_ge, rmin, rmax, jnp.full((1, 1), past + 1.0, _F32),
                                     count_ge(jnp.zeros((1, 1), _F32)),
                                     total(one(sc > 0.0)) + one(snew > 0.0), kf)

        @pl.when(jnp.max(c_lo) > kf)
        def _():
            need = kf - (total(one(sc_ref[...] > lo)) + one(snew > lo))
            tri = jnp.where(lax.broadcasted_iota(jnp.int32, (_LANES, _LANES), 0)
                            <= lax.broadcasted_iota(jnp.int32, (_LANES, _LANES), 1), 1.0, 0.0).astype(_BF)

            def tbody(p, carry):
                s = sc_ref[pl.ds(p, 1), :]
                eq = s == lo
                eq8 = jnp.broadcast_to(one(eq), (8, _LANES)).astype(_BF)
                pre = _dot(eq8, tri)[0:1, :]
                drop = jnp.logical_and(eq, (carry + pre) > need)
                sc_ref[pl.ds(p, 1), :] = jnp.where(drop, -inf, s)
                return carry + pre[:, _LANES - 1:_LANES]
            carry = lax.fori_loop(0, n_pages, tbody, jnp.zeros((1, 1), _F32))
            drop_new = jnp.logical_and(snew == lo, (carry + 1.0) > need)
            sn_ref[...] = jnp.where(drop_new, -inf, snew)
    else:
        lo = jnp.full((1, 1), -inf, _F32)

    kvw = _ATTN_KV_HEADS * _ATTN_HEAD_DIM
    rep = _ATTN_HEADS // _ATTN_KV_HEADS
    rr = lax.broadcasted_iota(jnp.int32, (_ATTN_HEAD_DIM, kvw), 0)
    rc = lax.broadcasted_iota(jnp.int32, (_ATTN_HEAD_DIM, kvw), 1)
    spread = jnp.where((rc % _ATTN_HEAD_DIM) == rr, 1.0, 0.0)
    hr = lax.broadcasted_iota(jnp.int32, (_ATTN_HEADS, kvw), 0)
    hc = lax.broadcasted_iota(jnp.int32, (_ATTN_HEADS, kvw), 1)
    own = (hc // _ATTN_HEAD_DIM) == (hr // rep)
    qbd = jnp.where(own, _dot_hi(q_ref[...], spread), 0.0).astype(_BF)
    kn = kn_ref[...].astype(_BF).astype(_F32)
    lnew = (jnp.sum(qbd.astype(_F32) * kn, axis=1, keepdims=True)
            + jnp.where(sn_ref[...] >= lo, 0.0, _NEG))
    wait_all(1)

    def logit_body(p, mx):
        lg = _dot(qbd, kbuf[p].astype(_BF)) + jnp.where(sc_ref[pl.ds(p, 1), :] >= lo, 0.0, _NEG)
        lg_ref[p] = lg
        return jnp.maximum(mx, lg)
    mx = lax.fori_loop(0, n_pages, logit_body, jnp.full((_ATTN_HEADS, _LANES), _NEG, _F32))
    m = jnp.maximum(jnp.max(mx, axis=1, keepdims=True), lnew)
    pn = jnp.exp(lnew - m)
    vn = vn_ref[...].astype(_BF).astype(_F32)
    wait_all(2)

    def pv_body(p, carry):
        psum, out = carry
        pr = jnp.exp(lg_ref[p] - m)
        return psum + pr, out + _dot_nt(pr.astype(_BF), vbuf[p].astype(_BF))
    psum, out = lax.fori_loop(0, n_pages, pv_body,
                              (jnp.zeros((_ATTN_HEADS, _LANES), _F32), pn.astype(_BF).astype(_F32) * vn))
    den = jnp.sum(psum, axis=1, keepdims=True) + pn
    out = jnp.where(own, out / den, 0.0)
    gather = jnp.where((lax.broadcasted_iota(jnp.int32, (kvw, _ATTN_HEAD_DIM), 0) % _ATTN_HEAD_DIM)
                       == lax.broadcasted_iota(jnp.int32, (kvw, _ATTN_HEAD_DIM), 1), 1.0, 0.0)
    o_ref[...] = _dot_hi(out, gather)


def _attn_sample(page_table, qi, wi, ki_new, q, k_new, v_new, cache_ki, cache_k, cache_v):
    db, n_pages = page_table.shape
    past = n_pages * _PAGE
    topk = min(_TOPK_MAX, (past + 1) // 4)
    kvw = _ATTN_KV_HEADS * _ATTN_HEAD_DIM
    n_pool = cache_k.shape[0]
    per_b = lambda s: pl.BlockSpec((None,) + s, lambda b, pt: (b,) + (0,) * len(s))
    anyspec = pl.BlockSpec(memory_space=pl.ANY)
    kv_t = lambda c: jnp.transpose(c, (0, 2, 3, 1)).reshape(n_pool, kvw, _PAGE)
    ins = [qi.reshape(db, _IDX_HEADS, _IDX_DIM),
           wi[:, _IDX_DIM:_IDX_DIM + _IDX_HEADS].reshape(db, _IDX_HEADS, 1),
           ki_new.reshape(db, 1, _IDX_DIM),
           q.reshape(db, _ATTN_HEADS, _ATTN_HEAD_DIM),
           k_new.reshape(db, 1, kvw), v_new.reshape(db, 1, kvw),
           jnp.transpose(cache_ki, (0, 2, 1)), kv_t(cache_k), kv_t(cache_v)]
    grid_spec = pltpu.PrefetchScalarGridSpec(
        num_scalar_prefetch=1,
        grid=(db,),
        in_specs=[per_b((_IDX_HEADS, _IDX_DIM)), per_b((_IDX_HEADS, 1)), per_b((1, _IDX_DIM)),
                  per_b((_ATTN_HEADS, _ATTN_HEAD_DIM)), per_b((1, kvw)), per_b((1, kvw)),
                  anyspec, anyspec, anyspec],
        out_specs=per_b((_ATTN_HEADS, _ATTN_HEAD_DIM)),
        scratch_shapes=[pltpu.VMEM((n_pages, _IDX_DIM, _PAGE), _F32),
                        pltpu.VMEM((n_pages, kvw, _PAGE), _F32),
                        pltpu.VMEM((n_pages, kvw, _PAGE), _F32),
                        pltpu.VMEM((n_pages, _PAGE), _F32), pltpu.VMEM((1, 1), _F32),
                        pltpu.VMEM((n_pages, _ATTN_HEADS, _PAGE), _F32),
                        pltpu.SemaphoreType.DMA((3,))],
    )
    o = pl.pallas_call(
        functools.partial(_attn_sample_kernel, n_pages=n_pages, topk=topk),
        grid_spec=grid_spec,
        out_shape=jax.ShapeDtypeStruct((db, _ATTN_HEADS, _ATTN_HEAD_DIM), _F32),
        compiler_params=_cparams("arbitrary"),
        name="attn_sample",
    )(page_table, *ins)
    return o.reshape(db, _ATTN_HEADS * _ATTN_HEAD_DIM)


def _pad_lanes(a, n=_LANES):
    return jnp.pad(a, [(0, 0)] * (a.ndim - 1) + [(0, n - a.shape[-1])])


def _even_weights(w_in, conv_w, conv_b, dt_bias, a_log, d_skip, norm_g, cf_w, cf_b, cf_g, cf_beta, w_out):
    inner = _SSD_HEADS * _SSD_HEAD_DIM
    conv_dim = conv_w.shape[1]
    cf = cf_w.shape[1]
    o1, o2, o3, o4 = inner, inner + conv_dim, inner + conv_dim + _SSD_HEADS, inner + conv_dim + _SSD_HEADS + 2 * cf
    wb = w_in.astype(_BF)
    proj = (wb[:, :o1], wb[:, o1:o2], _pad_lanes(wb[:, o2:o3]), wb[:, o3:o4], wb[:, o4:])
    row = lambda v: v.reshape(1, -1)
    ssd = (conv_w, row(conv_b), _pad_lanes(row(dt_bias)), _pad_lanes(row(a_log)))
    d_x = row(jnp.repeat(d_skip, _SSD_HEAD_DIM))
    cfp = (cf_w, row(cf_b), row(cf_g), row(cf_beta))
    wo = w_out.astype(_BF)
    return proj, ssd, d_x, row(norm_g), cfp, (wo[:inner], wo[inner:])


def _even_layer_prompt(x, wts, ln_g, ln_b, bsz, seq):
    proj, ssd, d_x, ng, cfp, wo = wts
    za, xbc, dt, glu, zb = _even_proj(x, proj)
    ya, yb, ssm, sc, cc = _even_mix_prompt(za, xbc, dt, glu, zb, (*ssd, d_x, ng, *cfp), bsz, seq)
    x_new = _outproj_ln(x, [ya, yb], wo, ln_g, ln_b)
    return x_new, ssm.reshape(bsz, _SSD_HEADS, _SSD_HEAD_DIM, _SSD_STATE), sc, cc


def _even_layer_sample(x, st_ssm, st_sconv, st_cconv, wts, ln_g, ln_b):
    proj, ssd, d_x, ng, cfp, wo = wts
    db = x.shape[0]
    inner = _SSD_HEADS * _SSD_HEAD_DIM
    za, xbc, dt, glu, zb = _even_proj(x, proj)
    xs, bm, cm, dect, xdtt, yb, sctx_n, cctx_n = _sample_even_pre(
        xbc, dt, glu, zb, jnp.swapaxes(st_sconv, 0, 1), jnp.swapaxes(st_cconv, 0, 1), (*ssd, *cfp))
    st_new, ya = _sample_even_rec(st_ssm.reshape(db, inner, _SSD_STATE), dect, xdtt, bm, cm, xs, za, d_x, ng)
    x_new = _outproj_ln(x, [ya.reshape(db, inner), yb], wo, ln_g, ln_b)
    return (x_new, st_new.reshape(st_ssm.shape), jnp.swapaxes(sctx_n, 0, 1), jnp.swapaxes(cctx_n, 0, 1))


def _odd_weights(w_in, w_out):
    aw = _ATTN_HEADS * _ATTN_HEAD_DIM
    kvw = _ATTN_KV_HEADS * _ATTN_HEAD_DIM
    iw = _IDX_HEADS * _IDX_DIM
    o1, o2, o3, o4 = aw, aw + kvw, aw + 2 * kvw, aw + 2 * kvw + iw
    o5 = o4 + _IDX_DIM + _IDX_HEADS
    wb = w_in.astype(_BF)
    ws = (wb[:, :o1], wb[:, o1:o2], wb[:, o2:o3], wb[:, o3:o4], _pad_lanes(wb[:, o4:o5]), wb[:, o5:])
    wts = (wb[:, o1:o2].T, wb[:, o4:o4 + _IDX_DIM].T)
    return ws, wts, w_out.astype(_BF)


def _odd_layer_prompt(x, wts, ln_g, ln_b, bsz, seq):
    ws, wtr, wo = wts
    tkb = min(512, seq)
    q, k, v, qi, ki, wi, z, kt, kit, vg = _odd_proj_prompt(x, ws, wtr, bsz, seq, tkb)
    o = _attn_prompt(qi, wi, q, z, kit, kt, vg, bsz, seq, tkb)
    x_new = _outproj_ln(x, [o], [wo], ln_g, ln_b)
    return (x_new, k.reshape(bsz, seq, _ATTN_KV_HEADS, _ATTN_HEAD_DIM),
            v.reshape(bsz, seq, _ATTN_KV_HEADS, _ATTN_HEAD_DIM), ki.reshape(bsz, seq, _IDX_DIM))


def _odd_layer_sample(x, cache_k, cache_v, cache_ki, page_table, wts, ln_g, ln_b):
    ws, _, wo = wts
    db = x.shape[0]
    q, k, v, qi, ki, wi, z = _odd_proj_sample(x, ws)
    o = _attn_sample(page_table, qi, wi, ki, q, k, v, cache_ki, cache_k, cache_v)
    x_new = _outproj_ln(x, [o], [wo], ln_g, ln_b, z=z)
    return (x_new, k.reshape(db, 1, _ATTN_KV_HEADS, _ATTN_HEAD_DIM),
            v.reshape(db, 1, _ATTN_KV_HEADS, _ATTN_HEAD_DIM), ki.reshape(db, 1, _IDX_DIM))


def kernel(x_prompt, x_sample, state_ssm_l0, state_ssdconv_l0, state_cfconv_l0, cache_k_l1, cache_v_l1, cache_kidx_l1, state_ssm_l2, state_ssdconv_l2, state_cfconv_l2, cache_k_l3, cache_v_l3, cache_kidx_l3, page_table, w_in_even, ssd_conv_w, ssd_conv_b, ssd_dt_bias, ssd_a_log, ssd_d, ssd_norm_g, cf_dw_w, cf_dw_b, cf_ln_g, cf_ln_b, w_out_even, w_in_odd, w_out_odd, ln_g, ln_b):
    bsz, seq, d = x_prompt.shape
    db = x_sample.shape[0]
    ssm_states = (state_ssm_l0, state_ssm_l2)
    sconv_states = (state_ssdconv_l0, state_ssdconv_l2)
    cconv_states = (state_cfconv_l0, state_cfconv_l2)
    k_caches = (cache_k_l1, cache_k_l3)
    v_caches = (cache_v_l1, cache_v_l3)
    ki_caches = (cache_kidx_l1, cache_kidx_l3)
    yp = x_prompt.reshape(bsz * seq, d)
    ys = x_sample.reshape(db, d)
    new_state = []
    for layer in range(_DEPTH):
        j = layer // 2
        g, b = ln_g[layer].reshape(1, d), ln_b[layer].reshape(1, d)
        if layer % 2 == 0:
            wts = _even_weights(w_in_even[j], ssd_conv_w[j], ssd_conv_b[j], ssd_dt_bias[j], ssd_a_log[j],
                                ssd_d[j], ssd_norm_g[j], cf_dw_w[j], cf_dw_b[j], cf_ln_g[j], cf_ln_b[j],
                                w_out_even[j])
            yp, ssm_p, sc_p, cc_p = _even_layer_prompt(yp, wts, g, b, bsz, seq)
            ys, ssm_s, sc_s, cc_s = _even_layer_sample(ys, ssm_states[j], sconv_states[j], cconv_states[j],
                                                       wts, g, b)
            new_state += [ssm_p, ssm_s, sc_p, sc_s, cc_p, cc_s]
        else:
            wts = _odd_weights(w_in_odd[j], w_out_odd[j])
            yp, k_p, v_p, ki_p = _odd_layer_prompt(yp, wts, g, b, bsz, seq)
            ys, k_s, v_s, ki_s = _odd_layer_sample(ys, k_caches[j], v_caches[j], ki_caches[j], page_table,
                                                   wts, g, b)
            new_state += [k_p, k_s, v_p, v_s, ki_p, ki_s]
    return (yp.reshape(bsz, seq, d), ys.reshape(db, 1, d), *new_state)
```

```python
import functools

import jax
import jax.numpy as jnp
from jax import lax
from jax.experimental import pallas as pl
from jax.experimental.pallas import tpu as pltpu

_BF = jnp.bfloat16
_F32 = jnp.float32
_HI = lax.Precision.HIGHEST

_SSD_HEADS = 16
_SSD_HEAD_DIM = 64
_SSD_GROUPS = 2
_SSD_STATE = 128
_SSD_CONV = 4
_CF_KERNEL = 31
_ATTN_HEADS = 16
_ATTN_KV_HEADS = 4
_ATTN_HEAD_DIM = 64
_IDX_HEADS = 8
_IDX_DIM = 64
_TOPK_MAX = 256
_PAGE = 128
_DEPTH = 4
_ALPHA = (2 * _DEPTH) ** 0.25
_EPS = 1e-5

_LANES = 128
_VMEM_LIMIT = 56 * 1024 * 1024
_NEG = -1e30
_MAX_BISECT = 200


def _dot(a, b):
    return jnp.dot(a, b, preferred_element_type=_F32)


def _dot_nt(a, b):
    return lax.dot_general(a, b, (((1,), (1,)), ((), ())), preferred_element_type=_F32)


def _dot_hi(a, b):
    return jnp.dot(a, b, precision=_HI, preferred_element_type=_F32)


def _dot_nt_hi(a, b):
    return lax.dot_general(a, b, (((1,), (1,)), ((), ())), precision=_HI,
                           preferred_element_type=_F32)


def _sigmoid(x):
    return 1.0 / (1.0 + jnp.exp(-x))


def _silu(x):
    return x * _sigmoid(x)


def _softplus(x):
    return jnp.maximum(x, 0.0) + jnp.log1p(jnp.exp(-jnp.abs(x)))


def _layer_norm(x, g, b):
    mu = jnp.mean(x, axis=-1, keepdims=True)
    xc = x - mu
    var = jnp.mean(xc * xc, axis=-1, keepdims=True)
    return xc * lax.rsqrt(var + _EPS) * g + b


def _cparams(*sem):
    return pltpu.CompilerParams(dimension_semantics=sem, vmem_limit_bytes=_VMEM_LIMIT)


def _const_spec(shape):
    return pl.BlockSpec(shape, lambda *_: (0,) * len(shape), pipeline_mode=pl.Buffered(1))


def _even_proj_kernel(x_ref, wza, wxbc, wdt, wglu, wzb, za_o, xbc_o, dt_o, glu_o, zb_o):
    xb = x_ref[...].astype(_BF)
    za_o[...] = _dot(xb, wza[...])
    xbc_o[...] = _dot(xb, wxbc[...])
    dt_o[...] = _dot(xb, wdt[...])
    glu_o[...] = _dot(xb, wglu[...])
    zb_o[...] = _dot(xb, wzb[...])


def _even_proj(x, ws):
    t, d = x.shape
    tm = min(256, t)
    widths = [w.shape[1] for w in ws]
    row = lambda n: pl.BlockSpec((tm, n), lambda i: (i, 0))
    return pl.pallas_call(
        _even_proj_kernel,
        grid=(t // tm,),
        in_specs=[row(d)] + [_const_spec(w.shape) for w in ws],
        out_specs=[row(n) for n in widths],
        out_shape=[jax.ShapeDtypeStruct((t, n), _F32) for n in widths],
        compiler_params=_cparams("arbitrary"),
        name="even_proj",
    )(x, *ws)


def _odd_proj_prompt_kernel(x_ref, wq, wk, wv, wqi, wkw, wz, wkT, wkiT,
                            q_o, k_o, v_o, qi_o, ki_o, wi_o, z_o, kT_o, kiT_o, vg_o):
    xb = x_ref[...].astype(_BF)
    q_o[...] = (_dot(xb, wq[...]) * (_ATTN_HEAD_DIM ** -0.5)).astype(_BF)
    k_o[...] = _dot(xb, wk[...])
    v = _dot(xb, wv[...])
    v_o[...] = v
    first_half = lax.broadcasted_iota(jnp.int32, (1, _LANES), 1) < _ATTN_HEAD_DIM
    for g in range(_ATTN_KV_HEADS):
        blk = v[:, (g // 2) * _LANES:(g // 2 + 1) * _LANES]
        if g % 2:
            blk = pltpu.roll(blk, _ATTN_HEAD_DIM, 1)
        vg_o[g] = jnp.where(first_half, blk, 1.0).astype(_BF)
    qi_o[...] = _dot(xb, wqi[...]).astype(_BF)
    kw = _dot(xb, wkw[...])
    ki_o[...] = kw[:, :_IDX_DIM]
    wi_o[...] = kw
    z_o[...] = _dot(xb, wz[...])
    kT_o[...] = _dot_nt(wkT[...], xb).astype(_BF)
    kiT_o[...] = _dot_nt(wkiT[...], xb).astype(_BF)


def _odd_proj_prompt(x, ws, wts, bsz, seq, tkb):
    t, d = x.shape
    tm = tkb
    nb = seq // tm
    wq, wk, wv, wqi, wkw, wz = ws
    wkT, wkiT = wts
    kvw = wk.shape[1]
    row = lambda n: pl.BlockSpec((tm, n), lambda i: (i, 0))
    outs = [
        (jax.ShapeDtypeStruct((t, wq.shape[1]), _BF), row(wq.shape[1])),
        (jax.ShapeDtypeStruct((t, kvw), _F32), row(kvw)),
        (jax.ShapeDtypeStruct((t, kvw), _F32), row(kvw)),
        (jax.ShapeDtypeStruct((t, wqi.shape[1]), _BF), row(wqi.shape[1])),
        (jax.ShapeDtypeStruct((t, _IDX_DIM), _F32), row(_IDX_DIM)),
        (jax.ShapeDtypeStruct((t, _LANES), _F32), row(_LANES)),
        (jax.ShapeDtypeStruct((t, wz.shape[1]), _F32), row(wz.shape[1])),
        (jax.ShapeDtypeStruct((bsz, nb, kvw, tm), _BF),
         pl.BlockSpec((None, None, kvw, tm), lambda i: (i // nb, i % nb, 0, 0))),
        (jax.ShapeDtypeStruct((bsz, nb, _IDX_DIM, tm), _BF),
         pl.BlockSpec((None, None, _IDX_DIM, tm), lambda i: (i // nb, i % nb, 0, 0))),
        (jax.ShapeDtypeStruct((bsz, _ATTN_KV_HEADS, seq, _LANES), _BF),
         pl.BlockSpec((None, _ATTN_KV_HEADS, tm, _LANES), lambda i: (i // nb, 0, i % nb, 0))),
    ]
    return pl.pallas_call(
        _odd_proj_prompt_kernel,
        grid=(t // tm,),
        in_specs=[row(d)] + [_const_spec(w.shape) for w in (*ws, *wts)],
        out_specs=[o[1] for o in outs],
        out_shape=[o[0] for o in outs],
        compiler_params=_cparams("arbitrary"),
        name="odd_proj_prompt",
    )(x, *ws, *wts)


def _odd_proj_sample_kernel(x_ref, wq, wk, wv, wqi, wkw, wz, q_o, k_o, v_o, qi_o, ki_o, wi_o, z_o):
    xb = x_ref[...].astype(_BF)
    q_o[...] = _dot(xb, wq[...]) * (_ATTN_HEAD_DIM ** -0.5)
    k_o[...] = _dot(xb, wk[...])
    v_o[...] = _dot(xb, wv[...])
    qi_o[...] = _dot(xb, wqi[...])
    kw = _dot(xb, wkw[...])
    ki_o[...] = kw[:, :_IDX_DIM]
    wi_o[...] = kw
    z_o[...] = _dot(xb, wz[...])


def _odd_proj_sample(x, ws):
    t, d = x.shape
    wq, wk, wv, wqi, wkw, wz = ws
    widths = [wq.shape[1], wk.shape[1], wv.shape[1], wqi.shape[1], _IDX_DIM, _LANES, wz.shape[1]]
    full = lambda n: pl.BlockSpec((t, n), lambda i: (0, 0))
    return pl.pallas_call(
        _odd_proj_sample_kernel,
        grid=(1,),
        in_specs=[full(d)] + [_const_spec(w.shape) for w in ws],
        out_specs=[full(n) for n in widths],
        out_shape=[jax.ShapeDtypeStruct((t, n), _F32) for n in widths],
        compiler_params=_cparams("arbitrary"),
        name="odd_proj_sample",
    )(x, *ws)


def _outproj_kernel(*refs, n_in, gated):
    x_ref = refs[0]
    a_refs = refs[1:1 + n_in]
    pos = 1 + n_in
    z_ref = refs[pos] if gated else None
    pos += 1 if gated else 0
    w_refs = refs[pos:pos + n_in]
    g_ref, b_ref, o_ref = refs[pos + n_in:pos + n_in + 3]
    acc = _ALPHA * x_ref[...]
    for a_ref, w_ref in zip(a_refs, w_refs):
        a = a_ref[...]
        if gated:
            a = a.astype(_F32) * _silu(z_ref[...])
        acc = acc + _dot(a.astype(_BF), w_ref[...])
    o_ref[...] = _layer_norm(acc, g_ref[...], b_ref[...])


def _outproj_ln(x, a_list, w_list, g, b, z=None):
    t, d = x.shape
    tm = min(512, t)
    row = lambda n: pl.BlockSpec((tm, n), lambda i: (i, 0))
    gated = z is not None
    ins = [x, *a_list] + ([z] if gated else []) + [*w_list, g, b]
    specs = ([row(d)] + [row(a.shape[1]) for a in a_list] + ([row(z.shape[1])] if gated else [])
             + [_const_spec(w.shape) for w in w_list] + [_const_spec(g.shape), _const_spec(b.shape)])
    return pl.pallas_call(
        functools.partial(_outproj_kernel, n_in=len(a_list), gated=gated),
        grid=(t // tm,),
        in_specs=specs,
        out_specs=row(d),
        out_shape=jax.ShapeDtypeStruct((t, d), _F32),
        compiler_params=_cparams("arbitrary"),
        name="outproj_ln",
    )(*ins)


def _group_rmsnorm_gate(y, za, ng):
    y = y * _silu(za)
    gw = y.shape[1] // _SSD_GROUPS
    parts = []
    for g in range(_SSD_GROUPS):
        yg = y[:, g * gw:(g + 1) * gw]
        ms = jnp.mean(yg * yg, axis=-1, keepdims=True)
        parts.append(yg * lax.rsqrt(ms + _EPS))
    return jnp.concatenate(parts, axis=-1) * ng


def _even_mix_kernel(za_ref, xbc_ref, dt_ref, glu_ref, zb_ref,
                     cw_ref, cb_ref, dtb_ref, alog_ref, d_ref, ng_ref,
                     fw_ref, fb_ref, fg_ref, fbeta_ref,
                     ya_o, yb_o, ssm_o, sc_o, cc_o,
                     xbuf, ubuf, state, ushift, cacc, *, q, nc):
    c = pl.program_id(1)
    inner = _SSD_HEADS * _SSD_HEAD_DIM
    gn = _SSD_GROUPS * _SSD_STATE
    xhalo = 8
    uhalo = 32

    @pl.when(c == 0)
    def _():
        xbuf[0:xhalo, :] = jnp.zeros((xhalo, xbuf.shape[1]), _F32)
        ubuf[0:uhalo, :] = jnp.zeros((uhalo, ubuf.shape[1]), _F32)
        state[...] = jnp.zeros(state.shape, _F32)

    xbuf[xhalo:xhalo + q, :] = xbc_ref[...]
    acc = jnp.broadcast_to(cb_ref[...], (q, xbuf.shape[1]))
    for k in range(_SSD_CONV):
        off = xhalo - (_SSD_CONV - 1) + k
        acc = acc + cw_ref[k:k + 1, :] * xbuf[off:off + q, :]
    xc = _silu(acc)
    xs = xc[:, :inner]
    bm = [xc[:, inner + g * _SSD_STATE: inner + (g + 1) * _SSD_STATE].astype(_BF) for g in range(_SSD_GROUPS)]
    cm = [xc[:, inner + gn + g * _SSD_STATE: inner + gn + (g + 1) * _SSD_STATE].astype(_BF)
          for g in range(_SSD_GROUPS)]

    @pl.when(c == nc - 1)
    def _():
        sc_o[...] = xbuf[xhalo + q - (_SSD_CONV - 1):xhalo + q, :]

    xbuf[0:xhalo, :] = xbuf[q:q + xhalo, :]

    lane = lax.broadcasted_iota(jnp.int32, (1, _LANES), 1)
    hmask = lane < _SSD_HEADS
    dt = jnp.where(hmask, _softplus(dt_ref[...] + dtb_ref[...]), 0.0)
    a = jnp.where(hmask, -jnp.exp(alog_ref[...]), 0.0)
    ri = lax.broadcasted_iota(jnp.int32, (q, q), 0)
    ci = lax.broadcasted_iota(jnp.int32, (q, q), 1)
    trilb = ri >= ci
    a_cum = _dot_hi(trilb.astype(_F32), dt * a)
    a_cum_t = a_cum.T
    dt_t = dt.T
    a_last_t = a_cum_t[:, q - 1:q]
    w_t = dt_t * jnp.exp(a_last_t - a_cum_t)
    ea = jnp.exp(a_cum)
    cd_t = jnp.exp(a_last_t)
    lo_lane = lane < _SSD_HEAD_DIM
    lo_sub = lax.broadcasted_iota(jnp.int32, (_LANES, 1), 0) < _SSD_HEAD_DIM

    cb = [_dot_nt(cm[g], bm[g]) for g in range(_SSD_GROUPS)]
    heads_per_group = _SSD_HEADS // _SSD_GROUPS
    ys = []
    for j in range(_SSD_HEADS // 2):
        h0, h1 = 2 * j, 2 * j + 1
        g = h0 // heads_per_group
        sl = slice(j * _LANES, (j + 1) * _LANES)
        xs2 = xs[:, sl]
        x2 = (xs2 * jnp.where(lo_lane, dt[:, h0:h0 + 1], dt[:, h1:h1 + 1])).astype(_BF)
        l0 = jnp.where(trilb, jnp.exp(a_cum[:, h0:h0 + 1] - a_cum_t[h0:h0 + 1, :]), 0.0)
        l1 = jnp.where(trilb, jnp.exp(a_cum[:, h1:h1 + 1] - a_cum_t[h1:h1 + 1, :]), 0.0)
        y0 = _dot((cb[g] * l0).astype(_BF), x2)
        y1 = _dot((cb[g] * l1).astype(_BF), x2)
        sp = state[sl, :]
        yoff = _dot_nt(cm[g], sp.astype(_BF)) * jnp.where(lo_lane, ea[:, h0:h0 + 1], ea[:, h1:h1 + 1])
        w2 = jnp.where(lo_sub, w_t[h0:h0 + 1, :], w_t[h1:h1 + 1, :])
        s_chunk = _dot((xs2.T * w2).astype(_BF), bm[g])
        cd2 = jnp.where(lo_sub, cd_t[h0:h0 + 1, :], cd_t[h1:h1 + 1, :])
        state[sl, :] = sp * cd2 + s_chunk
        ys.append(jnp.where(lo_lane, y0, y1) + yoff + d_ref[:, sl] * xs2)
    y = jnp.concatenate(ys, axis=-1)
    ya_o[...] = _group_rmsnorm_gate(y, za_ref[...], ng_ref[...]).astype(_BF)

    @pl.when(c == nc - 1)
    def _():
        ssm_o[...] = state[...]

    cf = ubuf.shape[1]
    u = glu_ref[:, :cf] * _sigmoid(glu_ref[:, cf:])
    ubuf[uhalo:uhalo + q, :] = u
    back = 8 * ((_CF_KERNEL - 1) // 8)
    for r in range(1, 8):
        ushift[r - 1] = ubuf[uhalo - back - r:uhalo + q - r, :]
    for cb in range(cf // _LANES):
        cs = slice(cb * _LANES, (cb + 1) * _LANES)
        acc = jnp.broadcast_to(fb_ref[:, cs], (q, _LANES))
        for j in range(_CF_KERNEL):
            r, a = j % 8, j // 8
            k = _CF_KERNEL - 1 - j
            if r == 0:
                win = ubuf[uhalo - 8 * a:uhalo - 8 * a + q, cs]
            else:
                win = ushift[r - 1, back - 8 * a:back - 8 * a + q, cs]
            acc = acc + fw_ref[k:k + 1, cs] * win
        cacc[:, cs] = acc
    v = _silu(_layer_norm(cacc[...], fg_ref[...], fbeta_ref[...]))
    yb_o[...] = (v * _silu(zb_ref[...])).astype(_BF)

    @pl.when(c == nc - 1)
    def _():
        cc_o[...] = ubuf[uhalo + q - (_CF_KERNEL - 1):uhalo + q, :]

    ubuf[0:uhalo, :] = ubuf[q:q + uhalo, :]


def _even_mix_prompt(za, xbc, dt, glu, zb, params, bsz, seq):
    q = 128
    nc = seq // q
    inner = _SSD_HEADS * _SSD_HEAD_DIM
    conv_dim = xbc.shape[1]
    cf = zb.shape[1]
    row = lambda n: pl.BlockSpec((q, n), lambda b, c: (b * nc + c, 0))
    per_b = lambda r, n: pl.BlockSpec((None, r, n), lambda b, c: (b, 0, 0))
    t = bsz * seq
    return pl.pallas_call(
        functools.partial(_even_mix_kernel, q=q, nc=nc),
        grid=(bsz, nc),
        in_specs=[row(inner), row(conv_dim), row(_LANES), row(2 * cf), row(cf)]
                 + [_const_spec(p.shape) for p in params],
        out_specs=[row(inner), row(cf), per_b(inner, _SSD_STATE),
                   per_b(_SSD_CONV - 1, conv_dim), per_b(_CF_KERNEL - 1, cf)],
        out_shape=[jax.ShapeDtypeStruct((t, inner), _BF), jax.ShapeDtypeStruct((t, cf), _BF),
                   jax.ShapeDtypeStruct((bsz, inner, _SSD_STATE), _F32),
                   jax.ShapeDtypeStruct((bsz, _SSD_CONV - 1, conv_dim), _F32),
                   jax.ShapeDtypeStruct((bsz, _CF_KERNEL - 1, cf), _F32)],
        scratch_shapes=[pltpu.VMEM((8 + q, conv_dim), _F32), pltpu.VMEM((32 + q, cf), _F32),
                        pltpu.VMEM((inner, _SSD_STATE), _F32),
                        pltpu.VMEM((7, q + 8 * ((_CF_KERNEL - 1) // 8), cf), _F32),
                        pltpu.VMEM((q, cf), _F32)],
        compiler_params=_cparams("arbitrary", "arbitrary"),
        name="even_mix_prompt",
    )(za, xbc, dt, glu, zb, *params)


def _sample_even_pre_kernel(xbc_ref, dt_ref, glu_ref, zb_ref, sctx_ref, cctx_ref,
                            cw_ref, cb_ref, dtb_ref, alog_ref, fw_ref, fb_ref, fg_ref, fbeta_ref,
                            xs_o, bm_o, cm_o, dect_o, xdtt_o, yb_o, sctx_o, cctx_o, pad):
    db = xbc_ref.shape[0]
    inner = _SSD_HEADS * _SSD_HEAD_DIM
    gn = _SSD_GROUPS * _SSD_STATE
    xbc = xbc_ref[...]
    acc = cb_ref[...] + cw_ref[_SSD_CONV - 1:_SSD_CONV, :] * xbc
    for k in range(_SSD_CONV - 1):
        acc = acc + cw_ref[k:k + 1, :] * sctx_ref[k]
    xc = _silu(acc)
    xs = xc[:, :inner]
    xs_o[...] = xs
    bm_o[...] = xc[:, inner:inner + gn]
    cm_o[...] = xc[:, inner + gn:inner + 2 * gn]
    for k in range(_SSD_CONV - 2):
        sctx_o[k] = sctx_ref[k + 1]
    sctx_o[_SSD_CONV - 2] = xbc

    lane = lax.broadcasted_iota(jnp.int32, (1, _LANES), 1)
    hmask = lane < _SSD_HEADS
    dt = jnp.where(hmask, _softplus(dt_ref[...] + dtb_ref[...]), 0.0)
    a = jnp.where(hmask, -jnp.exp(alog_ref[...]), 0.0)
    dec = jnp.where(hmask, jnp.exp(dt * a), 0.0)
    er = lax.broadcasted_iota(jnp.int32, (_LANES, inner), 0)
    ec = lax.broadcasted_iota(jnp.int32, (_LANES, inner), 1)
    expand = jnp.where((ec // _SSD_HEAD_DIM) == er, 1.0, 0.0)
    dec_x = _dot_hi(dec, expand)
    xdt = xs * _dot_hi(dt, expand)
    for src, dst in ((dec_x, dect_o), (xdt, xdtt_o)):
        pad[...] = jnp.zeros(pad.shape, _F32)
        pad[0:db, :] = src
        for j in range(inner // _LANES):
            dst[j * _LANES:(j + 1) * _LANES, :] = pad[:, j * _LANES:(j + 1) * _LANES].T

    cf = zb_ref.shape[1]
    u = glu_ref[:, :cf] * _sigmoid(glu_ref[:, cf:])
    acc = fb_ref[...] + fw_ref[_CF_KERNEL - 1:_CF_KERNEL, :] * u
    for k in range(_CF_KERNEL - 1):
        acc = acc + fw_ref[k:k + 1, :] * cctx_ref[k]
    v = _silu(_layer_norm(acc, fg_ref[...], fbeta_ref[...]))
    yb_o[...] = (v * _silu(zb_ref[...])).astype(_BF)
    for k in range(_CF_KERNEL - 2):
        cctx_o[k] = cctx_ref[k + 1]
    cctx_o[_CF_KERNEL - 2] = u


def _sample_even_pre(xbc, dt, glu, zb, sctx_t, cctx_t, params):
    db = xbc.shape[0]
    inner = _SSD_HEADS * _SSD_HEAD_DIM
    gn = _SSD_GROUPS * _SSD_STATE
    cf = zb.shape[1]
    ins = [xbc, dt, glu, zb, sctx_t, cctx_t, *params]
    full = lambda s: pl.BlockSpec(s, lambda i: (0,) * len(s))
    out_shapes = [(db, inner), (db, gn), (db, gn), (inner, _LANES), (inner, _LANES), (db, cf),
                  sctx_t.shape, cctx_t.shape]
    out_dtypes = [_F32, _F32, _F32, _F32, _F32, _BF, _F32, _F32]
    return pl.pallas_call(
        _sample_even_pre_kernel,
        grid=(1,),
        in_specs=[full(a.shape) for a in ins],
        out_specs=[full(s) for s in out_shapes],
        out_shape=[jax.ShapeDtypeStruct(s, d) for s, d in zip(out_shapes, out_dtypes)],
        scratch_shapes=[pltpu.VMEM((_LANES, inner), _F32)],
        compiler_params=_cparams("arbitrary"),
        name="sample_even_pre",
    )(*ins)


def _sample_even_rec_kernel(st_ref, dect_ref, xdtt_ref, bm_ref, cm_ref, xs_ref, za_ref, d_ref, ng_ref,
                            st_o, ya_o):
    b = pl.program_id(0)
    inner = st_ref.shape[0]
    half = inner // _SSD_GROUPS
    sel = jnp.where(lax.broadcasted_iota(jnp.int32, (_LANES, _LANES), 0) == b, 1.0, 0.0)
    dec = _dot_hi(dect_ref[...], sel)
    xd = _dot_hi(xdtt_ref[...], sel)
    grp0 = lax.broadcasted_iota(jnp.int32, (inner, 1), 0) < half
    brow = jnp.where(grp0, bm_ref[:, :_SSD_STATE], bm_ref[:, _SSD_STATE:])
    s_new = st_ref[...] * dec + xd * brow
    st_o[...] = s_new
    r8 = lax.broadcasted_iota(jnp.int32, (8, 1), 0)
    c8 = jnp.where(r8 == 0, cm_ref[:, :_SSD_STATE], jnp.where(r8 == 1, cm_ref[:, _SSD_STATE:], 0.0))
    yr = _dot_nt_hi(c8, s_new)
    lane = lax.broadcasted_iota(jnp.int32, (1, inner), 1)
    xs = xs_ref[...]
    y = jnp.where(lane < half, yr[0:1, :], yr[1:2, :]) + d_ref[...] * xs
    ya_o[...] = _group_rmsnorm_gate(y, za_ref[...], ng_ref[...]).astype(_BF)


def _sample_even_rec(state, dect, xdtt, bm, cm, xs, za, d_x, ng):
    db, inner, n = state.shape
    per_b = lambda s: pl.BlockSpec((None,) + s, lambda b: (b, 0, 0))
    gn = bm.shape[1]
    r3 = lambda a: a.reshape(db, 1, a.shape[1])
    return pl.pallas_call(
        _sample_even_rec_kernel,
        grid=(db,),
        in_specs=[per_b((inner, n)), _const_spec(dect.shape), _const_spec(xdtt.shape),
                  per_b((1, gn)), per_b((1, gn)), per_b((1, inner)), per_b((1, inner)),
                  _const_spec(d_x.shape), _const_spec(ng.shape)],
        out_specs=[per_b((inner, n)), per_b((1, inner))],
        out_shape=[jax.ShapeDtypeStruct((db, inner, n), _F32), jax.ShapeDtypeStruct((db, 1, inner), _BF)],
        compiler_params=_cparams("arbitrary"),
        name="sample_even_rec",
    )(state, dect, xdtt, r3(bm), r3(cm), r3(xs), r3(za), d_x, ng)


def _bisect_threshold(count_ge, rmin, rmax, n_adm, c_ge0, c_gt0, k):
    nonneg = c_ge0 >= k
    lo0 = jnp.where(n_adm <= k, rmin, jnp.where(nonneg, 0.0, rmin))
    c0 = jnp.where(n_adm <= k, n_adm, jnp.where(nonneg, c_ge0, n_adm))
    hi0 = jnp.where(c_gt0 >= k, rmax + (rmax - lo0), 0.0)

    def midpoint(lo, hi):
        return lo + (hi - lo) * 0.5

    def open_rows(lo, hi, c_lo, mid):
        return jnp.where(c_lo > k, jnp.where(mid > lo, jnp.where(mid < hi, 1.0, 0.0), 0.0), 0.0)

    def cond(st):
        it, _, _, _, _, todo = st
        return jnp.logical_and(it < _MAX_BISECT, jnp.max(todo) > 0.5)

    def body(st):
        it, lo, hi, c_lo, mid, _ = st
        c = count_ge(mid)
        ge = c >= k
        lo_n = jnp.where(ge, mid, lo)
        hi_n = jnp.where(ge, hi, mid)
        c_n = jnp.where(ge, c, c_lo)
        mid_n = midpoint(lo_n, hi_n)
        return it + 1, lo_n, hi_n, c_n, mid_n, open_rows(lo_n, hi_n, c_n, mid_n)

    mid0 = midpoint(lo0, hi0)
    st = (jnp.int32(0), lo0, hi0, c0, mid0, open_rows(lo0, hi0, c0, mid0))
    _, lo, _, c_lo, _, _ = lax.while_loop(cond, body, st)
    return lo, c_lo


def _attn_prompt_kernel(qi_ref, wi_ref, q_ref, z_ref, kit_ref, kt_ref, vg_ref, o_ref,
                        sc_ref, qh_ref, acc_ref, m_ref, al_ref, s_ref, p_ref, ob_ref,
                        *, tq, tkb, topk):
    i = pl.program_id(1)
    nkb = (i * tq + tq + tkb - 1) // tkb
    row = i * tq + lax.broadcasted_iota(jnp.int32, (tq, 1), 0)
    col0 = lax.broadcasted_iota(jnp.int32, (1, tkb), 1)
    kf = jnp.float32(topk)
    inf = jnp.float32(jnp.inf)

    qi = qi_ref[...]
    w = wi_ref[...] * (_IDX_HEADS ** -0.5 * _IDX_DIM ** -0.5)
    qis = [qi[:, h * _IDX_DIM:(h + 1) * _IDX_DIM] for h in range(_IDX_HEADS)]
    wcs = [w[:, _IDX_DIM + h:_IDX_DIM + h + 1] for h in range(_IDX_HEADS)]

    sw = 2 * _LANES
    nlb = tkb // _LANES

    def score_body(kb, carry):
        rmin, rmax, cge0, cgt0 = carry
        for jb in range(tkb // sw):
            kit = kit_ref[kb, :, jb * sw:(jb + 1) * sw]
            acc = jnp.zeros((tq, sw), _F32)
            for h in range(_IDX_HEADS):
                acc = acc + jnp.maximum(_dot(qis[h], kit), 0.0) * wcs[h]
            adm = (kb * tkb + jb * sw + col0[:, :sw]) <= row
            sc = jnp.where(adm, acc, -inf)
            sc_ref[kb, :, jb * sw:(jb + 1) * sw] = sc
            for lb in range(sw // _LANES):
                a = sc[:, lb * _LANES:(lb + 1) * _LANES]
                rmax = jnp.maximum(rmax, a)
                rmin = jnp.minimum(rmin, jnp.where(adm[:, lb * _LANES:(lb + 1) * _LANES], a, inf))
                cge0 = cge0 + jnp.where(a >= 0.0, 1.0, 0.0)
                cgt0 = cgt0 + jnp.where(a > 0.0, 1.0, 0.0)
        return rmin, rmax, cge0, cgt0

    zeros_l = jnp.zeros((tq, _LANES), _F32)
    rmin, rmax, cge0, cgt0 = lax.fori_loop(
        0, nkb, score_body,
        (jnp.full((tq, _LANES), inf, _F32), jnp.full((tq, _LANES), -inf, _F32), zeros_l, zeros_l))
    rmin = jnp.min(rmin, axis=1, keepdims=True)
    rmax = jnp.max(rmax, axis=1, keepdims=True)
    cge0 = jnp.sum(cge0, axis=1, keepdims=True)
    cgt0 = jnp.sum(cgt0, axis=1, keepdims=True)

    def count_ge(x):
        xb = jnp.broadcast_to(x, (tq, _LANES))

        def cbody(kb, cnt):
            for lb in range(nlb):
                cnt = cnt + jnp.where(sc_ref[kb, :, lb * _LANES:(lb + 1) * _LANES] >= xb, 1.0, 0.0)
            return cnt
        cnt = lax.fori_loop(0, nkb, cbody, jnp.zeros((tq, _LANES), _F32))
        return jnp.sum(cnt, axis=1, keepdims=True)

    n_adm = (row + 1).astype(_F32)
    lo, c_lo = _bisect_threshold(count_ge, rmin, rmax, n_adm, cge0, cgt0, kf)

    @pl.when(jnp.max(c_lo) > kf)
    def _():
        def gbody(kb, cnt):
            return cnt + jnp.sum(jnp.where(sc_ref[kb] > lo, 1.0, 0.0), axis=1, keepdims=True)
        need = kf - lax.fori_loop(0, nkb, gbody, jnp.zeros((tq, 1), _F32))
        tri = jnp.where(lax.broadcasted_iota(jnp.int32, (_LANES, _LANES), 0)
                        <= lax.broadcasted_iota(jnp.int32, (_LANES, _LANES), 1), 1.0, 0.0).astype(_BF)

        def tbody(kb, carry):
            for jb in range(tkb // _LANES):
                s = sc_ref[kb, :, jb * _LANES:(jb + 1) * _LANES]
                eq = s == lo
                pre = _dot(jnp.where(eq, 1.0, 0.0).astype(_BF), tri)
                drop = jnp.logical_and(eq, (carry + pre) > need)
                sc_ref[kb, :, jb * _LANES:(jb + 1) * _LANES] = jnp.where(drop, -inf, s)
                carry = carry + pre[:, _LANES - 1:_LANES]
            return carry
        lax.fori_loop(0, nkb, tbody, jnp.zeros((tq, 1), _F32))

    q = q_ref[...]
    for h in range(_ATTN_HEADS):
        qh_ref[h] = q[:, h * _ATTN_HEAD_DIM:(h + 1) * _ATTN_HEAD_DIM]
    m_ref[...] = jnp.full(m_ref.shape, _NEG, _F32)
    acc_ref[...] = jnp.zeros(acc_ref.shape, _F32)
    rep = _ATTN_HEADS // _ATTN_KV_HEADS

    def bias_body(kb, carry):
        sc_ref[kb] = jnp.where(sc_ref[kb] >= lo, 0.0, _NEG)
        return carry
    lax.fori_loop(0, nkb, bias_body, 0)

    def attn_body(kb, carry):
        start = pl.multiple_of(kb * tkb, tkb)
        for g in range(_ATTN_KV_HEADS):
            for r in range(rep):
                h = g * rep + r
                mx = None
                for hb in range(tkb // sw):
                    cs = slice(hb * sw, (hb + 1) * sw)
                    s = (_dot(qh_ref[h], kt_ref[kb, g * _ATTN_HEAD_DIM:(g + 1) * _ATTN_HEAD_DIM, cs])
                         + sc_ref[kb, :, cs])
                    s_ref[h, :, cs] = s
                    for lb in range(sw // _LANES):
                        part = s[:, lb * _LANES:(lb + 1) * _LANES]
                        mx = part if mx is None else jnp.maximum(mx, part)
                m_old = m_ref[h]
                m_new = jnp.maximum(m_old, jnp.max(mx, axis=1, keepdims=True))
                al_ref[h] = jnp.exp(m_old - m_new)
                m_ref[h] = m_new
        for h in range(_ATTN_HEADS):
            m_new = m_ref[h]
            for lb in range(nlb):
                cs = slice(lb * _LANES, (lb + 1) * _LANES)
                p_ref[h, :, cs] = jnp.exp(s_ref[h, :, cs] - m_new).astype(_BF)
        for g in range(_ATTN_KV_HEADS):
            vg = vg_ref[g, pl.ds(start, tkb), :]
            for r in range(rep):
                h = g * rep + r
                acc_ref[h] = acc_ref[h] * al_ref[h] + _dot(p_ref[h], vg)
        return carry

    lax.fori_loop(0, nkb, attn_body, 0)
    for h in range(_ATTN_HEADS):
        acc = acc_ref[h]
        o_h = acc / pltpu.roll(acc, _ATTN_HEAD_DIM, 1)
        ob_ref[:, h * _ATTN_HEAD_DIM:(h + 1) * _ATTN_HEAD_DIM] = o_h[:, :_ATTN_HEAD_DIM]
    o_ref[...] = (ob_ref[...] * _silu(z_ref[...])).astype(_BF)


def _attn_prompt(qi, wi, q, z, kit, kt, vg, bsz, seq, tkb):
    tq = 128
    nq = seq // tq
    nkbt = seq // tkb
    t = bsz * seq
    topk = min(_TOPK_MAX, seq // 4)
    width = q.shape[1]
    row = lambda n: pl.BlockSpec((tq, n), lambda b, i: (b * nq + i, 0))
    per_b = lambda s: pl.BlockSpec((None,) + s, lambda b, i: (b,) + (0,) * len(s),
                                   pipeline_mode=pl.Buffered(1))
    return pl.pallas_call(
        functools.partial(_attn_prompt_kernel, tq=tq, tkb=tkb, topk=topk),
        grid=(bsz, nq),
        in_specs=[row(qi.shape[1]), row(_LANES), row(width), row(width),
                  per_b(kit.shape[1:]), per_b(kt.shape[1:]), per_b(vg.shape[1:])],
        out_specs=row(width),
        out_shape=jax.ShapeDtypeStruct((t, width), _BF),
        scratch_shapes=[pltpu.VMEM((nkbt, tq, tkb), _F32),
                        pltpu.VMEM((_ATTN_HEADS, tq, _ATTN_HEAD_DIM), _BF),
                        pltpu.VMEM((_ATTN_HEADS, tq, _LANES), _F32),
                        pltpu.VMEM((_ATTN_HEADS, tq, _LANES), _F32),
                        pltpu.VMEM((_ATTN_HEADS, tq, _LANES), _F32),
                        pltpu.VMEM((_ATTN_HEADS, tq, tkb), _F32),
                        pltpu.VMEM((_ATTN_HEADS, tq, tkb), _BF),
                        pltpu.VMEM((tq, width), _F32)],
        compiler_params=_cparams("arbitrary", "arbitrary"),
        name="attn_prompt",
    )(qi, wi, q, z, kit, kt, vg)


def _attn_sample_kernel(pt_ref, qi_ref, w_ref, kin_ref, q_ref, kn_ref, vn_ref,
                        ckit_ref, ckt_ref, cvt_ref, o_ref,
                        kibuf, kbuf, vbuf, sc_ref, sn_ref, lg_ref, sem, *, n_pages, topk):
    b = pl.program_id(0)
    past = n_pages * _PAGE
    kf = jnp.float32(topk)
    inf = jnp.float32(jnp.inf)
    caches = ((ckit_ref, kibuf), (ckt_ref, kbuf), (cvt_ref, vbuf))
    unroll = min(8, n_pages)

    def page_copy(which, page):
        cache_ref, buf = caches[which]
        return pltpu.make_async_copy(cache_ref.at[pt_ref[b, page]], buf.at[page], sem.at[which])

    def start_all(which):
        lax.fori_loop(0, n_pages, lambda p, c: (page_copy(which, p).start(), c)[1], 0)

    def wait_all(which):
        lax.fori_loop(0, n_pages, lambda p, c: (page_copy(which, p).wait(), c)[1], 0)

    for which in range(3):
        start_all(which)

    def total(x):
        return jnp.sum(jnp.sum(x, axis=1, keepdims=True), axis=0, keepdims=True)

    qi8 = qi_ref[...].astype(_BF)
    w8 = w_ref[...] * (_IDX_HEADS ** -0.5 * _IDX_DIM ** -0.5)
    wait_all(0)

    def score_body(p, c):
        s8 = _dot(qi8, kibuf[p].astype(_BF))
        sc_ref[pl.ds(p, 1), :] = jnp.sum(jnp.maximum(s8, 0.0) * w8, axis=0, keepdims=True)
        return c
    lax.fori_loop(0, n_pages, score_body, 0, unroll=unroll)
    kin = kin_ref[...].astype(_BF).astype(_F32)
    sn8 = jnp.sum(qi8.astype(_F32) * kin, axis=1, keepdims=True)
    sn_ref[...] = jnp.sum(jnp.maximum(sn8, 0.0) * w8, axis=0, keepdims=True)
    snew = sn_ref[...]

    if past + 1 > topk:
        sc = sc_ref[...]
        rmax = jnp.maximum(jnp.max(jnp.max(sc, axis=1, keepdims=True), axis=0, keepdims=True), snew)
        rmin = jnp.minimum(jnp.min(jnp.min(sc, axis=1, keepdims=True), axis=0, keepdims=True), snew)
        one = lambda cnd: jnp.where(cnd, 1.0, 0.0)

        def count_ge(x):
            return total(one(sc_ref[...] >= x)) + one(snew >= x)

        lo, c_lo = _bisect_threshold(count_ge, rmin, rmax, jnp.full((1, 1), past + 1.0, _F32),
                                     count_ge(jnp.zeros((1, 1), _F32)),
                                     total(one(sc > 0.0)) + one(snew > 0.0), kf)

        @pl.when(jnp.max(c_lo) > kf)
        def _():
            need = kf - (total(one(sc_ref[...] > lo)) + one(snew > lo))
            tri = jnp.where(lax.broadcasted_iota(jnp.int32, (_LANES, _LANES), 0)
                            <= lax.broadcasted_iota(jnp.int32, (_LANES, _LANES), 1), 1.0, 0.0).astype(_BF)

            def tbody(p, carry):
                s = sc_ref[pl.ds(p, 1), :]
                eq = s == lo
                eq8 = jnp.broadcast_to(one(eq), (8, _LANES)).astype(_BF)
                pre = _dot(eq8, tri)[0:1, :]
                drop = jnp.logical_and(eq, (carry + pre) > need)
                sc_ref[pl.ds(p, 1), :] = jnp.where(drop, -inf, s)
                return carry + pre[:, _LANES - 1:_LANES]
            carry = lax.fori_loop(0, n_pages, tbody, jnp.zeros((1, 1), _F32))
            drop_new = jnp.logical_and(snew == lo, (carry + 1.0) > need)
            sn_ref[...] = jnp.where(drop_new, -inf, snew)
    else:
        lo = jnp.full((1, 1), -inf, _F32)

    kvw = _ATTN_KV_HEADS * _ATTN_HEAD_DIM
    rep = _ATTN_HEADS // _ATTN_KV_HEADS
    rr = lax.broadcasted_iota(jnp.int32, (_ATTN_HEAD_DIM, kvw), 0)
    rc = lax.broadcasted_iota(jnp.int32, (_ATTN_HEAD_DIM, kvw), 1)
    spread = jnp.where((rc % _ATTN_HEAD_DIM) == rr, 1.0, 0.0)
    hr = lax.broadcasted_iota(jnp.int32, (_ATTN_HEADS, kvw), 0)
    hc = lax.broadcasted_iota(jnp.int32, (_ATTN_HEADS, kvw), 1)
    own = (hc // _ATTN_HEAD_DIM) == (hr // rep)
    qbd = jnp.where(own, _dot_hi(q_ref[...], spread), 0.0).astype(_BF)
    kn = kn_ref[...].astype(_BF).astype(_F32)
    lnew = (jnp.sum(qbd.astype(_F32) * kn, axis=1, keepdims=True)
            + jnp.where(sn_ref[...] >= lo, 0.0, _NEG))
    wait_all(1)

    def logit_body(p, mx):
        lg = _dot(qbd, kbuf[p].astype(_BF)) + jnp.where(sc_ref[pl.ds(p, 1), :] >= lo, 0.0, _NEG)
        lg_ref[p] = lg
        return jnp.maximum(mx, lg)
    mx = lax.fori_loop(0, n_pages, logit_body, jnp.full((_ATTN_HEADS, _LANES), _NEG, _F32), unroll=unroll)
    m = jnp.maximum(jnp.max(mx, axis=1, keepdims=True), lnew)
    pn = jnp.exp(lnew - m)
    vn = vn_ref[...].astype(_BF).astype(_F32)
    wait_all(2)

    def pv_body(p, carry):
        psum, out = carry
        pr = jnp.exp(lg_ref[p] - m)
        return psum + pr, out + _dot_nt(pr.astype(_BF), vbuf[p].astype(_BF))
    psum, out = lax.fori_loop(0, n_pages, pv_body,
                              (jnp.zeros((_ATTN_HEADS, _LANES), _F32), pn.astype(_BF).astype(_F32) * vn),
                              unroll=unroll)
    den = jnp.sum(psum, axis=1, keepdims=True) + pn
    out = jnp.where(own, out / den, 0.0)
    gather = jnp.where((lax.broadcasted_iota(jnp.int32, (kvw, _ATTN_HEAD_DIM), 0) % _ATTN_HEAD_DIM)
                       == lax.broadcasted_iota(jnp.int32, (kvw, _ATTN_HEAD_DIM), 1), 1.0, 0.0)
    o_ref[...] = _dot_hi(out, gather)


def _attn_sample(page_table, qi, wi, ki_new, q, k_new, v_new, cache_ki, cache_k, cache_v):
    db, n_pages = page_table.shape
    past = n_pages * _PAGE
    topk = min(_TOPK_MAX, (past + 1) // 4)
    kvw = _ATTN_KV_HEADS * _ATTN_HEAD_DIM
    n_pool = cache_k.shape[0]
    per_b = lambda s: pl.BlockSpec((None,) + s, lambda b, pt: (b,) + (0,) * len(s))
    anyspec = pl.BlockSpec(memory_space=pl.ANY)
    kv_t = lambda c: jnp.transpose(c, (0, 2, 3, 1)).reshape(n_pool, kvw, _PAGE)
    ins = [qi.reshape(db, _IDX_HEADS, _IDX_DIM),
           wi[:, _IDX_DIM:_IDX_DIM + _IDX_HEADS].reshape(db, _IDX_HEADS, 1),
           ki_new.reshape(db, 1, _IDX_DIM),
           q.reshape(db, _ATTN_HEADS, _ATTN_HEAD_DIM),
           k_new.reshape(db, 1, kvw), v_new.reshape(db, 1, kvw),
           jnp.transpose(cache_ki, (0, 2, 1)), kv_t(cache_k), kv_t(cache_v)]
    grid_spec = pltpu.PrefetchScalarGridSpec(
        num_scalar_prefetch=1,
        grid=(db,),
        in_specs=[per_b((_IDX_HEADS, _IDX_DIM)), per_b((_IDX_HEADS, 1)), per_b((1, _IDX_DIM)),
                  per_b((_ATTN_HEADS, _ATTN_HEAD_DIM)), per_b((1, kvw)), per_b((1, kvw)),
                  anyspec, anyspec, anyspec],
        out_specs=per_b((_ATTN_HEADS, _ATTN_HEAD_DIM)),
        scratch_shapes=[pltpu.VMEM((n_pages, _IDX_DIM, _PAGE), _F32),
                        pltpu.VMEM((n_pages, kvw, _PAGE), _F32),
                        pltpu.VMEM((n_pages, kvw, _PAGE), _F32),
                        pltpu.VMEM((n_pages, _PAGE), _F32), pltpu.VMEM((1, 1), _F32),
                        pltpu.VMEM((n_pages, _ATTN_HEADS, _PAGE), _F32),
                        pltpu.SemaphoreType.DMA((3,))],
    )
    o = pl.pallas_call(
        functools.partial(_attn_sample_kernel, n_pages=n_pages, topk=topk),
        grid_spec=grid_spec,
        out_shape=jax.ShapeDtypeStruct((db, _ATTN_HEADS, _ATTN_HEAD_DIM), _F32),
        compiler_params=_cparams("arbitrary"),
        name="attn_sample",
    )(page_table, *ins)
    return o.reshape(db, _ATTN_HEADS * _ATTN_HEAD_DIM)


def _pad_lanes(a, n=_LANES):
    return jnp.pad(a, [(0, 0)] * (a.ndim - 1) + [(0, n - a.shape[-1])])


def _even_weights(w_in, conv_w, conv_b, dt_bias, a_log, d_skip, norm_g, cf_w, cf_b, cf_g, cf_beta, w_out):
    inner = _SSD_HEADS * _SSD_HEAD_DIM
    conv_dim = conv_w.shape[1]
    cf = cf_w.shape[1]
    o1, o2, o3, o4 = inner, inner + conv_dim, inner + conv_dim + _SSD_HEADS, inner + conv_dim + _SSD_HEADS + 2 * cf
    wb = w_in.astype(_BF)
    proj = (wb[:, :o1], wb[:, o1:o2], _pad_lanes(wb[:, o2:o3]), wb[:, o3:o4], wb[:, o4:])
    row = lambda v: v.reshape(1, -1)
    ssd = (conv_w, row(conv_b), _pad_lanes(row(dt_bias)), _pad_lanes(row(a_log)))
    d_x = row(jnp.repeat(d_skip, _SSD_HEAD_DIM))
    cfp = (cf_w, row(cf_b), row(cf_g), row(cf_beta))
    wo = w_out.astype(_BF)
    return proj, ssd, d_x, row(norm_g), cfp, (wo[:inner], wo[inner:])


def _even_layer_prompt(x, wts, ln_g, ln_b, bsz, seq):
    proj, ssd, d_x, ng, cfp, wo = wts
    za, xbc, dt, glu, zb = _even_proj(x, proj)
    ya, yb, ssm, sc, cc = _even_mix_prompt(za, xbc, dt, glu, zb, (*ssd, d_x, ng, *cfp), bsz, seq)
    x_new = _outproj_ln(x, [ya, yb], wo, ln_g, ln_b)
    return x_new, ssm.reshape(bsz, _SSD_HEADS, _SSD_HEAD_DIM, _SSD_STATE), sc, cc


def _even_layer_sample(x, st_ssm, st_sconv, st_cconv, wts, ln_g, ln_b):
    proj, ssd, d_x, ng, cfp, wo = wts
    db = x.shape[0]
    inner = _SSD_HEADS * _SSD_HEAD_DIM
    za, xbc, dt, glu, zb = _even_proj(x, proj)
    xs, bm, cm, dect, xdtt, yb, sctx_n, cctx_n = _sample_even_pre(
        xbc, dt, glu, zb, jnp.swapaxes(st_sconv, 0, 1), jnp.swapaxes(st_cconv, 0, 1), (*ssd, *cfp))
    st_new, ya = _sample_even_rec(st_ssm.reshape(db, inner, _SSD_STATE), dect, xdtt, bm, cm, xs, za, d_x, ng)
    x_new = _outproj_ln(x, [ya.reshape(db, inner), yb], wo, ln_g, ln_b)
    return (x_new, st_new.reshape(st_ssm.shape), jnp.swapaxes(sctx_n, 0, 1), jnp.swapaxes(cctx_n, 0, 1))


def _odd_weights(w_in, w_out):
    aw = _ATTN_HEADS * _ATTN_HEAD_DIM
    kvw = _ATTN_KV_HEADS * _ATTN_HEAD_DIM
    iw = _IDX_HEADS * _IDX_DIM
    o1, o2, o3, o4 = aw, aw + kvw, aw + 2 * kvw, aw + 2 * kvw + iw
    o5 = o4 + _IDX_DIM + _IDX_HEADS
    wb = w_in.astype(_BF)
    ws = (wb[:, :o1], wb[:, o1:o2], wb[:, o2:o3], wb[:, o3:o4], _pad_lanes(wb[:, o4:o5]), wb[:, o5:])
    wts = (wb[:, o1:o2].T, wb[:, o4:o4 + _IDX_DIM].T)
    return ws, wts, w_out.astype(_BF)


def _odd_layer_prompt(x, wts, ln_g, ln_b, bsz, seq):
    ws, wtr, wo = wts
    tkb = min(512, seq)
    q, k, v, qi, ki, wi, z, kt, kit, vg = _odd_proj_prompt(x, ws, wtr, bsz, seq, tkb)
    o = _attn_prompt(qi, wi, q, z, kit, kt, vg, bsz, seq, tkb)
    x_new = _outproj_ln(x, [o], [wo], ln_g, ln_b)
    return (x_new, k.reshape(bsz, seq, _ATTN_KV_HEADS, _ATTN_HEAD_DIM),
            v.reshape(bsz, seq, _ATTN_KV_HEADS, _ATTN_HEAD_DIM), ki.reshape(bsz, seq, _IDX_DIM))


def _odd_layer_sample(x, cache_k, cache_v, cache_ki, page_table, wts, ln_g, ln_b):
    ws, _, wo = wts
    db = x.shape[0]
    q, k, v, qi, ki, wi, z = _odd_proj_sample(x, ws)
    o = _attn_sample(page_table, qi, wi, ki, q, k, v, cache_ki, cache_k, cache_v)
    x_new = _outproj_ln(x, [o], [wo], ln_g, ln_b, z=z)
    return (x_new, k.reshape(db, 1, _ATTN_KV_HEADS, _ATTN_HEAD_DIM),
            v.reshape(db, 1, _ATTN_KV_HEADS, _ATTN_HEAD_DIM), ki.reshape(db, 1, _IDX_DIM))


def kernel(x_prompt, x_sample, state_ssm_l0, state_ssdconv_l0, state_cfconv_l0, cache_k_l1, cache_v_l1, cache_kidx_l1, state_ssm_l2, state_ssdconv_l2, state_cfconv_l2, cache_k_l3, cache_v_l3, cache_kidx_l3, page_table, w_in_even, ssd_conv_w, ssd_conv_b, ssd_dt_bias, ssd_a_log, ssd_d, ssd_norm_g, cf_dw_w, cf_dw_b, cf_ln_g, cf_ln_b, w_out_even, w_in_odd, w_out_odd, ln_g, ln_b):
    bsz, seq, d = x_prompt.shape
    db = x_sample.shape[0]
    ssm_states = (state_ssm_l0, state_ssm_l2)
    sconv_states = (state_ssdconv_l0, state_ssdconv_l2)
    cconv_states = (state_cfconv_l0, state_cfconv_l2)
    k_caches = (cache_k_l1, cache_k_l3)
    v_caches = (cache_v_l1, cache_v_l3)
    ki_caches = (cache_kidx_l1, cache_kidx_l3)
    yp = x_prompt.reshape(bsz * seq, d)
    ys = x_sample.reshape(db, d)
    new_state = []
    for layer in range(_DEPTH):
        j = layer // 2
        g, b = ln_g[layer].reshape(1, d), ln_b[layer].reshape(1, d)
        if layer % 2 == 0:
            wts = _even_weights(w_in_even[j], ssd_conv_w[j], ssd_conv_b[j], ssd_dt_bias[j], ssd_a_log[j],
                                ssd_d[j], ssd_norm_g[j], cf_dw_w[j], cf_dw_b[j], cf_ln_g[j], cf_ln_b[j],
                                w_out_even[j])
            yp, ssm_p, sc_p, cc_p = _even_layer_prompt(yp, wts, g, b, bsz, seq)
            ys, ssm_s, sc_s, cc_s = _even_layer_sample(ys, ssm_states[j], sconv_states[j], cconv_states[j],
                                                       wts, g, b)
            new_state += [ssm_p, ssm_s, sc_p, sc_s, cc_p, cc_s]
        else:
            wts = _odd_weights(w_in_odd[j], w_out_odd[j])
            yp, k_p, v_p, ki_p = _odd_layer_prompt(yp, wts, g, b, bsz, seq)
            ys, k_s, v_s, ki_s = _odd_layer_sample(ys, k_caches[j], v_caches[j], ki_caches[j], page_table,
                                                   wts, g, b)
            new_state += [k_p, k_s, v_p, v_s, ki_p, ki_s]
    return (yp.reshape(bsz, seq, d), ys.reshape(db, 1, d), *new_state)
```

```python
import functools

import jax
import jax.numpy as jnp
from jax import lax
from jax.experimental import pallas as pl
from jax.experimental.pallas import tpu as pltpu

_BF = jnp.bfloat16
_F32 = jnp.float32
_HI = lax.Precision.HIGHEST

_SSD_HEADS = 16
_SSD_HEAD_DIM = 64
_SSD_GROUPS = 2
_SSD_STATE = 128
_SSD_CONV = 4
_CF_KERNEL = 31
_ATTN_HEADS = 16
_ATTN_KV_HEADS = 4
_ATTN_HEAD_DIM = 64
_IDX_HEADS = 8
_IDX_DIM = 64
_TOPK_MAX = 256
_PAGE = 128
_DEPTH = 4
_ALPHA = (2 * _DEPTH) ** 0.25
_EPS = 1e-5

_LANES = 128
_VMEM_LIMIT = 56 * 1024 * 1024
_NEG = -1e30
_MAX_BISECT = 200


def _dot(a, b):
    return jnp.dot(a, b, preferred_element_type=_F32)


def _dot_nt(a, b):
    return lax.dot_general(a, b, (((1,), (1,)), ((), ())), preferred_element_type=_F32)


def _dot_hi(a, b):
    return jnp.dot(a, b, precision=_HI, preferred_element_type=_F32)


def _dot_nt_hi(a, b):
    return lax.dot_general(a, b, (((1,), (1,)), ((), ())), precision=_HI,
                           preferred_element_type=_F32)


def _sigmoid(x):
    return 1.0 / (1.0 + jnp.exp(-x))


def _silu(x):
    return x * _sigmoid(x)


def _softplus(x):
    return jnp.maximum(x, 0.0) + jnp.log1p(jnp.exp(-jnp.abs(x)))


def _layer_norm(x, g, b):
    mu = jnp.mean(x, axis=-1, keepdims=True)
    xc = x - mu
    var = jnp.mean(xc * xc, axis=-1, keepdims=True)
    return xc * lax.rsqrt(var + _EPS) * g + b


def _cparams(*sem):
    return pltpu.CompilerParams(dimension_semantics=sem, vmem_limit_bytes=_VMEM_LIMIT)


def _const_spec(shape):
    return pl.BlockSpec(shape, lambda *_: (0,) * len(shape), pipeline_mode=pl.Buffered(1))


def _even_proj_kernel(x_ref, wza, wxbc, wdt, wglu, wzb, za_o, xbc_o, dt_o, glu_o, zb_o):
    xb = x_ref[...].astype(_BF)
    za_o[...] = _dot(xb, wza[...])
    xbc_o[...] = _dot(xb, wxbc[...])
    dt_o[...] = _dot(xb, wdt[...])
    glu_o[...] = _dot(xb, wglu[...])
    zb_o[...] = _dot(xb, wzb[...])


def _even_proj(x, ws):
    t, d = x.shape
    tm = min(256, t)
    widths = [w.shape[1] for w in ws]
    row = lambda n: pl.BlockSpec((tm, n), lambda i: (i, 0))
    return pl.pallas_call(
        _even_proj_kernel,
        grid=(t // tm,),
        in_specs=[row(d)] + [_const_spec(w.shape) for w in ws],
        out_specs=[row(n) for n in widths],
        out_shape=[jax.ShapeDtypeStruct((t, n), _F32) for n in widths],
        compiler_params=_cparams("arbitrary"),
        name="even_proj",
    )(x, *ws)


def _odd_proj_prompt_kernel(x_ref, wq, wk, wv, wqi, wkw, wz, wkT, wkiT,
                            q_o, k_o, v_o, qi_o, ki_o, wi_o, z_o, kT_o, kiT_o, vg_o):
    xb = x_ref[...].astype(_BF)
    q_o[...] = (_dot(xb, wq[...]) * (_ATTN_HEAD_DIM ** -0.5)).astype(_BF)
    k_o[...] = _dot(xb, wk[...])
    v = _dot(xb, wv[...])
    v_o[...] = v
    first_half = lax.broadcasted_iota(jnp.int32, (1, _LANES), 1) < _ATTN_HEAD_DIM
    for g in range(_ATTN_KV_HEADS):
        blk = v[:, (g // 2) * _LANES:(g // 2 + 1) * _LANES]
        if g % 2:
            blk = pltpu.roll(blk, _ATTN_HEAD_DIM, 1)
        vg_o[g] = jnp.where(first_half, blk, 1.0).astype(_BF)
    qi_o[...] = _dot(xb, wqi[...]).astype(_BF)
    kw = _dot(xb, wkw[...])
    ki_o[...] = kw[:, :_IDX_DIM]
    wi_o[...] = kw
    z_o[...] = _dot(xb, wz[...])
    kT_o[...] = _dot_nt(wkT[...], xb).astype(_BF)
    kiT_o[...] = _dot_nt(wkiT[...], xb).astype(_BF)


def _odd_proj_prompt(x, ws, wts, bsz, seq, tkb):
    t, d = x.shape
    tm = tkb
    nb = seq // tm
    wq, wk, wv, wqi, wkw, wz = ws
    wkT, wkiT = wts
    kvw = wk.shape[1]
    row = lambda n: pl.BlockSpec((tm, n), lambda i: (i, 0))
    outs = [
        (jax.ShapeDtypeStruct((t, wq.shape[1]), _BF), row(wq.shape[1])),
        (jax.ShapeDtypeStruct((t, kvw), _F32), row(kvw)),
        (jax.ShapeDtypeStruct((t, kvw), _F32), row(kvw)),
        (jax.ShapeDtypeStruct((t, wqi.shape[1]), _BF), row(wqi.shape[1])),
        (jax.ShapeDtypeStruct((t, _IDX_DIM), _F32), row(_IDX_DIM)),
        (jax.ShapeDtypeStruct((t, _LANES), _F32), row(_LANES)),
        (jax.ShapeDtypeStruct((t, wz.shape[1]), _F32), row(wz.shape[1])),
        (jax.ShapeDtypeStruct((bsz, nb, kvw, tm), _BF),
         pl.BlockSpec((None, None, kvw, tm), lambda i: (i // nb, i % nb, 0, 0))),
        (jax.ShapeDtypeStruct((bsz, nb, _IDX_DIM, tm), _BF),
         pl.BlockSpec((None, None, _IDX_DIM, tm), lambda i: (i // nb, i % nb, 0, 0))),
        (jax.ShapeDtypeStruct((bsz, _ATTN_KV_HEADS, seq, _LANES), _BF),
         pl.BlockSpec((None, _ATTN_KV_HEADS, tm, _LANES), lambda i: (i // nb, 0, i % nb, 0))),
    ]
    return pl.pallas_call(
        _odd_proj_prompt_kernel,
        grid=(t // tm,),
        in_specs=[row(d)] + [_const_spec(w.shape) for w in (*ws, *wts)],
        out_specs=[o[1] for o in outs],
        out_shape=[o[0] for o in outs],
        compiler_params=_cparams("arbitrary"),
        name="odd_proj_prompt",
    )(x, *ws, *wts)


def _odd_proj_sample_kernel(x_ref, wq, wk, wv, wqi, wkw, wz, q_o, k_o, v_o, qi_o, ki_o, wi_o, z_o):
    xb = x_ref[...].astype(_BF)
    q_o[...] = _dot(xb, wq[...]) * (_ATTN_HEAD_DIM ** -0.5)
    k_o[...] = _dot(xb, wk[...])
    v_o[...] = _dot(xb, wv[...])
    qi_o[...] = _dot(xb, wqi[...])
    kw = _dot(xb, wkw[...])
    ki_o[...] = kw[:, :_IDX_DIM]
    wi_o[...] = kw
    z_o[...] = _dot(xb, wz[...])


def _odd_proj_sample(x, ws):
    t, d = x.shape
    wq, wk, wv, wqi, wkw, wz = ws
    widths = [wq.shape[1], wk.shape[1], wv.shape[1], wqi.shape[1], _IDX_DIM, _LANES, wz.shape[1]]
    full = lambda n: pl.BlockSpec((t, n), lambda i: (0, 0))
    return pl.pallas_call(
        _odd_proj_sample_kernel,
        grid=(1,),
        in_specs=[full(d)] + [_const_spec(w.shape) for w in ws],
        out_specs=[full(n) for n in widths],
        out_shape=[jax.ShapeDtypeStruct((t, n), _F32) for n in widths],
        compiler_params=_cparams("arbitrary"),
        name="odd_proj_sample",
    )(x, *ws)


def _outproj_kernel(*refs, n_in, gated):
    x_ref = refs[0]
    a_refs = refs[1:1 + n_in]
    pos = 1 + n_in
    z_ref = refs[pos] if gated else None
    pos += 1 if gated else 0
    w_refs = refs[pos:pos + n_in]
    g_ref, b_ref, o_ref = refs[pos + n_in:pos + n_in + 3]
    acc = _ALPHA * x_ref[...]
    for a_ref, w_ref in zip(a_refs, w_refs):
        a = a_ref[...]
        if gated:
            a = a.astype(_F32) * _silu(z_ref[...])
        acc = acc + _dot(a.astype(_BF), w_ref[...])
    o_ref[...] = _layer_norm(acc, g_ref[...], b_ref[...])


def _outproj_ln(x, a_list, w_list, g, b, z=None):
    t, d = x.shape
    tm = min(512, t)
    row = lambda n: pl.BlockSpec((tm, n), lambda i: (i, 0))
    gated = z is not None
    ins = [x, *a_list] + ([z] if gated else []) + [*w_list, g, b]
    specs = ([row(d)] + [row(a.shape[1]) for a in a_list] + ([row(z.shape[1])] if gated else [])
             + [_const_spec(w.shape) for w in w_list] + [_const_spec(g.shape), _const_spec(b.shape)])
    return pl.pallas_call(
        functools.partial(_outproj_kernel, n_in=len(a_list), gated=gated),
        grid=(t // tm,),
        in_specs=specs,
        out_specs=row(d),
        out_shape=jax.ShapeDtypeStruct((t, d), _F32),
        compiler_params=_cparams("arbitrary"),
        name="outproj_ln",
    )(*ins)


def _group_rmsnorm_gate(y, za, ng):
    y = y * _silu(za)
    gw = y.shape[1] // _SSD_GROUPS
    parts = []
    for g in range(_SSD_GROUPS):
        yg = y[:, g * gw:(g + 1) * gw]
        ms = jnp.mean(yg * yg, axis=-1, keepdims=True)
        parts.append(yg * lax.rsqrt(ms + _EPS))
    return jnp.concatenate(parts, axis=-1) * ng


def _even_mix_kernel(za_ref, xbc_ref, dt_ref, glu_ref, zb_ref,
                     cw_ref, cb_ref, dtb_ref, alog_ref, d_ref, ng_ref,
                     fw_ref, fb_ref, fg_ref, fbeta_ref,
                     ya_o, yb_o, ssm_o, sc_o, cc_o,
                     xbuf, ubuf, state, ushift, cacc, *, q, nc):
    c = pl.program_id(1)
    inner = _SSD_HEADS * _SSD_HEAD_DIM
    gn = _SSD_GROUPS * _SSD_STATE
    xhalo = 8
    uhalo = 32

    @pl.when(c == 0)
    def _():
        xbuf[0:xhalo, :] = jnp.zeros((xhalo, xbuf.shape[1]), _F32)
        ubuf[0:uhalo, :] = jnp.zeros((uhalo, ubuf.shape[1]), _F32)
        state[...] = jnp.zeros(state.shape, _F32)

    xbuf[xhalo:xhalo + q, :] = xbc_ref[...]
    acc = jnp.broadcast_to(cb_ref[...], (q, xbuf.shape[1]))
    for k in range(_SSD_CONV):
        off = xhalo - (_SSD_CONV - 1) + k
        acc = acc + cw_ref[k:k + 1, :] * xbuf[off:off + q, :]
    xc = _silu(acc)
    xs = xc[:, :inner]
    bm = [xc[:, inner + g * _SSD_STATE: inner + (g + 1) * _SSD_STATE].astype(_BF) for g in range(_SSD_GROUPS)]
    cm = [xc[:, inner + gn + g * _SSD_STATE: inner + gn + (g + 1) * _SSD_STATE].astype(_BF)
          for g in range(_SSD_GROUPS)]

    @pl.when(c == nc - 1)
    def _():
        sc_o[...] = xbuf[xhalo + q - (_SSD_CONV - 1):xhalo + q, :]

    xbuf[0:xhalo, :] = xbuf[q:q + xhalo, :]

    lane = lax.broadcasted_iota(jnp.int32, (1, _LANES), 1)
    hmask = lane < _SSD_HEADS
    dt = jnp.where(hmask, _softplus(dt_ref[...] + dtb_ref[...]), 0.0)
    a = jnp.where(hmask, -jnp.exp(alog_ref[...]), 0.0)
    ri = lax.broadcasted_iota(jnp.int32, (q, q), 0)
    ci = lax.broadcasted_iota(jnp.int32, (q, q), 1)
    trilb = ri >= ci
    a_cum = _dot_hi(trilb.astype(_F32), dt * a)
    a_cum_t = a_cum.T
    dt_t = dt.T
    a_last_t = a_cum_t[:, q - 1:q]
    w_t = dt_t * jnp.exp(a_last_t - a_cum_t)
    ea = jnp.exp(a_cum)
    cd_t = jnp.exp(a_last_t)
    lo_lane = lane < _SSD_HEAD_DIM
    lo_sub = lax.broadcasted_iota(jnp.int32, (_LANES, 1), 0) < _SSD_HEAD_DIM

    cb = [_dot_nt(cm[g], bm[g]) for g in range(_SSD_GROUPS)]
    heads_per_group = _SSD_HEADS // _SSD_GROUPS
    ys = []
    for j in range(_SSD_HEADS // 2):
        h0, h1 = 2 * j, 2 * j + 1
        g = h0 // heads_per_group
        sl = slice(j * _LANES, (j + 1) * _LANES)
        xs2 = xs[:, sl]
        x2 = (xs2 * jnp.where(lo_lane, dt[:, h0:h0 + 1], dt[:, h1:h1 + 1])).astype(_BF)
        l0 = jnp.where(trilb, jnp.exp(a_cum[:, h0:h0 + 1] - a_cum_t[h0:h0 + 1, :]), 0.0)
        l1 = jnp.where(trilb, jnp.exp(a_cum[:, h1:h1 + 1] - a_cum_t[h1:h1 + 1, :]), 0.0)
        y0 = _dot((cb[g] * l0).astype(_BF), x2)
        y1 = _dot((cb[g] * l1).astype(_BF), x2)
        sp = state[sl, :]
        yoff = _dot_nt(cm[g], sp.astype(_BF)) * jnp.where(lo_lane, ea[:, h0:h0 + 1], ea[:, h1:h1 + 1])
        w2 = jnp.where(lo_sub, w_t[h0:h0 + 1, :], w_t[h1:h1 + 1, :])
        s_chunk = _dot((xs2.T * w2).astype(_BF), bm[g])
        cd2 = jnp.where(lo_sub, cd_t[h0:h0 + 1, :], cd_t[h1:h1 + 1, :])
        state[sl, :] = sp * cd2 + s_chunk
        ys.append(jnp.where(lo_lane, y0, y1) + yoff + d_ref[:, sl] * xs2)
    y = jnp.concatenate(ys, axis=-1)
    ya_o[...] = _group_rmsnorm_gate(y, za_ref[...], ng_ref[...]).astype(_BF)

    @pl.when(c == nc - 1)
    def _():
        ssm_o[...] = state[...]

    cf = ubuf.shape[1]
    u = glu_ref[:, :cf] * _sigmoid(glu_ref[:, cf:])
    ubuf[uhalo:uhalo + q, :] = u
    back = 8 * ((_CF_KERNEL - 1) // 8)
    for r in range(1, 8):
        ushift[r - 1] = ubuf[uhalo - back - r:uhalo + q - r, :]
    for cb in range(cf // _LANES):
        cs = slice(cb * _LANES, (cb + 1) * _LANES)
        acc = jnp.broadcast_to(fb_ref[:, cs], (q, _LANES))
        for j in range(_CF_KERNEL):
            r, a = j % 8, j // 8
            k = _CF_KERNEL - 1 - j
            if r == 0:
                win = ubuf[uhalo - 8 * a:uhalo - 8 * a + q, cs]
            else:
                win = ushift[r - 1, back - 8 * a:back - 8 * a + q, cs]
            acc = acc + fw_ref[k:k + 1, cs] * win
        cacc[:, cs] = acc
    v = _silu(_layer_norm(cacc[...], fg_ref[...], fbeta_ref[...]))
    yb_o[...] = (v * _silu(zb_ref[...])).astype(_BF)

    @pl.when(c == nc - 1)
    def _():
        cc_o[...] = ubuf[uhalo + q - (_CF_KERNEL - 1):uhalo + q, :]

    ubuf[0:uhalo, :] = ubuf[q:q + uhalo, :]


def _even_mix_prompt(za, xbc, dt, glu, zb, params, bsz, seq):
    q = 128
    nc = seq // q
    inner = _SSD_HEADS * _SSD_HEAD_DIM
    conv_dim = xbc.shape[1]
    cf = zb.shape[1]
    row = lambda n: pl.BlockSpec((q, n), lambda b, c: (b * nc + c, 0))
    per_b = lambda r, n: pl.BlockSpec((None, r, n), lambda b, c: (b, 0, 0))
    t = bsz * seq
    return pl.pallas_call(
        functools.partial(_even_mix_kernel, q=q, nc=nc),
        grid=(bsz, nc),
        in_specs=[row(inner), row(conv_dim), row(_LANES), row(2 * cf), row(cf)]
                 + [_const_spec(p.shape) for p in params],
        out_specs=[row(inner), row(cf), per_b(inner, _SSD_STATE),
                   per_b(_SSD_CONV - 1, conv_dim), per_b(_CF_KERNEL - 1, cf)],
        out_shape=[jax.ShapeDtypeStruct((t, inner), _BF), jax.ShapeDtypeStruct((t, cf), _BF),
                   jax.ShapeDtypeStruct((bsz, inner, _SSD_STATE), _F32),
                   jax.ShapeDtypeStruct((bsz, _SSD_CONV - 1, conv_dim), _F32),
                   jax.ShapeDtypeStruct((bsz, _CF_KERNEL - 1, cf), _F32)],
        scratch_shapes=[pltpu.VMEM((8 + q, conv_dim), _F32), pltpu.VMEM((32 + q, cf), _F32),
                        pltpu.VMEM((inner, _SSD_STATE), _F32),
                        pltpu.VMEM((7, q + 8 * ((_CF_KERNEL - 1) // 8), cf), _F32),
                        pltpu.VMEM((q, cf), _F32)],
        compiler_params=_cparams("arbitrary", "arbitrary"),
        name="even_mix_prompt",
    )(za, xbc, dt, glu, zb, *params)


def _sample_even_pre_kernel(xbc_ref, dt_ref, glu_ref, zb_ref, sctx_ref, cctx_ref,
                            cw_ref, cb_ref, dtb_ref, alog_ref, fw_ref, fb_ref, fg_ref, fbeta_ref,
                            xs_o, bm_o, cm_o, dect_o, xdtt_o, yb_o, sctx_o, cctx_o, pad):
    db = xbc_ref.shape[0]
    inner = _SSD_HEADS * _SSD_HEAD_DIM
    gn = _SSD_GROUPS * _SSD_STATE
    xbc = xbc_ref[...]
    acc = cb_ref[...] + cw_ref[_SSD_CONV - 1:_SSD_CONV, :] * xbc
    for k in range(_SSD_CONV - 1):
        acc = acc + cw_ref[k:k + 1, :] * sctx_ref[k]
    xc = _silu(acc)
    xs = xc[:, :inner]
    xs_o[...] = xs
    bm_o[...] = xc[:, inner:inner + gn]
    cm_o[...] = xc[:, inner + gn:inner + 2 * gn]
    for k in range(_SSD_CONV - 2):
        sctx_o[k] = sctx_ref[k + 1]
    sctx_o[_SSD_CONV - 2] = xbc

    lane = lax.broadcasted_iota(jnp.int32, (1, _LANES), 1)
    hmask = lane < _SSD_HEADS
    dt = jnp.where(hmask, _softplus(dt_ref[...] + dtb_ref[...]), 0.0)
    a = jnp.where(hmask, -jnp.exp(alog_ref[...]), 0.0)
    dec = jnp.where(hmask, jnp.exp(dt * a), 0.0)
    er = lax.broadcasted_iota(jnp.int32, (_LANES, inner), 0)
    ec = lax.broadcasted_iota(jnp.int32, (_LANES, inner), 1)
    expand = jnp.where((ec // _SSD_HEAD_DIM) == er, 1.0, 0.0)
    dec_x = _dot_hi(dec, expand)
    xdt = xs * _dot_hi(dt, expand)
    for src, dst in ((dec_x, dect_o), (xdt, xdtt_o)):
        pad[...] = jnp.zeros(pad.shape, _F32)
        pad[0:db, :] = src
        for j in range(inner // _LANES):
            dst[j * _LANES:(j + 1) * _LANES, :] = pad[:, j * _LANES:(j + 1) * _LANES].T

    cf = zb_ref.shape[1]
    u = glu_ref[:, :cf] * _sigmoid(glu_ref[:, cf:])
    acc = fb_ref[...] + fw_ref[_CF_KERNEL - 1:_CF_KERNEL, :] * u
    for k in range(_CF_KERNEL - 1):
        acc = acc + fw_ref[k:k + 1, :] * cctx_ref[k]
    v = _silu(_layer_norm(acc, fg_ref[...], fbeta_ref[...]))
    yb_o[...] = (v * _silu(zb_ref[...])).astype(_BF)
    for k in range(_CF_KERNEL - 2):
        cctx_o[k] = cctx_ref[k + 1]
    cctx_o[_CF_KERNEL - 2] = u


def _sample_even_pre(xbc, dt, glu, zb, sctx_t, cctx_t, params):
    db = xbc.shape[0]
    inner = _SSD_HEADS * _SSD_HEAD_DIM
    gn = _SSD_GROUPS * _SSD_STATE
    cf = zb.shape[1]
    ins = [xbc, dt, glu, zb, sctx_t, cctx_t, *params]
    full = lambda s: pl.BlockSpec(s, lambda i: (0,) * len(s))
    out_shapes = [(db, inner), (db, gn), (db, gn), (inner, _LANES), (inner, _LANES), (db, cf),
                  sctx_t.shape, cctx_t.shape]
    out_dtypes = [_F32, _F32, _F32, _F32, _F32, _BF, _F32, _F32]
    return pl.pallas_call(
        _sample_even_pre_kernel,
        grid=(1,),
        in_specs=[full(a.shape) for a in ins],
        out_specs=[full(s) for s in out_shapes],
        out_shape=[jax.ShapeDtypeStruct(s, d) for s, d in zip(out_shapes, out_dtypes)],
        scratch_shapes=[pltpu.VMEM((_LANES, inner), _F32)],
        compiler_params=_cparams("arbitrary"),
        name="sample_even_pre",
    )(*ins)


def _sample_even_rec_kernel(st_ref, dect_ref, xdtt_ref, bm_ref, cm_ref, xs_ref, za_ref, d_ref, ng_ref,
                            st_o, ya_o):
    b = pl.program_id(0)
    inner = st_ref.shape[0]
    half = inner // _SSD_GROUPS
    sel = jnp.where(lax.broadcasted_iota(jnp.int32, (_LANES, _LANES), 0) == b, 1.0, 0.0)
    dec = _dot_hi(dect_ref[...], sel)
    xd = _dot_hi(xdtt_ref[...], sel)
    grp0 = lax.broadcasted_iota(jnp.int32, (inner, 1), 0) < half
    brow = jnp.where(grp0, bm_ref[:, :_SSD_STATE], bm_ref[:, _SSD_STATE:])
    s_new = st_ref[...] * dec + xd * brow
    st_o[...] = s_new
    r8 = lax.broadcasted_iota(jnp.int32, (8, 1), 0)
    c8 = jnp.where(r8 == 0, cm_ref[:, :_SSD_STATE], jnp.where(r8 == 1, cm_ref[:, _SSD_STATE:], 0.0))
    yr = _dot_nt_hi(c8, s_new)
    lane = lax.broadcasted_iota(jnp.int32, (1, inner), 1)
    xs = xs_ref[...]
    y = jnp.where(lane < half, yr[0:1, :], yr[1:2, :]) + d_ref[...] * xs
    ya_o[...] = _group_rmsnorm_gate(y, za_ref[...], ng_ref[...]).astype(_BF)


def _sample_even_rec(state, dect, xdtt, bm, cm, xs, za, d_x, ng):
    db, inner, n = state.shape
    per_b = lambda s: pl.BlockSpec((None,) + s, lambda b: (b, 0, 0))
    gn = bm.shape[1]
    r3 = lambda a: a.reshape(db, 1, a.shape[1])
    return pl.pallas_call(
        _sample_even_rec_kernel,
        grid=(db,),
        in_specs=[per_b((inner, n)), _const_spec(dect.shape), _const_spec(xdtt.shape),
                  per_b((1, gn)), per_b((1, gn)), per_b((1, inner)), per_b((1, inner)),
                  _const_spec(d_x.shape), _const_spec(ng.shape)],
        out_specs=[per_b((inner, n)), per_b((1, inner))],
        out_shape=[jax.ShapeDtypeStruct((db, inner, n), _F32), jax.ShapeDtypeStruct((db, 1, inner), _BF)],
        compiler_params=_cparams("arbitrary"),
        name="sample_even_rec",
    )(state, dect, xdtt, r3(bm), r3(cm), r3(xs), r3(za), d_x, ng)


def _bisect_threshold(count_ge, rmin, rmax, n_adm, c_ge0, c_gt0, k):
    nonneg = c_ge0 >= k
    lo0 = jnp.where(n_adm <= k, rmin, jnp.where(nonneg, 0.0, rmin))
    c0 = jnp.where(n_adm <= k, n_adm, jnp.where(nonneg, c_ge0, n_adm))
    hi0 = jnp.where(c_gt0 >= k, rmax + (rmax - lo0), 0.0)

    def midpoint(lo, hi):
        return lo + (hi - lo) * 0.5

    def open_rows(lo, hi, c_lo, mid):
        return jnp.where(c_lo > k, jnp.where(mid > lo, jnp.where(mid < hi, 1.0, 0.0), 0.0), 0.0)

    def cond(st):
        it, _, _, _, _, todo = st
        return jnp.logical_and(it < _MAX_BISECT, jnp.max(todo) > 0.5)

    def body(st):
        it, lo, hi, c_lo, mid, _ = st
        c = count_ge(mid)
        ge = c >= k
        lo_n = jnp.where(ge, mid, lo)
        hi_n = jnp.where(ge, hi, mid)
        c_n = jnp.where(ge, c, c_lo)
        mid_n = midpoint(lo_n, hi_n)
        return it + 1, lo_n, hi_n, c_n, mid_n, open_rows(lo_n, hi_n, c_n, mid_n)

    mid0 = midpoint(lo0, hi0)
    st = (jnp.int32(0), lo0, hi0, c0, mid0, open_rows(lo0, hi0, c0, mid0))
    _, lo, _, c_lo, _, _ = lax.while_loop(cond, body, st)
    return lo, c_lo


def _attn_prompt_kernel(qi_ref, wi_ref, q_ref, z_ref, kit_ref, kt_ref, vg_ref, o_ref,
                        sc_ref, qis_ref, qh_ref, acc_ref, m_ref, al_ref, s_ref, p_ref, ob_ref,
                        *, tq, tkb, topk):
    i = pl.program_id(1)
    nkb = (i * tq + tq + tkb - 1) // tkb
    row = i * tq + lax.broadcasted_iota(jnp.int32, (tq, 1), 0)
    col0 = lax.broadcasted_iota(jnp.int32, (1, tkb), 1)
    kf = jnp.float32(topk)
    inf = jnp.float32(jnp.inf)

    qi = qi_ref[...]
    for h in range(_IDX_HEADS):
        qis_ref[h * tq:(h + 1) * tq, :] = qi[:, h * _IDX_DIM:(h + 1) * _IDX_DIM]
    w = wi_ref[...] * (_IDX_HEADS ** -0.5 * _IDX_DIM ** -0.5)
    wcs = [w[:, _IDX_DIM + h:_IDX_DIM + h + 1] for h in range(_IDX_HEADS)]

    sw = 2 * _LANES
    nlb = tkb // _LANES

    def score_body(kb, carry):
        rmin, rmax, cge0, cgt0 = carry
        for jb in range(tkb // sw):
            s8 = _dot(qis_ref[...], kit_ref[kb, :, jb * sw:(jb + 1) * sw])
            acc = jnp.zeros((tq, sw), _F32)
            for h in range(_IDX_HEADS):
                acc = acc + jnp.maximum(s8[h * tq:(h + 1) * tq], 0.0) * wcs[h]
            adm = (kb * tkb + jb * sw + col0[:, :sw]) <= row
            sc = jnp.where(adm, acc, -inf)
            sc_ref[kb, :, jb * sw:(jb + 1) * sw] = sc
            for lb in range(sw // _LANES):
                a = sc[:, lb * _LANES:(lb + 1) * _LANES]
                rmax = jnp.maximum(rmax, a)
                rmin = jnp.minimum(rmin, jnp.where(adm[:, lb * _LANES:(lb + 1) * _LANES], a, inf))
                cge0 = cge0 + jnp.where(a >= 0.0, 1.0, 0.0)
                cgt0 = cgt0 + jnp.where(a > 0.0, 1.0, 0.0)
        return rmin, rmax, cge0, cgt0

    zeros_l = jnp.zeros((tq, _LANES), _F32)
    rmin, rmax, cge0, cgt0 = lax.fori_loop(
        0, nkb, score_body,
        (jnp.full((tq, _LANES), inf, _F32), jnp.full((tq, _LANES), -inf, _F32), zeros_l, zeros_l))
    rmin = jnp.min(rmin, axis=1, keepdims=True)
    rmax = jnp.max(rmax, axis=1, keepdims=True)
    cge0 = jnp.sum(cge0, axis=1, keepdims=True)
    cgt0 = jnp.sum(cgt0, axis=1, keepdims=True)

    def count_ge(x):
        xb = jnp.broadcast_to(x, (tq, _LANES))

        def cbody(kb, cnt):
            for lb in range(nlb):
                cnt = cnt + jnp.where(sc_ref[kb, :, lb * _LANES:(lb + 1) * _LANES] >= xb, 1.0, 0.0)
            return cnt
        cnt = lax.fori_loop(0, nkb, cbody, jnp.zeros((tq, _LANES), _F32))
        return jnp.sum(cnt, axis=1, keepdims=True)

    n_adm = (row + 1).astype(_F32)
    lo, c_lo = _bisect_threshold(count_ge, rmin, rmax, n_adm, cge0, cgt0, kf)

    @pl.when(jnp.max(c_lo) > kf)
    def _():
        def gbody(kb, cnt):
            return cnt + jnp.sum(jnp.where(sc_ref[kb] > lo, 1.0, 0.0), axis=1, keepdims=True)
        need = kf - lax.fori_loop(0, nkb, gbody, jnp.zeros((tq, 1), _F32))
        tri = jnp.where(lax.broadcasted_iota(jnp.int32, (_LANES, _LANES), 0)
                        <= lax.broadcasted_iota(jnp.int32, (_LANES, _LANES), 1), 1.0, 0.0).astype(_BF)

        def tbody(kb, carry):
            for jb in range(tkb // _LANES):
                s = sc_ref[kb, :, jb * _LANES:(jb + 1) * _LANES]
                eq = s == lo
                pre = _dot(jnp.where(eq, 1.0, 0.0).astype(_BF), tri)
                drop = jnp.logical_and(eq, (carry + pre) > need)
                sc_ref[kb, :, jb * _LANES:(jb + 1) * _LANES] = jnp.where(drop, -inf, s)
                carry = carry + pre[:, _LANES - 1:_LANES]
            return carry
        lax.fori_loop(0, nkb, tbody, jnp.zeros((tq, 1), _F32))

    q = q_ref[...]
    rep = _ATTN_HEADS // _ATTN_KV_HEADS
    for h in range(_ATTN_HEADS):
        g, r = divmod(h, rep)
        qh_ref[g, r * tq:(r + 1) * tq, :] = q[:, h * _ATTN_HEAD_DIM:(h + 1) * _ATTN_HEAD_DIM]
    m_ref[...] = jnp.full(m_ref.shape, _NEG, _F32)
    acc_ref[...] = jnp.zeros(acc_ref.shape, _F32)

    def bias_body(kb, carry):
        sc_ref[kb] = jnp.where(sc_ref[kb] >= lo, 0.0, _NEG)
        return carry
    lax.fori_loop(0, nkb, bias_body, 0)

    def attn_body(kb, carry):
        start = pl.multiple_of(kb * tkb, tkb)
        for g in range(_ATTN_KV_HEADS):
            mxs = [None] * rep
            for hb in range(tkb // sw):
                cs = slice(hb * sw, (hb + 1) * sw)
                s4 = _dot(qh_ref[g], kt_ref[kb, g * _ATTN_HEAD_DIM:(g + 1) * _ATTN_HEAD_DIM, cs])
                bias = sc_ref[kb, :, cs]
                for r in range(rep):
                    rows = slice(r * tq, (r + 1) * tq)
                    s = s4[rows] + bias
                    s_ref[g, rows, cs] = s
                    for lb in range(sw // _LANES):
                        part = s[:, lb * _LANES:(lb + 1) * _LANES]
                        mxs[r] = part if mxs[r] is None else jnp.maximum(mxs[r], part)
            for r in range(rep):
                rows = slice(r * tq, (r + 1) * tq)
                m_old = m_ref[g, rows, :]
                m_new = jnp.maximum(m_old, jnp.max(mxs[r], axis=1, keepdims=True))
                al_ref[g, rows, :] = jnp.exp(m_old - m_new)
                m_ref[g, rows, :] = m_new
        for g in range(_ATTN_KV_HEADS):
            for r in range(rep):
                rows = slice(r * tq, (r + 1) * tq)
                m_new = m_ref[g, rows, :]
                for lb in range(nlb):
                    cs = slice(lb * _LANES, (lb + 1) * _LANES)
                    p_ref[g, rows, cs] = jnp.exp(s_ref[g, rows, cs] - m_new).astype(_BF)
        for g in range(_ATTN_KV_HEADS):
            acc_ref[g] = acc_ref[g] * al_ref[g] + _dot(p_ref[g], vg_ref[g, pl.ds(start, tkb), :])
        return carry

    lax.fori_loop(0, nkb, attn_body, 0)
    for h in range(_ATTN_HEADS):
        g, r = divmod(h, rep)
        acc = acc_ref[g, r * tq:(r + 1) * tq, :]
        o_h = acc / pltpu.roll(acc, _ATTN_HEAD_DIM, 1)
        ob_ref[:, h * _ATTN_HEAD_DIM:(h + 1) * _ATTN_HEAD_DIM] = o_h[:, :_ATTN_HEAD_DIM]
    o_ref[...] = (ob_ref[...] * _silu(z_ref[...])).astype(_BF)


def _attn_prompt(qi, wi, q, z, kit, kt, vg, bsz, seq, tkb):
    tq = 128
    nq = seq // tq
    nkbt = seq // tkb
    t = bsz * seq
    topk = min(_TOPK_MAX, seq // 4)
    width = q.shape[1]
    rep = _ATTN_HEADS // _ATTN_KV_HEADS
    row = lambda n: pl.BlockSpec((tq, n), lambda b, i: (b * nq + i, 0))
    per_b = lambda s: pl.BlockSpec((None,) + s, lambda b, i: (b,) + (0,) * len(s),
                                   pipeline_mode=pl.Buffered(1))
    return pl.pallas_call(
        functools.partial(_attn_prompt_kernel, tq=tq, tkb=tkb, topk=topk),
        grid=(bsz, nq),
        in_specs=[row(qi.shape[1]), row(_LANES), row(width), row(width),
                  per_b(kit.shape[1:]), per_b(kt.shape[1:]), per_b(vg.shape[1:])],
        out_specs=row(width),
        out_shape=jax.ShapeDtypeStruct((t, width), _BF),
        scratch_shapes=[pltpu.VMEM((nkbt, tq, tkb), _F32),
                        pltpu.VMEM((_IDX_HEADS * tq, _IDX_DIM), _BF),
                        pltpu.VMEM((_ATTN_KV_HEADS, rep * tq, _ATTN_HEAD_DIM), _BF),
                        pltpu.VMEM((_ATTN_KV_HEADS, rep * tq, _LANES), _F32),
                        pltpu.VMEM((_ATTN_KV_HEADS, rep * tq, _LANES), _F32),
                        pltpu.VMEM((_ATTN_KV_HEADS, rep * tq, _LANES), _F32),
                        pltpu.VMEM((_ATTN_KV_HEADS, rep * tq, tkb), _F32),
                        pltpu.VMEM((_ATTN_KV_HEADS, rep * tq, tkb), _BF),
                        pltpu.VMEM((tq, width), _F32)],
        compiler_params=_cparams("arbitrary", "arbitrary"),
        name="attn_prompt",
    )(qi, wi, q, z, kit, kt, vg)


def _attn_sample_kernel(pt_ref, qi_ref, w_ref, kin_ref, q_ref, kn_ref, vn_ref,
                        ckit_ref, ckt_ref, cvt_ref, o_ref,
                        kibuf, kbuf, vbuf, sc_ref, sn_ref, lg_ref, sem, *, n_pages, topk):
    b = pl.program_id(0)
    past = n_pages * _PAGE
    kf = jnp.float32(topk)
    inf = jnp.float32(jnp.inf)
    caches = ((ckit_ref, kibuf), (ckt_ref, kbuf), (cvt_ref, vbuf))
    unroll = min(8, n_pages)

    def page_copy(which, page):
        cache_ref, buf = caches[which]
        return pltpu.make_async_copy(cache_ref.at[pt_ref[b, page]], buf.at[page], sem.at[which])

    def start_all(which):
        lax.fori_loop(0, n_pages, lambda p, c: (page_copy(which, p).start(), c)[1], 0)

    def wait_all(which):
        lax.fori_loop(0, n_pages, lambda p, c: (page_copy(which, p).wait(), c)[1], 0)

    for which in range(3):
        start_all(which)

    def total(x):
        return jnp.sum(jnp.sum(x, axis=1, keepdims=True), axis=0, keepdims=True)

    qi8 = qi_ref[...].astype(_BF)
    w8 = w_ref[...] * (_IDX_HEADS ** -0.5 * _IDX_DIM ** -0.5)
    wait_all(0)

    def score_body(p, c):
        s8 = _dot(qi8, kibuf[p].astype(_BF))
        sc_ref[pl.ds(p, 1), :] = jnp.sum(jnp.maximum(s8, 0.0) * w8, axis=0, keepdims=True)
        return c
    lax.fori_loop(0, n_pages, score_body, 0, unroll=unroll)
    kin = kin_ref[...].astype(_BF).astype(_F32)
    sn8 = jnp.sum(qi8.astype(_F32) * kin, axis=1, keepdims=True)
    sn_ref[...] = jnp.sum(jnp.maximum(sn8, 0.0) * w8, axis=0, keepdims=True)
    snew = sn_ref[...]

    if past + 1 > topk:
        sc = sc_ref[...]
        rmax = jnp.maximum(jnp.max(jnp.max(sc, axis=1, keepdims=True), axis=0, keepdims=True), snew)
        rmin = jnp.minimum(jnp.min(jnp.min(sc, axis=1, keepdims=True), axis=0, keepdims=True), snew)
        one = lambda cnd: jnp.where(cnd, 1.0, 0.0)

        def count_ge(x):
            return total(one(sc_ref[...] >= x)) + one(snew >= x)

        lo, c_lo = _bisect_threshold(count_ge, rmin, rmax, jnp.full((1, 1), past + 1.0, _F32),
                                     count_ge(jnp.zeros((1, 1), _F32)),
                                     total(one(sc > 0.0)) + one(snew > 0.0), kf)

        @pl.when(jnp.max(c_lo) > kf)
        def _():
            need = kf - (total(one(sc_ref[...] > lo)) + one(snew > lo))
            tri = jnp.where(lax.broadcasted_iota(jnp.int32, (_LANES, _LANES), 0)
                            <= lax.broadcasted_iota(jnp.int32, (_LANES, _LANES), 1), 1.0, 0.0).astype(_BF)

            def tbody(p, carry):
                s = sc_ref[pl.ds(p, 1), :]
                eq = s == lo
                eq8 = jnp.broadcast_to(one(eq), (8, _LANES)).astype(_BF)
                pre = _dot(eq8, tri)[0:1, :]
                drop = jnp.logical_and(eq, (carry + pre) > need)
                sc_ref[pl.ds(p, 1), :] = jnp.where(drop, -inf, s)
                return carry + pre[:, _LANES - 1:_LANES]
            carry = lax.fori_loop(0, n_pages, tbody, jnp.zeros((1, 1), _F32))
            drop_new = jnp.logical_and(snew == lo, (carry + 1.0) > need)
            sn_ref[...] = jnp.where(drop_new, -inf, snew)
    else:
        lo = jnp.full((1, 1), -inf, _F32)

    kvw = _ATTN_KV_HEADS * _ATTN_HEAD_DIM
    rep = _ATTN_HEADS // _ATTN_KV_HEADS
    rr = lax.broadcasted_iota(jnp.int32, (_ATTN_HEAD_DIM, kvw), 0)
    rc = lax.broadcasted_iota(jnp.int32, (_ATTN_HEAD_DIM, kvw), 1)
    spread = jnp.where((rc % _ATTN_HEAD_DIM) == rr, 1.0, 0.0)
    hr = lax.broadcasted_iota(jnp.int32, (_ATTN_HEADS, kvw), 0)
    hc = lax.broadcasted_iota(jnp.int32, (_ATTN_HEADS, kvw), 1)
    own = (hc // _ATTN_HEAD_DIM) == (hr // rep)
    qbd = jnp.where(own, _dot_hi(q_ref[...], spread), 0.0).astype(_BF)
    kn = kn_ref[...].astype(_BF).astype(_F32)
    lnew = (jnp.sum(qbd.astype(_F32) * kn, axis=1, keepdims=True)
            + jnp.where(sn_ref[...] >= lo, 0.0, _NEG))
    wait_all(1)

    def logit_body(p, mx):
        lg = _dot(qbd, kbuf[p].astype(_BF)) + jnp.where(sc_ref[pl.ds(p, 1), :] >= lo, 0.0, _NEG)
        lg_ref[p] = lg
        return jnp.maximum(mx, lg)
    mx = lax.fori_loop(0, n_pages, logit_body, jnp.full((_ATTN_HEADS, _LANES), _NEG, _F32), unroll=unroll)
    m = jnp.maximum(jnp.max(mx, axis=1, keepdims=True), lnew)
    pn = jnp.exp(lnew - m)
    vn = vn_ref[...].astype(_BF).astype(_F32)
    wait_all(2)

    def pv_body(p, carry):
        psum, out = carry
        pr = jnp.exp(lg_ref[p] - m)
        return psum + pr, out + _dot_nt(pr.astype(_BF), vbuf[p].astype(_BF))
    psum, out = lax.fori_loop(0, n_pages, pv_body,
                              (jnp.zeros((_ATTN_HEADS, _LANES), _F32), pn.astype(_BF).astype(_F32) * vn),
                              unroll=unroll)
    den = jnp.sum(psum, axis=1, keepdims=True) + pn
    out = jnp.where(own, out / den, 0.0)
    gather = jnp.where((lax.broadcasted_iota(jnp.int32, (kvw, _ATTN_HEAD_DIM), 0) % _ATTN_HEAD_DIM)
                       == lax.broadcasted_iota(jnp.int32, (kvw, _ATTN_HEAD_DIM), 1), 1.0, 0.0)
    o_ref[...] = _dot_hi(out, gather)


def _attn_sample(page_table, qi, wi, ki_new, q, k_new, v_new, cache_ki, cache_k, cache_v):
    db, n_pages = page_table.shape
    past = n_pages * _PAGE
    topk = min(_TOPK_MAX, (past + 1) // 4)
    kvw = _ATTN_KV_HEADS * _ATTN_HEAD_DIM
    n_pool = cache_k.shape[0]
    per_b = lambda s: pl.BlockSpec((None,) + s, lambda b, pt: (b,) + (0,) * len(s))
    anyspec = pl.BlockSpec(memory_space=pl.ANY)
    kv_t = lambda c: jnp.transpose(c, (0, 2, 3, 1)).reshape(n_pool, kvw, _PAGE)
    ins = [qi.reshape(db, _IDX_HEADS, _IDX_DIM),
           wi[:, _IDX_DIM:_IDX_DIM + _IDX_HEADS].reshape(db, _IDX_HEADS, 1),
           ki_new.reshape(db, 1, _IDX_DIM),
           q.reshape(db, _ATTN_HEADS, _ATTN_HEAD_DIM),
           k_new.reshape(db, 1, kvw), v_new.reshape(db, 1, kvw),
           jnp.transpose(cache_ki, (0, 2, 1)), kv_t(cache_k), kv_t(cache_v)]
    grid_spec = pltpu.PrefetchScalarGridSpec(
        num_scalar_prefetch=1,
        grid=(db,),
        in_specs=[per_b((_IDX_HEADS, _IDX_DIM)), per_b((_IDX_HEADS, 1)), per_b((1, _IDX_DIM)),
                  per_b((_ATTN_HEADS, _ATTN_HEAD_DIM)), per_b((1, kvw)), per_b((1, kvw)),
                  anyspec, anyspec, anyspec],
        out_specs=per_b((_ATTN_HEADS, _ATTN_HEAD_DIM)),
        scratch_shapes=[pltpu.VMEM((n_pages, _IDX_DIM, _PAGE), _F32),
                        pltpu.VMEM((n_pages, kvw, _PAGE), _F32),
                        pltpu.VMEM((n_pages, kvw, _PAGE), _F32),
                        pltpu.VMEM((n_pages, _PAGE), _F32), pltpu.VMEM((1, 1), _F32),
                        pltpu.VMEM((n_pages, _ATTN_HEADS, _PAGE), _F32),
                        pltpu.SemaphoreType.DMA((3,))],
    )
    o = pl.pallas_call(
        functools.partial(_attn_sample_kernel, n_pages=n_pages, topk=topk),
        grid_spec=grid_spec,
        out_shape=jax.ShapeDtypeStruct((db, _ATTN_HEADS, _ATTN_HEAD_DIM), _F32),
        compiler_params=_cparams("arbitrary"),
        name="attn_sample",
    )(page_table, *ins)
    return o.reshape(db, _ATTN_HEADS * _ATTN_HEAD_DIM)


def _pad_lanes(a, n=_LANES):
    return jnp.pad(a, [(0, 0)] * (a.ndim - 1) + [(0, n - a.shape[-1])])


def _even_weights(w_in, conv_w, conv_b, dt_bias, a_log, d_skip, norm_g, cf_w, cf_b, cf_g, cf_beta, w_out):
    inner = _SSD_HEADS * _SSD_HEAD_DIM
    conv_dim = conv_w.shape[1]
    cf = cf_w.shape[1]
    o1, o2, o3, o4 = inner, inner + conv_dim, inner + conv_dim + _SSD_HEADS, inner + conv_dim + _SSD_HEADS + 2 * cf
    wb = w_in.astype(_BF)
    proj = (wb[:, :o1], wb[:, o1:o2], _pad_lanes(wb[:, o2:o3]), wb[:, o3:o4], wb[:, o4:])
    row = lambda v: v.reshape(1, -1)
    ssd = (conv_w, row(conv_b), _pad_lanes(row(dt_bias)), _pad_lanes(row(a_log)))
    d_x = row(jnp.repeat(d_skip, _SSD_HEAD_DIM))
    cfp = (cf_w, row(cf_b), row(cf_g), row(cf_beta))
    wo = w_out.astype(_BF)
    return proj, ssd, d_x, row(norm_g), cfp, (wo[:inner], wo[inner:])


def _even_layer_prompt(x, wts, ln_g, ln_b, bsz, seq):
    proj, ssd, d_x, ng, cfp, wo = wts
    za, xbc, dt, glu, zb = _even_proj(x, proj)
    ya, yb, ssm, sc, cc = _even_mix_prompt(za, xbc, dt, glu, zb, (*ssd, d_x, ng, *cfp), bsz, seq)
    x_new = _outproj_ln(x, [ya, yb], wo, ln_g, ln_b)
    return x_new, ssm.reshape(bsz, _SSD_HEADS, _SSD_HEAD_DIM, _SSD_STATE), sc, cc


def _even_layer_sample(x, st_ssm, st_sconv, st_cconv, wts, ln_g, ln_b):
    proj, ssd, d_x, ng, cfp, wo = wts
    db = x.shape[0]
    inner = _SSD_HEADS * _SSD_HEAD_DIM
    za, xbc, dt, glu, zb = _even_proj(x, proj)
    xs, bm, cm, dect, xdtt, yb, sctx_n, cctx_n = _sample_even_pre(
        xbc, dt, glu, zb, jnp.swapaxes(st_sconv, 0, 1), jnp.swapaxes(st_cconv, 0, 1), (*ssd, *cfp))
    st_new, ya = _sample_even_rec(st_ssm.reshape(db, inner, _SSD_STATE), dect, xdtt, bm, cm, xs, za, d_x, ng)
    x_new = _outproj_ln(x, [ya.reshape(db, inner), yb], wo, ln_g, ln_b)
    return (x_new, st_new.reshape(st_ssm.shape), jnp.swapaxes(sctx_n, 0, 1), jnp.swapaxes(cctx_n, 0, 1))


def _odd_weights(w_in, w_out):
    aw = _ATTN_HEADS * _ATTN_HEAD_DIM
    kvw = _ATTN_KV_HEADS * _ATTN_HEAD_DIM
    iw = _IDX_HEADS * _IDX_DIM
    o1, o2, o3, o4 = aw, aw + kvw, aw + 2 * kvw, aw + 2 * kvw + iw
    o5 = o4 + _IDX_DIM + _IDX_HEADS
    wb = w_in.astype(_BF)
    ws = (wb[:, :o1], wb[:, o1:o2], wb[:, o2:o3], wb[:, o3:o4], _pad_lanes(wb[:, o4:o5]), wb[:, o5:])
    wts = (wb[:, o1:o2].T, wb[:, o4:o4 + _IDX_DIM].T)
    return ws, wts, w_out.astype(_BF)


def _odd_layer_prompt(x, wts, ln_g, ln_b, bsz, seq):
    ws, wtr, wo = wts
    tkb = min(512, seq)
    q, k, v, qi, ki, wi, z, kt, kit, vg = _odd_proj_prompt(x, ws, wtr, bsz, seq, tkb)
    o = _attn_prompt(qi, wi, q, z, kit, kt, vg, bsz, seq, tkb)
    x_new = _outproj_ln(x, [o], [wo], ln_g, ln_b)
    return (x_new, k.reshape(bsz, seq, _ATTN_KV_HEADS, _ATTN_HEAD_DIM),
            v.reshape(bsz, seq, _ATTN_KV_HEADS, _ATTN_HEAD_DIM), ki.reshape(bsz, seq, _IDX_DIM))


def _odd_layer_sample(x, cache_k, cache_v, cache_ki, page_table, wts, ln_g, ln_b):
    ws, _, wo = wts
    db = x.shape[0]
    q, k, v, qi, ki, wi, z = _odd_proj_sample(x, ws)
    o = _attn_sample(page_table, qi, wi, ki, q, k, v, cache_ki, cache_k, cache_v)
    x_new = _outproj_ln(x, [o], [wo], ln_g, ln_b, z=z)
    return (x_new, k.reshape(db, 1, _ATTN_KV_HEADS, _ATTN_HEAD_DIM),
            v.reshape(db, 1, _ATTN_KV_HEADS, _ATTN_HEAD_DIM), ki.reshape(db, 1, _IDX_DIM))


def kernel(x_prompt, x_sample, state_ssm_l0, state_ssdconv_l0, state_cfconv_l0, cache_k_l1, cache_v_l1, cache_kidx_l1, state_ssm_l2, state_ssdconv_l2, state_cfconv_l2, cache_k_l3, cache_v_l3, cache_kidx_l3, page_table, w_in_even, ssd_conv_w, ssd_conv_b, ssd_dt_bias, ssd_a_log, ssd_d, ssd_norm_g, cf_dw_w, cf_dw_b, cf_ln_g, cf_ln_b, w_out_even, w_in_odd, w_out_odd, ln_g, ln_b):
    bsz, seq, d = x_prompt.shape
    db = x_sample.shape[0]
    ssm_states = (state_ssm_l0, state_ssm_l2)
    sconv_states = (state_ssdconv_l0, state_ssdconv_l2)
    cconv_states = (state_cfconv_l0, state_cfconv_l2)
    k_caches = (cache_k_l1, cache_k_l3)
    v_caches = (cache_v_l1, cache_v_l3)
    ki_caches = (cache_kidx_l1, cache_kidx_l3)
    yp = x_prompt.reshape(bsz * seq, d)
    ys = x_sample.reshape(db, d)
    new_state = []
    for layer in range(_DEPTH):
        j = layer // 2
        g, b = ln_g[layer].reshape(1, d), ln_b[layer].reshape(1, d)
        if layer % 2 == 0:
            wts = _even_weights(w_in_even[j], ssd_conv_w[j], ssd_conv_b[j], ssd_dt_bias[j], ssd_a_log[j],
                                ssd_d[j], ssd_norm_g[j], cf_dw_w[j], cf_dw_b[j], cf_ln_g[j], cf_ln_b[j],
                                w_out_even[j])
            yp, ssm_p, sc_p, cc_p = _even_layer_prompt(yp, wts, g, b, bsz, seq)
            ys, ssm_s, sc_s, cc_s = _even_layer_sample(ys, ssm_states[j], sconv_states[j], cconv_states[j],
                                                       wts, g, b)
            new_state += [ssm_p, ssm_s, sc_p, sc_s, cc_p, cc_s]
        else:
            wts = _odd_weights(w_in_odd[j], w_out_odd[j])
            yp, k_p, v_p, ki_p = _odd_layer_prompt(yp, wts, g, b, bsz, seq)
            ys, k_s, v_s, ki_s = _odd_layer_sample(ys, k_caches[j], v_caches[j], ki_caches[j], page_table,
                                                   wts, g, b)
            new_state += [k_p, k_s, v_p, v_s, ki_p, ki_s]
    return (yp.reshape(bsz, seq, d), ys.reshape(db, 1, d), *new_state)
```

```python
import functools

import jax
import jax.numpy as jnp
from jax import lax
from jax.experimental import pallas as pl
from jax.experimental.pallas import tpu as pltpu

_BF = jnp.bfloat16
_F32 = jnp.float32
_HI = lax.Precision.HIGHEST

_SSD_HEADS = 16
_SSD_HEAD_DIM = 64
_SSD_GROUPS = 2
_SSD_STATE = 128
_SSD_CONV = 4
_CF_KERNEL = 31
_ATTN_HEADS = 16
_ATTN_KV_HEADS = 4
_ATTN_HEAD_DIM = 64
_IDX_HEADS = 8
_IDX_DIM = 64
_TOPK_MAX = 256
_PAGE = 128
_DEPTH = 4
_ALPHA = (2 * _DEPTH) ** 0.25
_EPS = 1e-5

_LANES = 128
_VMEM_LIMIT = 56 * 1024 * 1024
_NEG = -1e30
_MAX_BISECT = 200


def _dot(a, b):
    return jnp.dot(a, b, preferred_element_type=_F32)


def _dot_nt(a, b):
    return lax.dot_general(a, b, (((1,), (1,)), ((), ())), preferred_element_type=_F32)


def _dot_hi(a, b):
    return jnp.dot(a, b, precision=_HI, preferred_element_type=_F32)


def _dot_nt_hi(a, b):
    return lax.dot_general(a, b, (((1,), (1,)), ((), ())), precision=_HI,
                           preferred_element_type=_F32)


def _sigmoid(x):
    return 1.0 / (1.0 + jnp.exp(-x))


def _silu(x):
    return x * _sigmoid(x)


def _softplus(x):
    return jnp.maximum(x, 0.0) + jnp.log1p(jnp.exp(-jnp.abs(x)))


def _layer_norm(x, g, b):
    mu = jnp.mean(x, axis=-1, keepdims=True)
    xc = x - mu
    var = jnp.mean(xc * xc, axis=-1, keepdims=True)
    return xc * lax.rsqrt(var + _EPS) * g + b


def _cparams(*sem):
    return pltpu.CompilerParams(dimension_semantics=sem, vmem_limit_bytes=_VMEM_LIMIT)


def _const_spec(shape):
    return pl.BlockSpec(shape, lambda *_: (0,) * len(shape), pipeline_mode=pl.Buffered(1))


def _even_proj_kernel(x_ref, wza, wxbc, wdt, wglu, wzb, za_o, xbc_o, dt_o, glu_o, zb_o):
    xb = x_ref[...].astype(_BF)
    za_o[...] = _dot(xb, wza[...])
    xbc_o[...] = _dot(xb, wxbc[...])
    dt_o[...] = _dot(xb, wdt[...])
    glu_o[...] = _dot(xb, wglu[...])
    zb_o[...] = _dot(xb, wzb[...])


def _even_proj(x, ws):
    t, d = x.shape
    tm = min(256, t)
    widths = [w.shape[1] for w in ws]
    row = lambda n: pl.BlockSpec((tm, n), lambda i: (i, 0))
    return pl.pallas_call(
        _even_proj_kernel,
        grid=(t // tm,),
        in_specs=[row(d)] + [_const_spec(w.shape) for w in ws],
        out_specs=[row(n) for n in widths],
        out_shape=[jax.ShapeDtypeStruct((t, n), _F32) for n in widths],
        compiler_params=_cparams("arbitrary"),
        name="even_proj",
    )(x, *ws)


def _odd_proj_prompt_kernel(x_ref, wk, wv, wkw, wz, wq_t, wqi_t, wkw_t, wv_t,
                            k_o, v_o, ki_o, z_o, qt_o, qit_o, wit_o, kib_o, kg_o, vt_o):
    xb = x_ref[...].astype(_BF)
    k = _dot(xb, wk[...])
    k_o[...] = k
    v_o[...] = _dot(xb, wv[...])
    kw = _dot(xb, wkw[...])
    ki_o[...] = kw[:, :_IDX_DIM]
    kib_o[...] = kw[:, :_IDX_DIM].astype(_BF)
    z_o[...] = _dot(xb, wz[...])
    qt_o[...] = (_dot_nt(wq_t[...], xb) * (_ATTN_HEAD_DIM ** -0.5)).astype(_BF)
    qit_o[...] = _dot_nt(wqi_t[...], xb).astype(_BF)
    wit_o[...] = _dot_nt(wkw_t[...], xb)
    vt = _dot_nt(wv_t[...], xb)
    ones = jnp.ones((_ATTN_HEAD_DIM, xb.shape[0]), _BF)
    for g in range(_ATTN_KV_HEADS):
        cols = slice(g * _ATTN_HEAD_DIM, (g + 1) * _ATTN_HEAD_DIM)
        kg_o[g] = k[:, cols].astype(_BF)
        vt_o[g, 0:_ATTN_HEAD_DIM, :] = vt[cols, :].astype(_BF)
        vt_o[g, _ATTN_HEAD_DIM:2 * _ATTN_HEAD_DIM, :] = ones


def _odd_proj_prompt(x, ws, wts, bsz, seq, tkb):
    t, d = x.shape
    tm = tkb
    nb = seq // tm
    _, wk, wv, _, wkw, wz = ws
    kvw = wk.shape[1]
    aw, iw = wts[0].shape[0], wts[1].shape[0]
    row = lambda n: pl.BlockSpec((tm, n), lambda i: (i, 0))
    col = lambda n: pl.BlockSpec((n, tm), lambda i: (0, i))
    outs = [
        (jax.ShapeDtypeStruct((t, kvw), _F32), row(kvw)),
        (jax.ShapeDtypeStruct((t, kvw), _F32), row(kvw)),
        (jax.ShapeDtypeStruct((t, _IDX_DIM), _F32), row(_IDX_DIM)),
        (jax.ShapeDtypeStruct((t, wz.shape[1]), _F32), row(wz.shape[1])),
        (jax.ShapeDtypeStruct((aw, t), _BF), col(aw)),
        (jax.ShapeDtypeStruct((iw, t), _BF), col(iw)),
        (jax.ShapeDtypeStruct((_LANES, t), _F32), col(_LANES)),
        (jax.ShapeDtypeStruct((t, _IDX_DIM), _BF), row(_IDX_DIM)),
        (jax.ShapeDtypeStruct((bsz, _ATTN_KV_HEADS, seq, _ATTN_HEAD_DIM), _BF),
         pl.BlockSpec((None, _ATTN_KV_HEADS, tm, _ATTN_HEAD_DIM), lambda i: (i // nb, 0, i % nb, 0))),
        (jax.ShapeDtypeStruct((bsz, nb, _ATTN_KV_HEADS, 2 * _ATTN_HEAD_DIM, tm), _BF),
         pl.BlockSpec((None, None, _ATTN_KV_HEADS, 2 * _ATTN_HEAD_DIM, tm),
                      lambda i: (i // nb, i % nb, 0, 0, 0))),
    ]
    wins = (wk, wv, wkw, wz, *wts)
    return pl.pallas_call(
        _odd_proj_prompt_kernel,
        grid=(t // tm,),
        in_specs=[row(d)] + [_const_spec(w.shape) for w in wins],
        out_specs=[o[1] for o in outs],
        out_shape=[o[0] for o in outs],
        compiler_params=_cparams("arbitrary"),
        name="odd_proj_prompt",
    )(x, *wins)


def _odd_proj_sample_kernel(x_ref, wq, wk, wv, wqi, wkw, wz, q_o, k_o, v_o, qi_o, ki_o, wi_o, z_o):
    xb = x_ref[...].astype(_BF)
    q_o[...] = _dot(xb, wq[...]) * (_ATTN_HEAD_DIM ** -0.5)
    k_o[...] = _dot(xb, wk[...])
    v_o[...] = _dot(xb, wv[...])
    qi_o[...] = _dot(xb, wqi[...])
    kw = _dot(xb, wkw[...])
    ki_o[...] = kw[:, :_IDX_DIM]
    wi_o[...] = kw
    z_o[...] = _dot(xb, wz[...])


def _odd_proj_sample(x, ws):
    t, d = x.shape
    wq, wk, wv, wqi, wkw, wz = ws
    widths = [wq.shape[1], wk.shape[1], wv.shape[1], wqi.shape[1], _IDX_DIM, _LANES, wz.shape[1]]
    full = lambda n: pl.BlockSpec((t, n), lambda i: (0, 0))
    return pl.pallas_call(
        _odd_proj_sample_kernel,
        grid=(1,),
        in_specs=[full(d)] + [_const_spec(w.shape) for w in ws],
        out_specs=[full(n) for n in widths],
        out_shape=[jax.ShapeDtypeStruct((t, n), _F32) for n in widths],
        compiler_params=_cparams("arbitrary"),
        name="odd_proj_sample",
    )(x, *ws)


def _outproj_kernel(*refs, n_in, gated):
    x_ref = refs[0]
    a_refs = refs[1:1 + n_in]
    pos = 1 + n_in
    z_ref = refs[pos] if gated else None
    pos += 1 if gated else 0
    w_refs = refs[pos:pos + n_in]
    g_ref, b_ref, o_ref = refs[pos + n_in:pos + n_in + 3]
    acc = _ALPHA * x_ref[...]
    for a_ref, w_ref in zip(a_refs, w_refs):
        a = a_ref[...]
        if gated:
            a = a.astype(_F32) * _silu(z_ref[...])
        acc = acc + _dot(a.astype(_BF), w_ref[...])
    o_ref[...] = _layer_norm(acc, g_ref[...], b_ref[...])


def _outproj_ln(x, a_list, w_list, g, b, z=None):
    t, d = x.shape
    tm = min(512, t)
    row = lambda n: pl.BlockSpec((tm, n), lambda i: (i, 0))
    gated = z is not None
    ins = [x, *a_list] + ([z] if gated else []) + [*w_list, g, b]
    specs = ([row(d)] + [row(a.shape[1]) for a in a_list] + ([row(z.shape[1])] if gated else [])
             + [_const_spec(w.shape) for w in w_list] + [_const_spec(g.shape), _const_spec(b.shape)])
    return pl.pallas_call(
        functools.partial(_outproj_kernel, n_in=len(a_list), gated=gated),
        grid=(t // tm,),
        in_specs=specs,
        out_specs=row(d),
        out_shape=jax.ShapeDtypeStruct((t, d), _F32),
        compiler_params=_cparams("arbitrary"),
        name="outproj_ln",
    )(*ins)


def _group_rmsnorm_gate(y, za, ng):
    y = y * _silu(za)
    gw = y.shape[1] // _SSD_GROUPS
    parts = []
    for g in range(_SSD_GROUPS):
        yg = y[:, g * gw:(g + 1) * gw]
        ms = jnp.mean(yg * yg, axis=-1, keepdims=True)
        parts.append(yg * lax.rsqrt(ms + _EPS))
    return jnp.concatenate(parts, axis=-1) * ng


def _even_mix_kernel(za_ref, xbc_ref, dt_ref, glu_ref, zb_ref,
                     cw_ref, cb_ref, dtb_ref, alog_ref, d_ref, ng_ref,
                     fw_ref, fb_ref, fg_ref, fbeta_ref,
                     ya_o, yb_o, ssm_o, sc_o, cc_o,
                     xbuf, ubuf, state, ushift, cacc, *, q, nc):
    c = pl.program_id(1)
    inner = _SSD_HEADS * _SSD_HEAD_DIM
    gn = _SSD_GROUPS * _SSD_STATE
    xhalo = 8
    uhalo = 32

    @pl.when(c == 0)
    def _():
        xbuf[0:xhalo, :] = jnp.zeros((xhalo, xbuf.shape[1]), _F32)
        ubuf[0:uhalo, :] = jnp.zeros((uhalo, ubuf.shape[1]), _F32)
        state[...] = jnp.zeros(state.shape, _F32)

    xbuf[xhalo:xhalo + q, :] = xbc_ref[...]
    acc = jnp.broadcast_to(cb_ref[...], (q, xbuf.shape[1]))
    for k in range(_SSD_CONV):
        off = xhalo - (_SSD_CONV - 1) + k
        acc = acc + cw_ref[k:k + 1, :] * xbuf[off:off + q, :]
    xc = _silu(acc)
    xs = xc[:, :inner]
    bm = [xc[:, inner + g * _SSD_STATE: inner + (g + 1) * _SSD_STATE].astype(_BF) for g in range(_SSD_GROUPS)]
    cm = [xc[:, inner + gn + g * _SSD_STATE: inner + gn + (g + 1) * _SSD_STATE].astype(_BF)
          for g in range(_SSD_GROUPS)]

    @pl.when(c == nc - 1)
    def _():
        sc_o[...] = xbuf[xhalo + q - (_SSD_CONV - 1):xhalo + q, :]

    xbuf[0:xhalo, :] = xbuf[q:q + xhalo, :]

    lane = lax.broadcasted_iota(jnp.int32, (1, _LANES), 1)
    hmask = lane < _SSD_HEADS
    dt = jnp.where(hmask, _softplus(dt_ref[...] + dtb_ref[...]), 0.0)
    a = jnp.where(hmask, -jnp.exp(alog_ref[...]), 0.0)
    ri = lax.broadcasted_iota(jnp.int32, (q, q), 0)
    ci = lax.broadcasted_iota(jnp.int32, (q, q), 1)
    trilb = ri >= ci
    a_cum = _dot_hi(trilb.astype(_F32), dt * a)
    a_cum_t = a_cum.T
    dt_t = dt.T
    a_last_t = a_cum_t[:, q - 1:q]
    w_t = dt_t * jnp.exp(a_last_t - a_cum_t)
    ea = jnp.exp(a_cum)
    cd_t = jnp.exp(a_last_t)
    lo_lane = lane < _SSD_HEAD_DIM
    lo_sub = lax.broadcasted_iota(jnp.int32, (_LANES, 1), 0) < _SSD_HEAD_DIM

    cb = [_dot_nt(cm[g], bm[g]) for g in range(_SSD_GROUPS)]
    heads_per_group = _SSD_HEADS // _SSD_GROUPS
    ys = []
    for j in range(_SSD_HEADS // 2):
        h0, h1 = 2 * j, 2 * j + 1
        g = h0 // heads_per_group
        sl = slice(j * _LANES, (j + 1) * _LANES)
        xs2 = xs[:, sl]
        x2 = (xs2 * jnp.where(lo_lane, dt[:, h0:h0 + 1], dt[:, h1:h1 + 1])).astype(_BF)
        l0 = jnp.where(trilb, jnp.exp(a_cum[:, h0:h0 + 1] - a_cum_t[h0:h0 + 1, :]), 0.0)
        l1 = jnp.where(trilb, jnp.exp(a_cum[:, h1:h1 + 1] - a_cum_t[h1:h1 + 1, :]), 0.0)
        y0 = _dot((cb[g] * l0).astype(_BF), x2)
        y1 = _dot((cb[g] * l1).astype(_BF), x2)
        sp = state[sl, :]
        yoff = _dot_nt(cm[g], sp.astype(_BF)) * jnp.where(lo_lane, ea[:, h0:h0 + 1], ea[:, h1:h1 + 1])
        w2 = jnp.where(lo_sub, w_t[h0:h0 + 1, :], w_t[h1:h1 + 1, :])
        s_chunk = _dot((xs2.T * w2).astype(_BF), bm[g])
        cd2 = jnp.where(lo_sub, cd_t[h0:h0 + 1, :], cd_t[h1:h1 + 1, :])
        state[sl, :] = sp * cd2 + s_chunk
        ys.append(jnp.where(lo_lane, y0, y1) + yoff + d_ref[:, sl] * xs2)
    y = jnp.concatenate(ys, axis=-1)
    ya_o[...] = _group_rmsnorm_gate(y, za_ref[...], ng_ref[...]).astype(_BF)

    @pl.when(c == nc - 1)
    def _():
        ssm_o[...] = state[...]

    cf = ubuf.shape[1]
    u = glu_ref[:, :cf] * _sigmoid(glu_ref[:, cf:])
    ubuf[uhalo:uhalo + q, :] = u
    back = 8 * ((_CF_KERNEL - 1) // 8)
    for r in range(1, 8):
        ushift[r - 1] = ubuf[uhalo - back - r:uhalo + q - r, :]
    for cb in range(cf // _LANES):
        cs = slice(cb * _LANES, (cb + 1) * _LANES)
        acc = jnp.broadcast_to(fb_ref[:, cs], (q, _LANES))
        for j in range(_CF_KERNEL):
            r, a = j % 8, j // 8
            k = _CF_KERNEL - 1 - j
            if r == 0:
                win = ubuf[uhalo - 8 * a:uhalo - 8 * a + q, cs]
            else:
                win = ushift[r - 1, back - 8 * a:back - 8 * a + q, cs]
            acc = acc + fw_ref[k:k + 1, cs] * win
        cacc[:, cs] = acc
    v = _silu(_layer_norm(cacc[...], fg_ref[...], fbeta_ref[...]))
    yb_o[...] = (v * _silu(zb_ref[...])).astype(_BF)

    @pl.when(c == nc - 1)
    def _():
        cc_o[...] = ubuf[uhalo + q - (_CF_KERNEL - 1):uhalo + q, :]

    ubuf[0:uhalo, :] = ubuf[q:q + uhalo, :]


def _even_mix_prompt(za, xbc, dt, glu, zb, params, bsz, seq):
    q = 128
    nc = seq // q
    inner = _SSD_HEADS * _SSD_HEAD_DIM
    conv_dim = xbc.shape[1]
    cf = zb.shape[1]
    row = lambda n: pl.BlockSpec((q, n), lambda b, c: (b * nc + c, 0))
    per_b = lambda r, n: pl.BlockSpec((None, r, n), lambda b, c: (b, 0, 0))
    t = bsz * seq
    return pl.pallas_call(
        functools.partial(_even_mix_kernel, q=q, nc=nc),
        grid=(bsz, nc),
        in_specs=[row(inner), row(conv_dim), row(_LANES), row(2 * cf), row(cf)]
                 + [_const_spec(p.shape) for p in params],
        out_specs=[row(inner), row(cf), per_b(inner, _SSD_STATE),
                   per_b(_SSD_CONV - 1, conv_dim), per_b(_CF_KERNEL - 1, cf)],
        out_shape=[jax.ShapeDtypeStruct((t, inner), _BF), jax.ShapeDtypeStruct((t, cf), _BF),
                   jax.ShapeDtypeStruct((bsz, inner, _SSD_STATE), _F32),
                   jax.ShapeDtypeStruct((bsz, _SSD_CONV - 1, conv_dim), _F32),
                   jax.ShapeDtypeStruct((bsz, _CF_KERNEL - 1, cf), _F32)],
        scratch_shapes=[pltpu.VMEM((8 + q, conv_dim), _F32), pltpu.VMEM((32 + q, cf), _F32),
                        pltpu.VMEM((inner, _SSD_STATE), _F32),
                        pltpu.VMEM((7, q + 8 * ((_CF_KERNEL - 1) // 8), cf), _F32),
                        pltpu.VMEM((q, cf), _F32)],
        compiler_params=_cparams("arbitrary", "arbitrary"),
        name="even_mix_prompt",
    )(za, xbc, dt, glu, zb, *params)


def _sample_even_pre_kernel(xbc_ref, dt_ref, glu_ref, zb_ref, sctx_ref, cctx_ref,
                            cw_ref, cb_ref, dtb_ref, alog_ref, fw_ref, fb_ref, fg_ref, fbeta_ref,
                            xs_o, bm_o, cm_o, dect_o, xdtt_o, yb_o, sctx_o, cctx_o, pad):
    db = xbc_ref.shape[0]
    inner = _SSD_HEADS * _SSD_HEAD_DIM
    gn = _SSD_GROUPS * _SSD_STATE
    xbc = xbc_ref[...]
    acc = cb_ref[...] + cw_ref[_SSD_CONV - 1:_SSD_CONV, :] * xbc
    for k in range(_SSD_CONV - 1):
        acc = acc + cw_ref[k:k + 1, :] * sctx_ref[k]
    xc = _silu(acc)
    xs = xc[:, :inner]
    xs_o[...] = xs
    bm_o[...] = xc[:, inner:inner + gn]
    cm_o[...] = xc[:, inner + gn:inner + 2 * gn]
    for k in range(_SSD_CONV - 2):
        sctx_o[k] = sctx_ref[k + 1]
    sctx_o[_SSD_CONV - 2] = xbc

    lane = lax.broadcasted_iota(jnp.int32, (1, _LANES), 1)
    hmask = lane < _SSD_HEADS
    dt = jnp.where(hmask, _softplus(dt_ref[...] + dtb_ref[...]), 0.0)
    a = jnp.where(hmask, -jnp.exp(alog_ref[...]), 0.0)
    dec = jnp.where(hmask, jnp.exp(dt * a), 0.0)
    er = lax.broadcasted_iota(jnp.int32, (_LANES, inner), 0)
    ec = lax.broadcasted_iota(jnp.int32, (_LANES, inner), 1)
    expand = jnp.where((ec // _SSD_HEAD_DIM) == er, 1.0, 0.0)
    dec_x = _dot_hi(dec, expand)
    xdt = xs * _dot_hi(dt, expand)
    for src, dst in ((dec_x, dect_o), (xdt, xdtt_o)):
        pad[...] = jnp.zeros(pad.shape, _F32)
        pad[0:db, :] = src
        for j in range(inner // _LANES):
            dst[j * _LANES:(j + 1) * _LANES, :] = pad[:, j * _LANES:(j + 1) * _LANES].T

    cf = zb_ref.shape[1]
    u = glu_ref[:, :cf] * _sigmoid(glu_ref[:, cf:])
    acc = fb_ref[...] + fw_ref[_CF_KERNEL - 1:_CF_KERNEL, :] * u
    for k in range(_CF_KERNEL - 1):
        acc = acc + fw_ref[k:k + 1, :] * cctx_ref[k]
    v = _silu(_layer_norm(acc, fg_ref[...], fbeta_ref[...]))
    yb_o[...] = (v * _silu(zb_ref[...])).astype(_BF)
    for k in range(_CF_KERNEL - 2):
        cctx_o[k] = cctx_ref[k + 1]
    cctx_o[_CF_KERNEL - 2] = u


def _sample_even_pre(xbc, dt, glu, zb, sctx_t, cctx_t, params):
    db = xbc.shape[0]
    inner = _SSD_HEADS * _SSD_HEAD_DIM
    gn = _SSD_GROUPS * _SSD_STATE
    cf = zb.shape[1]
    ins = [xbc, dt, glu, zb, sctx_t, cctx_t, *params]
    full = lambda s: pl.BlockSpec(s, lambda i: (0,) * len(s))
    out_shapes = [(db, inner), (db, gn), (db, gn), (inner, _LANES), (inner, _LANES), (db, cf),
                  sctx_t.shape, cctx_t.shape]
    out_dtypes = [_F32, _F32, _F32, _F32, _F32, _BF, _F32, _F32]
    return pl.pallas_call(
        _sample_even_pre_kernel,
        grid=(1,),
        in_specs=[full(a.shape) for a in ins],
        out_specs=[full(s) for s in out_shapes],
        out_shape=[jax.ShapeDtypeStruct(s, d) for s, d in zip(out_shapes, out_dtypes)],
        scratch_shapes=[pltpu.VMEM((_LANES, inner), _F32)],
        compiler_params=_cparams("arbitrary"),
        name="sample_even_pre",
    )(*ins)


def _sample_even_rec_kernel(st_ref, dect_ref, xdtt_ref, bm_ref, cm_ref, xs_ref, za_ref, d_ref, ng_ref,
                            st_o, ya_o):
    b = pl.program_id(0)
    inner = st_ref.shape[0]
    half = inner // _SSD_GROUPS
    sel = jnp.where(lax.broadcasted_iota(jnp.int32, (_LANES, _LANES), 0) == b, 1.0, 0.0)
    dec = _dot_hi(dect_ref[...], sel)
    xd = _dot_hi(xdtt_ref[...], sel)
    grp0 = lax.broadcasted_iota(jnp.int32, (inner, 1), 0) < half
    brow = jnp.where(grp0, bm_ref[:, :_SSD_STATE], bm_ref[:, _SSD_STATE:])
    s_new = st_ref[...] * dec + xd * brow
    st_o[...] = s_new
    r8 = lax.broadcasted_iota(jnp.int32, (8, 1), 0)
    c8 = jnp.where(r8 == 0, cm_ref[:, :_SSD_STATE], jnp.where(r8 == 1, cm_ref[:, _SSD_STATE:], 0.0))
    yr = _dot_nt_hi(c8, s_new)
    lane = lax.broadcasted_iota(jnp.int32, (1, inner), 1)
    xs = xs_ref[...]
    y = jnp.where(lane < half, yr[0:1, :], yr[1:2, :]) + d_ref[...] * xs
    ya_o[...] = _group_rmsnorm_gate(y, za_ref[...], ng_ref[...]).astype(_BF)


def _sample_even_rec(state, dect, xdtt, bm, cm, xs, za, d_x, ng):
    db, inner, n = state.shape
    per_b = lambda s: pl.BlockSpec((None,) + s, lambda b: (b, 0, 0))
    gn = bm.shape[1]
    r3 = lambda a: a.reshape(db, 1, a.shape[1])
    return pl.pallas_call(
        _sample_even_rec_kernel,
        grid=(db,),
        in_specs=[per_b((inner, n)), _const_spec(dect.shape), _const_spec(xdtt.shape),
                  per_b((1, gn)), per_b((1, gn)), per_b((1, inner)), per_b((1, inner)),
                  _const_spec(d_x.shape), _const_spec(ng.shape)],
        out_specs=[per_b((inner, n)), per_b((1, inner))],
        out_shape=[jax.ShapeDtypeStruct((db, inner, n), _F32), jax.ShapeDtypeStruct((db, 1, inner), _BF)],
        compiler_params=_cparams("arbitrary"),
        name="sample_even_rec",
    )(state, dect, xdtt, r3(bm), r3(cm), r3(xs), r3(za), d_x, ng)


def _bisect_threshold(count_ge, rmin, rmax, n_adm, c_ge0, c_gt0, k):
    nonneg = c_ge0 >= k
    lo0 = jnp.where(n_adm <= k, rmin, jnp.where(nonneg, 0.0, rmin))
    c0 = jnp.where(n_adm <= k, n_adm, jnp.where(nonneg, c_ge0, n_adm))
    hi0 = jnp.where(c_gt0 >= k, rmax + (rmax - lo0), 0.0)

    def midpoint(lo, hi):
        return lo + (hi - lo) * 0.5

    def open_rows(lo, hi, c_lo, mid):
        return jnp.where(c_lo > k, jnp.where(mid > lo, jnp.where(mid < hi, 1.0, 0.0), 0.0), 0.0)

    def cond(st):
        it, _, _, _, _, todo = st
        return jnp.logical_and(it < _MAX_BISECT, jnp.max(todo) > 0.5)

    def body(st):
        it, lo, hi, c_lo, mid, _ = st
        c = count_ge(mid)
        ge = c >= k
        lo_n = jnp.where(ge, mid, lo)
        hi_n = jnp.where(ge, hi, mid)
        c_n = jnp.where(ge, c, c_lo)
        mid_n = midpoint(lo_n, hi_n)
        return it + 1, lo_n, hi_n, c_n, mid_n, open_rows(lo_n, hi_n, c_n, mid_n)

    mid0 = midpoint(lo0, hi0)
    st = (jnp.int32(0), lo0, hi0, c0, mid0, open_rows(lo0, hi0, c0, mid0))
    _, lo, _, c_lo, _, _ = lax.while_loop(cond, body, st)
    return lo, c_lo


def _fold8(x, op):
    n, c = x.shape
    if n % 64 == 0 and n > 64:
        x = op(x.reshape(n // 64, 64, c), axis=0)
    return op(x.reshape(x.shape[0] // 8, 8, c), axis=0)


def _attn_prompt_kernel(qit_ref, wit_ref, qt_ref, z_ref, ki_ref, kg_ref, vt_ref, o_ref,
                        sc_ref, qis_ref, qg_ref, acc_ref, m_ref, al_ref, s_ref, p_ref, ob_ref,
                        *, tq, tkb, topk):
    i = pl.program_id(1)
    nkb = (i * tq + tq + tkb - 1) // tkb
    qpos = i * tq + lax.broadcasted_iota(jnp.int32, (1, tq), 1)
    kf = jnp.float32(topk)
    inf = jnp.float32(jnp.inf)
    rep = _ATTN_HEADS // _ATTN_KV_HEADS
    sk = 2 * _LANES

    qit = qit_ref[...]
    for h in range(_IDX_HEADS):
        qis_ref[:, h * tq:(h + 1) * tq] = qit[h * _IDX_DIM:(h + 1) * _IDX_DIM, :]
    w = wit_ref[...] * (_IDX_HEADS ** -0.5 * _IDX_DIM ** -0.5)
    wrs = [w[_IDX_DIM + h:_IDX_DIM + h + 1, :] for h in range(_IDX_HEADS)]

    def score_body(kb, carry):
        rmin, rmax, cge0, cgt0 = carry
        for jb in range(tkb // sk):
            start = pl.multiple_of(kb * tkb + jb * sk, sk)
            s8 = _dot(ki_ref[pl.ds(start, sk), :], qis_ref[...])
            acc = jnp.zeros((sk, tq), _F32)
            for h in range(_IDX_HEADS):
                acc = acc + jnp.maximum(s8[:, h * tq:(h + 1) * tq], 0.0) * wrs[h]
            kpos = start + lax.broadcasted_iota(jnp.int32, (sk, 1), 0)
            adm = kpos <= qpos
            sc = jnp.where(adm, acc, -inf)
            sc_ref[kb, jb * sk:(jb + 1) * sk, :] = sc
            rmax = jnp.maximum(rmax, _fold8(sc, jnp.max))
            rmin = jnp.minimum(rmin, _fold8(jnp.where(adm, acc, inf), jnp.min))
            cge0 = cge0 + _fold8(jnp.where(sc >= 0.0, 1.0, 0.0), jnp.sum)
            cgt0 = cgt0 + _fold8(jnp.where(sc > 0.0, 1.0, 0.0), jnp.sum)
        return rmin, rmax, cge0, cgt0

    zeros8 = jnp.zeros((8, tq), _F32)
    rmin, rmax, cge0, cgt0 = lax.fori_loop(
        0, nkb, score_body,
        (jnp.full((8, tq), inf, _F32), jnp.full((8, tq), -inf, _F32), zeros8, zeros8))
    rmin = jnp.min(rmin, axis=0, keepdims=True)
    rmax = jnp.max(rmax, axis=0, keepdims=True)
    cge0 = jnp.sum(cge0, axis=0, keepdims=True)
    cgt0 = jnp.sum(cgt0, axis=0, keepdims=True)

    def count_ge(x):
        def cbody(kb, cnt):
            return cnt + _fold8(jnp.where(sc_ref[kb] >= x, 1.0, 0.0), jnp.sum)
        cnt = lax.fori_loop(0, nkb, cbody, zeros8)
        return jnp.sum(cnt, axis=0, keepdims=True)

    n_adm = (qpos + 1).astype(_F32)
    lo, c_lo = _bisect_threshold(count_ge, rmin, rmax, n_adm, cge0, cgt0, kf)

    @pl.when(jnp.max(c_lo) > kf)
    def _():
        def gbody(kb, cnt):
            return cnt + _fold8(jnp.where(sc_ref[kb] > lo, 1.0, 0.0), jnp.sum)
        need = kf - jnp.sum(lax.fori_loop(0, nkb, gbody, zeros8), axis=0, keepdims=True)
        tri = jnp.where(lax.broadcasted_iota(jnp.int32, (tkb, tkb), 0)
                        >= lax.broadcasted_iota(jnp.int32, (tkb, tkb), 1), 1.0, 0.0).astype(_BF)

        def tbody(kb, carry):
            s = sc_ref[kb]
            eq = s == lo
            pre = _dot(tri, jnp.where(eq, 1.0, 0.0).astype(_BF))
            drop = jnp.logical_and(eq, (carry + pre) > need)
            sc_ref[kb] = jnp.where(drop, -inf, s)
            return carry + pre[tkb - 1:tkb, :]
        lax.fori_loop(0, nkb, tbody, jnp.zeros((1, tq), _F32))

    qt = qt_ref[...]
    for h in range(_ATTN_HEADS):
        g, r = divmod(h, rep)
        qg_ref[g, :, r * tq:(r + 1) * tq] = qt[h * _ATTN_HEAD_DIM:(h + 1) * _ATTN_HEAD_DIM, :]
    m_ref[...] = jnp.full(m_ref.shape, _NEG, _F32)
    acc_ref[...] = jnp.zeros(acc_ref.shape, _F32)

    def bias_body(kb, carry):
        sc_ref[kb] = jnp.where(sc_ref[kb] >= lo, 0.0, _NEG)
        return carry
    lax.fori_loop(0, nkb, bias_body, 0)

    def attn_body(kb, carry):
        base = pl.multiple_of(kb * tkb, tkb)
        for g in range(_ATTN_KV_HEADS):
            mxs = [None] * rep
            for jb in range(tkb // sk):
                rows = slice(jb * sk, (jb + 1) * sk)
                s4 = _dot(kg_ref[g, pl.ds(base + jb * sk, sk), :], qg_ref[g])
                bias = sc_ref[kb, rows, :]
                for r in range(rep):
                    cols = slice(r * tq, (r + 1) * tq)
                    s = s4[:, cols] + bias
                    s_ref[g, rows, cols] = s
                    part = _fold8(s, jnp.max)
                    mxs[r] = part if mxs[r] is None else jnp.maximum(mxs[r], part)
            mx = jnp.concatenate([jnp.max(m8, axis=0, keepdims=True) for m8 in mxs], axis=1)
            m_old = m_ref[g]
            m_new = jnp.maximum(m_old, mx)
            al_ref[g] = jnp.exp(m_old - m_new)
            m_ref[g] = m_new
        for g in range(_ATTN_KV_HEADS):
            m_new = m_ref[g]
            for jb in range(tkb // _LANES):
                rows = slice(jb * _LANES, (jb + 1) * _LANES)
                p_ref[g, rows, :] = jnp.exp(s_ref[g, rows, :] - m_new).astype(_BF)
        for g in range(_ATTN_KV_HEADS):
            acc_ref[g] = acc_ref[g] * al_ref[g] + _dot(vt_ref[kb, g], p_ref[g])
        return carry

    lax.fori_loop(0, nkb, attn_body, 0)
    for h in range(_ATTN_HEADS):
        g, r = divmod(h, rep)
        acc = acc_ref[g, :, r * tq:(r + 1) * tq].T
        o_h = acc / pltpu.roll(acc, _ATTN_HEAD_DIM, 1)
        ob_ref[:, h * _ATTN_HEAD_DIM:(h + 1) * _ATTN_HEAD_DIM] = o_h[:, :_ATTN_HEAD_DIM]
    o_ref[...] = (ob_ref[...] * _silu(z_ref[...])).astype(_BF)


def _attn_prompt(qit, wit, qt, z, ki, kg, vt, bsz, seq, tkb):
    tq = 256
    nq = seq // tq
    nkbt = seq // tkb
    t = bsz * seq
    topk = min(_TOPK_MAX, seq // 4)
    width = z.shape[1]
    rep = _ATTN_HEADS // _ATTN_KV_HEADS
    row = lambda n: pl.BlockSpec((tq, n), lambda b, i: (b * nq + i, 0))
    col = lambda n: pl.BlockSpec((n, tq), lambda b, i: (0, b * nq + i))
    once = dict(pipeline_mode=pl.Buffered(1))
    return pl.pallas_call(
        functools.partial(_attn_prompt_kernel, tq=tq, tkb=tkb, topk=topk),
        grid=(bsz, nq),
        in_specs=[col(qit.shape[0]), col(_LANES), col(qt.shape[0]), row(width),
                  pl.BlockSpec((seq, _IDX_DIM), lambda b, i: (b, 0), **once),
                  pl.BlockSpec((None,) + kg.shape[1:], lambda b, i: (b, 0, 0, 0), **once),
                  pl.BlockSpec((None,) + vt.shape[1:], lambda b, i: (b, 0, 0, 0, 0), **once)],
        out_specs=row(width),
        out_shape=jax.ShapeDtypeStruct((t, width), _BF),
        scratch_shapes=[pltpu.VMEM((nkbt, tkb, tq), _F32),
                        pltpu.VMEM((_IDX_DIM, _IDX_HEADS * tq), _BF),
                        pltpu.VMEM((_ATTN_KV_HEADS, _ATTN_HEAD_DIM, rep * tq), _BF),
                        pltpu.VMEM((_ATTN_KV_HEADS, 2 * _ATTN_HEAD_DIM, rep * tq), _F32),
                        pltpu.VMEM((_ATTN_KV_HEADS, 1, rep * tq), _F32),
                        pltpu.VMEM((_ATTN_KV_HEADS, 1, rep * tq), _F32),
                        pltpu.VMEM((_ATTN_KV_HEADS, tkb, rep * tq), _F32),
                        pltpu.VMEM((_ATTN_KV_HEADS, tkb, rep * tq), _BF),
                        pltpu.VMEM((tq, width), _F32)],
        compiler_params=_cparams("arbitrary", "arbitrary"),
        name="attn_prompt",
    )(qit, wit, qt, z, ki, kg, vt)


def _attn_sample_kernel(pt_ref, qi_ref, w_ref, kin_ref, q_ref, kn_ref, vn_ref,
                        ckit_ref, ckt_ref, cvt_ref, o_ref,
                        kibuf, kbuf, vbuf, sc_ref, sn_ref, lg_ref, sem, *, n_pages, topk):
    b = pl.program_id(0)
    past = n_pages * _PAGE
    kf = jnp.float32(topk)
    inf = jnp.float32(jnp.inf)
    caches = ((ckit_ref, kibuf), (ckt_ref, kbuf), (cvt_ref, vbuf))
    unroll = min(8, n_pages)

    def page_copy(which, page):
        cache_ref, buf = caches[which]
        return pltpu.make_async_copy(cache_ref.at[pt_ref[b, page]], buf.at[page], sem.at[which])

    def start_all(which):
        lax.fori_loop(0, n_pages, lambda p, c: (page_copy(which, p).start(), c)[1], 0)

    def wait_all(which):
        lax.fori_loop(0, n_pages, lambda p, c: (page_copy(which, p).wait(), c)[1], 0)

    for which in range(3):
        start_all(which)

    def total(x):
        return jnp.sum(jnp.sum(x, axis=1, keepdims=True), axis=0, keepdims=True)

    qi8 = qi_ref[...].astype(_BF)
    w8 = w_ref[...] * (_IDX_HEADS ** -0.5 * _IDX_DIM ** -0.5)
    wait_all(0)

    def score_body(p, c):
        s8 = _dot(qi8, kibuf[p].astype(_BF))
        sc_ref[pl.ds(p, 1), :] = jnp.sum(jnp.maximum(s8, 0.0) * w8, axis=0, keepdims=True)
        return c
    lax.fori_loop(0, n_pages, score_body, 0, unroll=unroll)
    kin = kin_ref[...].astype(_BF).astype(_F32)
    sn8 = jnp.sum(qi8.astype(_F32) * kin, axis=1, keepdims=True)
    sn_ref[...] = jnp.sum(jnp.maximum(sn8, 0.0) * w8, axis=0, keepdims=True)
    snew = sn_ref[...]

    if past + 1 > topk:
        sc = sc_ref[...]
        rmax = jnp.maximum(jnp.max(jnp.max(sc, axis=1, keepdims=True), axis=0, keepdims=True), snew)
        rmin = jnp.minimum(jnp.min(jnp.min(sc, axis=1, keepdims=True), axis=0, keepdims=True), snew)
        one = lambda cnd: jnp.where(cnd, 1.0, 0.0)

        def count_ge(x):
            return total(one(sc_ref[...] >= x)) + one(snew >= x)

        lo, c_lo = _bisect_threshold(count_ge, rmin, rmax, jnp.full((1, 1), past + 1.0, _F32),
                                     count_ge(jnp.zeros((1, 1), _F32)),
                                     total(one(sc > 0.0)) + one(snew > 0.0), kf)

        @pl.when(jnp.max(c_lo) > kf)
        def _():
            need = kf - (total(one(sc_ref[...] > lo)) + one(snew > lo))
            tri = jnp.where(lax.broadcasted_iota(jnp.int32, (_LANES, _LANES), 0)
                            <= lax.broadcasted_iota(jnp.int32, (_LANES, _LANES), 1), 1.0, 0.0).astype(_BF)

            def tbody(p, carry):
                s = sc_ref[pl.ds(p, 1), :]
                eq = s == lo
                eq8 = jnp.broadcast_to(one(eq), (8, _LANES)).astype(_BF)
                pre = _dot(eq8, tri)[0:1, :]
                drop = jnp.logical_and(eq, (carry + pre) > need)
                sc_ref[pl.ds(p, 1), :] = jnp.where(drop, -inf, s)
                return carry + pre[:, _LANES - 1:_LANES]
            carry = lax.fori_loop(0, n_pages, tbody, jnp.zeros((1, 1), _F32))
            drop_new = jnp.logical_and(snew == lo, (carry + 1.0) > need)
            sn_ref[...] = jnp.where(drop_new, -inf, snew)
    else:
        lo = jnp.full((1, 1), -inf, _F32)

    kvw = _ATTN_KV_HEADS * _ATTN_HEAD_DIM
    rep = _ATTN_HEADS // _ATTN_KV_HEADS
    rr = lax.broadcasted_iota(jnp.int32, (_ATTN_HEAD_DIM, kvw), 0)
    rc = lax.broadcasted_iota(jnp.int32, (_ATTN_HEAD_DIM, kvw), 1)
    spread = jnp.where((rc % _ATTN_HEAD_DIM) == rr, 1.0, 0.0)
    hr = lax.broadcasted_iota(jnp.int32, (_ATTN_HEADS, kvw), 0)
    hc = lax.broadcasted_iota(jnp.int32, (_ATTN_HEADS, kvw), 1)
    own = (hc // _ATTN_HEAD_DIM) == (hr // rep)
    qbd = jnp.where(own, _dot_hi(q_ref[...], spread), 0.0).astype(_BF)
    kn = kn_ref[...].astype(_BF).astype(_F32)
    lnew = (jnp.sum(qbd.astype(_F32) * kn, axis=1, keepdims=True)
            + jnp.where(sn_ref[...] >= lo, 0.0, _NEG))
    wait_all(1)

    def logit_body(p, mx):
        lg = _dot(qbd, kbuf[p].astype(_BF)) + jnp.where(sc_ref[pl.ds(p, 1), :] >= lo, 0.0, _NEG)
        lg_ref[p] = lg
        return jnp.maximum(mx, lg)
    mx = lax.fori_loop(0, n_pages, logit_body, jnp.full((_ATTN_HEADS, _LANES), _NEG, _F32), unroll=unroll)
    m = jnp.maximum(jnp.max(mx, axis=1, keepdims=True), lnew)
    pn = jnp.exp(lnew - m)
    vn = vn_ref[...].astype(_BF).astype(_F32)
    wait_all(2)

    def pv_body(p, carry):
        psum, out = carry
        pr = jnp.exp(lg_ref[p] - m)
        return psum + pr, out + _dot_nt(pr.astype(_BF), vbuf[p].astype(_BF))
    psum, out = lax.fori_loop(0, n_pages, pv_body,
                              (jnp.zeros((_ATTN_HEADS, _LANES), _F32), pn.astype(_BF).astype(_F32) * vn),
                              unroll=unroll)
    den = jnp.sum(psum, axis=1, keepdims=True) + pn
    out = jnp.where(own, out / den, 0.0)
    gather = jnp.where((lax.broadcasted_iota(jnp.int32, (kvw, _ATTN_HEAD_DIM), 0) % _ATTN_HEAD_DIM)
                       == lax.broadcasted_iota(jnp.int32, (kvw, _ATTN_HEAD_DIM), 1), 1.0, 0.0)
    o_ref[...] = _dot_hi(out, gather)


def _attn_sample(page_table, qi, wi, ki_new, q, k_new, v_new, cache_ki, cache_k, cache_v):
    db, n_pages = page_table.shape
    past = n_pages * _PAGE
    topk = min(_TOPK_MAX, (past + 1) // 4)
    kvw = _ATTN_KV_HEADS * _ATTN_HEAD_DIM
    n_pool = cache_k.shape[0]
    per_b = lambda s: pl.BlockSpec((None,) + s, lambda b, pt: (b,) + (0,) * len(s))
    anyspec = pl.BlockSpec(memory_space=pl.ANY)
    kv_t = lambda c: jnp.transpose(c, (0, 2, 3, 1)).reshape(n_pool, kvw, _PAGE)
    ins = [qi.reshape(db, _IDX_HEADS, _IDX_DIM),
           wi[:, _IDX_DIM:_IDX_DIM + _IDX_HEADS].reshape(db, _IDX_HEADS, 1),
           ki_new.reshape(db, 1, _IDX_DIM),
           q.reshape(db, _ATTN_HEADS, _ATTN_HEAD_DIM),
           k_new.reshape(db, 1, kvw), v_new.reshape(db, 1, kvw),
           jnp.transpose(cache_ki, (0, 2, 1)), kv_t(cache_k), kv_t(cache_v)]
    grid_spec = pltpu.PrefetchScalarGridSpec(
        num_scalar_prefetch=1,
        grid=(db,),
        in_specs=[per_b((_IDX_HEADS, _IDX_DIM)), per_b((_IDX_HEADS, 1)), per_b((1, _IDX_DIM)),
                  per_b((_ATTN_HEADS, _ATTN_HEAD_DIM)), per_b((1, kvw)), per_b((1, kvw)),
                  anyspec, anyspec, anyspec],
        out_specs=per_b((_ATTN_HEADS, _ATTN_HEAD_DIM)),
        scratch_shapes=[pltpu.VMEM((n_pages, _IDX_DIM, _PAGE), _F32),
                        pltpu.VMEM((n_pages, kvw, _PAGE), _F32),
                        pltpu.VMEM((n_pages, kvw, _PAGE), _F32),
                        pltpu.VMEM((n_pages, _PAGE), _F32), pltpu.VMEM((1, 1), _F32),
                        pltpu.VMEM((n_pages, _ATTN_HEADS, _PAGE), _F32),
                        pltpu.SemaphoreType.DMA((3,))],
    )
    o = pl.pallas_call(
        functools.partial(_attn_sample_kernel, n_pages=n_pages, topk=topk),
        grid_spec=grid_spec,
        out_shape=jax.ShapeDtypeStruct((db, _ATTN_HEADS, _ATTN_HEAD_DIM), _F32),
        compiler_params=_cparams("arbitrary"),
        name="attn_sample",
    )(page_table, *ins)
    return o.reshape(db, _ATTN_HEADS * _ATTN_HEAD_DIM)


def _pad_lanes(a, n=_LANES):
    return jnp.pad(a, [(0, 0)] * (a.ndim - 1) + [(0, n - a.shape[-1])])


def _even_weights(w_in, conv_w, conv_b, dt_bias, a_log, d_skip, norm_g, cf_w, cf_b, cf_g, cf_beta, w_out):
    inner = _SSD_HEADS * _SSD_HEAD_DIM
    conv_dim = conv_w.shape[1]
    cf = cf_w.shape[1]
    o1, o2, o3, o4 = inner, inner + conv_dim, inner + conv_dim + _SSD_HEADS, inner + conv_dim + _SSD_HEADS + 2 * cf
    wb = w_in.astype(_BF)
    proj = (wb[:, :o1], wb[:, o1:o2], _pad_lanes(wb[:, o2:o3]), wb[:, o3:o4], wb[:, o4:])
    row = lambda v: v.reshape(1, -1)
    ssd = (conv_w, row(conv_b), _pad_lanes(row(dt_bias)), _pad_lanes(row(a_log)))
    d_x = row(jnp.repeat(d_skip, _SSD_HEAD_DIM))
    cfp = (cf_w, row(cf_b), row(cf_g), row(cf_beta))
    wo = w_out.astype(_BF)
    return proj, ssd, d_x, row(norm_g), cfp, (wo[:inner], wo[inner:])


def _even_layer_prompt(x, wts, ln_g, ln_b, bsz, seq):
    proj, ssd, d_x, ng, cfp, wo = wts
    za, xbc, dt, glu, zb = _even_proj(x, proj)
    ya, yb, ssm, sc, cc = _even_mix_prompt(za, xbc, dt, glu, zb, (*ssd, d_x, ng, *cfp), bsz, seq)
    x_new = _outproj_ln(x, [ya, yb], wo, ln_g, ln_b)
    return x_new, ssm.reshape(bsz, _SSD_HEADS, _SSD_HEAD_DIM, _SSD_STATE), sc, cc


def _even_layer_sample(x, st_ssm, st_sconv, st_cconv, wts, ln_g, ln_b):
    proj, ssd, d_x, ng, cfp, wo = wts
    db = x.shape[0]
    inner = _SSD_HEADS * _SSD_HEAD_DIM
    za, xbc, dt, glu, zb = _even_proj(x, proj)
    xs, bm, cm, dect, xdtt, yb, sctx_n, cctx_n = _sample_even_pre(
        xbc, dt, glu, zb, jnp.swapaxes(st_sconv, 0, 1), jnp.swapaxes(st_cconv, 0, 1), (*ssd, *cfp))
    st_new, ya = _sample_even_rec(st_ssm.reshape(db, inner, _SSD_STATE), dect, xdtt, bm, cm, xs, za, d_x, ng)
    x_new = _outproj_ln(x, [ya.reshape(db, inner), yb], wo, ln_g, ln_b)
    return (x_new, st_new.reshape(st_ssm.shape), jnp.swapaxes(sctx_n, 0, 1), jnp.swapaxes(cctx_n, 0, 1))


def _odd_weights(w_in, w_out):
    aw = _ATTN_HEADS * _ATTN_HEAD_DIM
    kvw = _ATTN_KV_HEADS * _ATTN_HEAD_DIM
    iw = _IDX_HEADS * _IDX_DIM
    o1, o2, o3, o4 = aw, aw + kvw, aw + 2 * kvw, aw + 2 * kvw + iw
    o5 = o4 + _IDX_DIM + _IDX_HEADS
    wb = w_in.astype(_BF)
    ws = (wb[:, :o1], wb[:, o1:o2], wb[:, o2:o3], wb[:, o3:o4], _pad_lanes(wb[:, o4:o5]), wb[:, o5:])
    wts = (ws[0].T, ws[3].T, ws[4].T, ws[2].T)
    return ws, wts, w_out.astype(_BF)


def _odd_layer_prompt(x, wts, ln_g, ln_b, bsz, seq):
    ws, wtr, wo = wts
    tkb = min(512, seq)
    k, v, ki, z, qt, qit, wit, kib, kg, vt = _odd_proj_prompt(x, ws, wtr, bsz, seq, tkb)
    o = _attn_prompt(qit, wit, qt, z, kib, kg, vt, bsz, seq, tkb)
    x_new = _outproj_ln(x, [o], [wo], ln_g, ln_b)
    return (x_new, k.reshape(bsz, seq, _ATTN_KV_HEADS, _ATTN_HEAD_DIM),
            v.reshape(bsz, seq, _ATTN_KV_HEADS, _ATTN_HEAD_DIM), ki.reshape(bsz, seq, _IDX_DIM))


def _odd_layer_sample(x, cache_k, cache_v, cache_ki, page_table, wts, ln_g, ln_b):
    ws, _, wo = wts
    db = x.shape[0]
    q, k, v, qi, ki, wi, z = _odd_proj_sample(x, ws)
    o = _attn_sample(page_table, qi, wi, ki, q, k, v, cache_ki, cache_k, cache_v)
    x_new = _outproj_ln(x, [o], [wo], ln_g, ln_b, z=z)
    return (x_new, k.reshape(db, 1, _ATTN_KV_HEADS, _ATTN_HEAD_DIM),
            v.reshape(db, 1, _ATTN_KV_HEADS, _ATTN_HEAD_DIM), ki.reshape(db, 1, _IDX_DIM))


def kernel(x_prompt, x_sample, state_ssm_l0, state_ssdconv_l0, state_cfconv_l0, cache_k_l1, cache_v_l1, cache_kidx_l1, state_ssm_l2, state_ssdconv_l2, state_cfconv_l2, cache_k_l3, cache_v_l3, cache_kidx_l3, page_table, w_in_even, ssd_conv_w, ssd_conv_b, ssd_dt_bias, ssd_a_log, ssd_d, ssd_norm_g, cf_dw_w, cf_dw_b, cf_ln_g, cf_ln_b, w_out_even, w_in_odd, w_out_odd, ln_g, ln_b):
    bsz, seq, d = x_prompt.shape
    db = x_sample.shape[0]
    ssm_states = (state_ssm_l0, state_ssm_l2)
    sconv_states = (state_ssdconv_l0, state_ssdconv_l2)
    cconv_states = (state_cfconv_l0, state_cfconv_l2)
    k_caches = (cache_k_l1, cache_k_l3)
    v_caches = (cache_v_l1, cache_v_l3)
    ki_caches = (cache_kidx_l1, cache_kidx_l3)
    yp = x_prompt.reshape(bsz * seq, d)
    ys = x_sample.reshape(db, d)
    new_state = []
    for layer in range(_DEPTH):
        j = layer // 2
        g, b = ln_g[layer].reshape(1, d), ln_b[layer].reshape(1, d)
        if layer % 2 == 0:
            wts = _even_weights(w_in_even[j], ssd_conv_w[j], ssd_conv_b[j], ssd_dt_bias[j], ssd_a_log[j],
                                ssd_d[j], ssd_norm_g[j], cf_dw_w[j], cf_dw_b[j], cf_ln_g[j], cf_ln_b[j],
                                w_out_even[j])
            yp, ssm_p, sc_p, cc_p = _even_layer_prompt(yp, wts, g, b, bsz, seq)
            ys, ssm_s, sc_s, cc_s = _even_layer_sample(ys, ssm_states[j], sconv_states[j], cconv_states[j],
                                                       wts, g, b)
            new_state += [ssm_p, ssm_s, sc_p, sc_s, cc_p, cc_s]
        else:
            wts = _odd_weights(w_in_odd[j], w_out_odd[j])
            yp, k_p, v_p, ki_p = _odd_layer_prompt(yp, wts, g, b, bsz, seq)
            ys, k_s, v_s, ki_s = _odd_layer_sample(ys, k_caches[j], v_caches[j], ki_caches[j], page_table,
                                                   wts, g, b)
            new_state += [k_p, k_s, v_p, v_s, ki_p, ki_s]
    return (yp.reshape(bsz, seq, d), ys.reshape(db, 1, d), *new_state)
```

```python
import functools

import jax
import jax.numpy as jnp
from jax import lax
from jax.experimental import pallas as pl
from jax.experimental.pallas import tpu as pltpu

_BF = jnp.bfloat16
_F32 = jnp.float32
_HI = lax.Precision.HIGHEST

_SSD_HEADS = 16
_SSD_HEAD_DIM = 64
_SSD_GROUPS = 2
_SSD_STATE = 128
_SSD_CONV = 4
_CF_KERNEL = 31
_ATTN_HEADS = 16
_ATTN_KV_HEADS = 4
_ATTN_HEAD_DIM = 64
_IDX_HEADS = 8
_IDX_DIM = 64
_TOPK_MAX = 256
_PAGE = 128
_DEPTH = 4
_ALPHA = (2 * _DEPTH) ** 0.25
_EPS = 1e-5

_LANES = 128
_VMEM_LIMIT = 56 * 1024 * 1024
_NEG = -1e30
_MAX_BISECT = 200


def _dot(a, b):
    return jnp.dot(a, b, preferred_element_type=_F32)


def _dot_nt(a, b):
    return lax.dot_general(a, b, (((1,), (1,)), ((), ())), preferred_element_type=_F32)


def _dot_hi(a, b):
    return jnp.dot(a, b, precision=_HI, preferred_element_type=_F32)


def _dot_nt_hi(a, b):
    return lax.dot_general(a, b, (((1,), (1,)), ((), ())), precision=_HI,
                           preferred_element_type=_F32)


def _sigmoid(x):
    return 1.0 / (1.0 + jnp.exp(-x))


def _silu(x):
    return x * _sigmoid(x)


def _softplus(x):
    return jnp.maximum(x, 0.0) + jnp.log1p(jnp.exp(-jnp.abs(x)))


def _layer_norm(x, g, b):
    mu = jnp.mean(x, axis=-1, keepdims=True)
    xc = x - mu
    var = jnp.mean(xc * xc, axis=-1, keepdims=True)
    return xc * lax.rsqrt(var + _EPS) * g + b


def _cparams(*sem):
    return pltpu.CompilerParams(dimension_semantics=sem, vmem_limit_bytes=_VMEM_LIMIT)


def _const_spec(shape):
    return pl.BlockSpec(shape, lambda *_: (0,) * len(shape), pipeline_mode=pl.Buffered(1))


def _even_proj_kernel(x_ref, wza, wxbc, wdt, wglu, wzb, za_o, xbc_o, dt_o, glu_o, zb_o):
    xb = x_ref[...].astype(_BF)
    za_o[...] = _dot(xb, wza[...])
    xbc_o[...] = _dot(xb, wxbc[...])
    dt_o[...] = _dot(xb, wdt[...])
    glu_o[...] = _dot(xb, wglu[...])
    zb_o[...] = _dot(xb, wzb[...])


def _even_proj(x, ws):
    t, d = x.shape
    tm = min(256, t)
    widths = [w.shape[1] for w in ws]
    row = lambda n: pl.BlockSpec((tm, n), lambda i: (i, 0))
    return pl.pallas_call(
        _even_proj_kernel,
        grid=(t // tm,),
        in_specs=[row(d)] + [_const_spec(w.shape) for w in ws],
        out_specs=[row(n) for n in widths],
        out_shape=[jax.ShapeDtypeStruct((t, n), _F32) for n in widths],
        compiler_params=_cparams("arbitrary"),
        name="even_proj",
    )(x, *ws)


def _odd_proj_prompt_kernel(x_ref, wk, wv, wkw, wz, wq_t, wqi_t, wkw_t, wv_t,
                            k_o, v_o, ki_o, z_o, qt_o, qit_o, wit_o, kib_o, kg_o, vt_o):
    xb = x_ref[...].astype(_BF)
    k = _dot(xb, wk[...])
    k_o[...] = k
    v_o[...] = _dot(xb, wv[...])
    kw = _dot(xb, wkw[...])
    ki_o[...] = kw[:, :_IDX_DIM]
    kib_o[...] = kw[:, :_IDX_DIM].astype(_BF)
    z_o[...] = _dot(xb, wz[...])
    qt_o[...] = (_dot_nt(wq_t[...], xb) * (_ATTN_HEAD_DIM ** -0.5)).astype(_BF)
    qit_o[...] = _dot_nt(wqi_t[...], xb).astype(_BF)
    wit_o[...] = _dot_nt(wkw_t[...], xb)
    vt = _dot_nt(wv_t[...], xb)
    ones = jnp.ones((_ATTN_HEAD_DIM, xb.shape[0]), _BF)
    for g in range(_ATTN_KV_HEADS):
        cols = slice(g * _ATTN_HEAD_DIM, (g + 1) * _ATTN_HEAD_DIM)
        kg_o[g] = k[:, cols].astype(_BF)
        vt_o[g, 0:_ATTN_HEAD_DIM, :] = vt[cols, :].astype(_BF)
        vt_o[g, _ATTN_HEAD_DIM:2 * _ATTN_HEAD_DIM, :] = ones


def _odd_proj_prompt(x, ws, wts, bsz, seq, tkb):
    t, d = x.shape
    tm = tkb
    nb = seq // tm
    _, wk, wv, _, wkw, wz = ws
    kvw = wk.shape[1]
    aw, iw = wts[0].shape[0], wts[1].shape[0]
    row = lambda n: pl.BlockSpec((tm, n), lambda i: (i, 0))
    col = lambda n: pl.BlockSpec((n, tm), lambda i: (0, i))
    outs = [
        (jax.ShapeDtypeStruct((t, kvw), _F32), row(kvw)),
        (jax.ShapeDtypeStruct((t, kvw), _F32), row(kvw)),
        (jax.ShapeDtypeStruct((t, _IDX_DIM), _F32), row(_IDX_DIM)),
        (jax.ShapeDtypeStruct((t, wz.shape[1]), _F32), row(wz.shape[1])),
        (jax.ShapeDtypeStruct((aw, t), _BF), col(aw)),
        (jax.ShapeDtypeStruct((iw, t), _BF), col(iw)),
        (jax.ShapeDtypeStruct((_LANES, t), _F32), col(_LANES)),
        (jax.ShapeDtypeStruct((t, _IDX_DIM), _BF), row(_IDX_DIM)),
        (jax.ShapeDtypeStruct((bsz, _ATTN_KV_HEADS, seq, _ATTN_HEAD_DIM), _BF),
         pl.BlockSpec((None, _ATTN_KV_HEADS, tm, _ATTN_HEAD_DIM), lambda i: (i // nb, 0, i % nb, 0))),
        (jax.ShapeDtypeStruct((bsz, nb, _ATTN_KV_HEADS, 2 * _ATTN_HEAD_DIM, tm), _BF),
         pl.BlockSpec((None, None, _ATTN_KV_HEADS, 2 * _ATTN_HEAD_DIM, tm),
                      lambda i: (i // nb, i % nb, 0, 0, 0))),
    ]
    wins = (wk, wv, wkw, wz, *wts)
    return pl.pallas_call(
        _odd_proj_prompt_kernel,
        grid=(t // tm,),
        in_specs=[row(d)] + [_const_spec(w.shape) for w in wins],
        out_specs=[o[1] for o in outs],
        out_shape=[o[0] for o in outs],
        compiler_params=_cparams("arbitrary"),
        name="odd_proj_prompt",
    )(x, *wins)


def _odd_proj_sample_kernel(x_ref, wq, wk, wv, wqi, wkw, wz, q_o, k_o, v_o, qi_o, ki_o, wi_o, z_o):
    xb = x_ref[...].astype(_BF)
    q_o[...] = _dot(xb, wq[...]) * (_ATTN_HEAD_DIM ** -0.5)
    k_o[...] = _dot(xb, wk[...])
    v_o[...] = _dot(xb, wv[...])
    qi_o[...] = _dot(xb, wqi[...])
    kw = _dot(xb, wkw[...])
    ki_o[...] = kw[:, :_IDX_DIM]
    wi_o[...] = kw
    z_o[...] = _dot(xb, wz[...])


def _odd_proj_sample(x, ws):
    t, d = x.shape
    wq, wk, wv, wqi, wkw, wz = ws
    widths = [wq.shape[1], wk.shape[1], wv.shape[1], wqi.shape[1], _IDX_DIM, _LANES, wz.shape[1]]
    full = lambda n: pl.BlockSpec((t, n), lambda i: (0, 0))
    return pl.pallas_call(
        _odd_proj_sample_kernel,
        grid=(1,),
        in_specs=[full(d)] + [_const_spec(w.shape) for w in ws],
        out_specs=[full(n) for n in widths],
        out_shape=[jax.ShapeDtypeStruct((t, n), _F32) for n in widths],
        compiler_params=_cparams("arbitrary"),
        name="odd_proj_sample",
    )(x, *ws)


def _outproj_kernel(*refs, n_in, gated):
    x_ref = refs[0]
    a_refs = refs[1:1 + n_in]
    pos = 1 + n_in
    z_ref = refs[pos] if gated else None
    pos += 1 if gated else 0
    w_refs = refs[pos:pos + n_in]
    g_ref, b_ref, o_ref = refs[pos + n_in:pos + n_in + 3]
    acc = _ALPHA * x_ref[...]
    for a_ref, w_ref in zip(a_refs, w_refs):
        a = a_ref[...]
        if gated:
            a = a.astype(_F32) * _silu(z_ref[...])
        acc = acc + _dot(a.astype(_BF), w_ref[...])
    o_ref[...] = _layer_norm(acc, g_ref[...], b_ref[...])


def _outproj_ln(x, a_list, w_list, g, b, z=None):
    t, d = x.shape
    tm = min(512, t)
    row = lambda n: pl.BlockSpec((tm, n), lambda i: (i, 0))
    gated = z is not None
    ins = [x, *a_list] + ([z] if gated else []) + [*w_list, g, b]
    specs = ([row(d)] + [row(a.shape[1]) for a in a_list] + ([row(z.shape[1])] if gated else [])
             + [_const_spec(w.shape) for w in w_list] + [_const_spec(g.shape), _const_spec(b.shape)])
    return pl.pallas_call(
        functools.partial(_outproj_kernel, n_in=len(a_list), gated=gated),
        grid=(t // tm,),
        in_specs=specs,
        out_specs=row(d),
        out_shape=jax.ShapeDtypeStruct((t, d), _F32),
        compiler_params=_cparams("arbitrary"),
        name="outproj_ln",
    )(*ins)


def _group_rmsnorm_gate(y, za, ng):
    y = y * _silu(za)
    gw = y.shape[1] // _SSD_GROUPS
    parts = []
    for g in range(_SSD_GROUPS):
        yg = y[:, g * gw:(g + 1) * gw]
        ms = jnp.mean(yg * yg, axis=-1, keepdims=True)
        parts.append(yg * lax.rsqrt(ms + _EPS))
    return jnp.concatenate(parts, axis=-1) * ng


def _even_mix_kernel(za_ref, xbc_ref, dt_ref, glu_ref, zb_ref,
                     cw_ref, cb_ref, dtb_ref, alog_ref, d_ref, ng_ref,
                     fw_ref, fb_ref, fg_ref, fbeta_ref,
                     ya_o, yb_o, ssm_o, sc_o, cc_o,
                     xbuf, ubuf, state, ushift, cacc, *, q, nc):
    c = pl.program_id(1)
    inner = _SSD_HEADS * _SSD_HEAD_DIM
    gn = _SSD_GROUPS * _SSD_STATE
    xhalo = 8
    uhalo = 32

    @pl.when(c == 0)
    def _():
        xbuf[0:xhalo, :] = jnp.zeros((xhalo, xbuf.shape[1]), _F32)
        ubuf[0:uhalo, :] = jnp.zeros((uhalo, ubuf.shape[1]), _F32)
        state[...] = jnp.zeros(state.shape, _F32)

    xbuf[xhalo:xhalo + q, :] = xbc_ref[...]
    acc = jnp.broadcast_to(cb_ref[...], (q, xbuf.shape[1]))
    for k in range(_SSD_CONV):
        off = xhalo - (_SSD_CONV - 1) + k
        acc = acc + cw_ref[k:k + 1, :] * xbuf[off:off + q, :]
    xc = _silu(acc)
    xs = xc[:, :inner]
    bm = [xc[:, inner + g * _SSD_STATE: inner + (g + 1) * _SSD_STATE].astype(_BF) for g in range(_SSD_GROUPS)]
    cm = [xc[:, inner + gn + g * _SSD_STATE: inner + gn + (g + 1) * _SSD_STATE].astype(_BF)
          for g in range(_SSD_GROUPS)]

    @pl.when(c == nc - 1)
    def _():
        sc_o[...] = xbuf[xhalo + q - (_SSD_CONV - 1):xhalo + q, :]

    xbuf[0:xhalo, :] = xbuf[q:q + xhalo, :]

    lane = lax.broadcasted_iota(jnp.int32, (1, _LANES), 1)
    hmask = lane < _SSD_HEADS
    dt = jnp.where(hmask, _softplus(dt_ref[...] + dtb_ref[...]), 0.0)
    a = jnp.where(hmask, -jnp.exp(alog_ref[...]), 0.0)
    ri = lax.broadcasted_iota(jnp.int32, (q, q), 0)
    ci = lax.broadcasted_iota(jnp.int32, (q, q), 1)
    trilb = ri >= ci
    a_cum = _dot_hi(trilb.astype(_F32), dt * a)
    a_cum_t = a_cum.T
    dt_t = dt.T
    a_last_t = a_cum_t[:, q - 1:q]
    w_t = dt_t * jnp.exp(a_last_t - a_cum_t)
    ea = jnp.exp(a_cum)
    cd_t = jnp.exp(a_last_t)
    lo_lane = lane < _SSD_HEAD_DIM
    lo_sub = lax.broadcasted_iota(jnp.int32, (_LANES, 1), 0) < _SSD_HEAD_DIM

    cb = [_dot_nt(cm[g], bm[g]) for g in range(_SSD_GROUPS)]
    heads_per_group = _SSD_HEADS // _SSD_GROUPS
    ys = []
    for j in range(_SSD_HEADS // 2):
        h0, h1 = 2 * j, 2 * j + 1
        g = h0 // heads_per_group
        sl = slice(j * _LANES, (j + 1) * _LANES)
        xs2 = xs[:, sl]
        x2 = (xs2 * jnp.where(lo_lane, dt[:, h0:h0 + 1], dt[:, h1:h1 + 1])).astype(_BF)
        l0 = jnp.where(trilb, jnp.exp(a_cum[:, h0:h0 + 1] - a_cum_t[h0:h0 + 1, :]), 0.0)
        l1 = jnp.where(trilb, jnp.exp(a_cum[:, h1:h1 + 1] - a_cum_t[h1:h1 + 1, :]), 0.0)
        y0 = _dot((cb[g] * l0).astype(_BF), x2)
        y1 = _dot((cb[g] * l1).astype(_BF), x2)
        sp = state[sl, :]
        yoff = _dot_nt(cm[g], sp.astype(_BF)) * jnp.where(lo_lane, ea[:, h0:h0 + 1], ea[:, h1:h1 + 1])
        w2 = jnp.where(lo_sub, w_t[h0:h0 + 1, :], w_t[h1:h1 + 1, :])
        s_chunk = _dot((xs2.T * w2).astype(_BF), bm[g])
        cd2 = jnp.where(lo_sub, cd_t[h0:h0 + 1, :], cd_t[h1:h1 + 1, :])
        state[sl, :] = sp * cd2 + s_chunk
        ys.append(jnp.where(lo_lane, y0, y1) + yoff + d_ref[:, sl] * xs2)
    y = jnp.concatenate(ys, axis=-1)
    ya_o[...] = _group_rmsnorm_gate(y, za_ref[...], ng_ref[...]).astype(_BF)

    @pl.when(c == nc - 1)
    def _():
        ssm_o[...] = state[...]

    cf = ubuf.shape[1]
    u = glu_ref[:, :cf] * _sigmoid(glu_ref[:, cf:])
    ubuf[uhalo:uhalo + q, :] = u
    back = 8 * ((_CF_KERNEL - 1) // 8)
    for r in range(1, 8):
        ushift[r - 1] = ubuf[uhalo - back - r:uhalo + q - r, :]
    for cb in range(cf // _LANES):
        cs = slice(cb * _LANES, (cb + 1) * _LANES)
        acc = jnp.broadcast_to(fb_ref[:, cs], (q, _LANES))
        for j in range(_CF_KERNEL):
            r, a = j % 8, j // 8
            k = _CF_KERNEL - 1 - j
            if r == 0:
                win = ubuf[uhalo - 8 * a:uhalo - 8 * a + q, cs]
            else:
                win = ushift[r - 1, back - 8 * a:back - 8 * a + q, cs]
            acc = acc + fw_ref[k:k + 1, cs] * win
        cacc[:, cs] = acc
    v = _silu(_layer_norm(cacc[...], fg_ref[...], fbeta_ref[...]))
    yb_o[...] = (v * _silu(zb_ref[...])).astype(_BF)

    @pl.when(c == nc - 1)
    def _():
        cc_o[...] = ubuf[uhalo + q - (_CF_KERNEL - 1):uhalo + q, :]

    ubuf[0:uhalo, :] = ubuf[q:q + uhalo, :]


def _even_mix_prompt(za, xbc, dt, glu, zb, params, bsz, seq):
    q = 128
    nc = seq // q
    inner = _SSD_HEADS * _SSD_HEAD_DIM
    conv_dim = xbc.shape[1]
    cf = zb.shape[1]
    row = lambda n: pl.BlockSpec((q, n), lambda b, c: (b * nc + c, 0))
    per_b = lambda r, n: pl.BlockSpec((None, r, n), lambda b, c: (b, 0, 0))
    t = bsz * seq
    return pl.pallas_call(
        functools.partial(_even_mix_kernel, q=q, nc=nc),
        grid=(bsz, nc),
        in_specs=[row(inner), row(conv_dim), row(_LANES), row(2 * cf), row(cf)]
                 + [_const_spec(p.shape) for p in params],
        out_specs=[row(inner), row(cf), per_b(inner, _SSD_STATE),
                   per_b(_SSD_CONV - 1, conv_dim), per_b(_CF_KERNEL - 1, cf)],
        out_shape=[jax.ShapeDtypeStruct((t, inner), _BF), jax.ShapeDtypeStruct((t, cf), _BF),
                   jax.ShapeDtypeStruct((bsz, inner, _SSD_STATE), _F32),
                   jax.ShapeDtypeStruct((bsz, _SSD_CONV - 1, conv_dim), _F32),
                   jax.ShapeDtypeStruct((bsz, _CF_KERNEL - 1, cf), _F32)],
        scratch_shapes=[pltpu.VMEM((8 + q, conv_dim), _F32), pltpu.VMEM((32 + q, cf), _F32),
                        pltpu.VMEM((inner, _SSD_STATE), _F32),
                        pltpu.VMEM((7, q + 8 * ((_CF_KERNEL - 1) // 8), cf), _F32),
                        pltpu.VMEM((q, cf), _F32)],
        compiler_params=_cparams("arbitrary", "arbitrary"),
        name="even_mix_prompt",
    )(za, xbc, dt, glu, zb, *params)


def _sample_even_pre_kernel(xbc_ref, dt_ref, glu_ref, zb_ref, sctx_ref, cctx_ref,
                            cw_ref, cb_ref, dtb_ref, alog_ref, fw_ref, fb_ref, fg_ref, fbeta_ref,
                            xs_o, bm_o, cm_o, dect_o, xdtt_o, yb_o, sctx_o, cctx_o, pad):
    db = xbc_ref.shape[0]
    inner = _SSD_HEADS * _SSD_HEAD_DIM
    gn = _SSD_GROUPS * _SSD_STATE
    xbc = xbc_ref[...]
    acc = cb_ref[...] + cw_ref[_SSD_CONV - 1:_SSD_CONV, :] * xbc
    for k in range(_SSD_CONV - 1):
        acc = acc + cw_ref[k:k + 1, :] * sctx_ref[k]
    xc = _silu(acc)
    xs = xc[:, :inner]
    xs_o[...] = xs
    bm_o[...] = xc[:, inner:inner + gn]
    cm_o[...] = xc[:, inner + gn:inner + 2 * gn]
    for k in range(_SSD_CONV - 2):
        sctx_o[k] = sctx_ref[k + 1]
    sctx_o[_SSD_CONV - 2] = xbc

    lane = lax.broadcasted_iota(jnp.int32, (1, _LANES), 1)
    hmask = lane < _SSD_HEADS
    dt = jnp.where(hmask, _softplus(dt_ref[...] + dtb_ref[...]), 0.0)
    a = jnp.where(hmask, -jnp.exp(alog_ref[...]), 0.0)
    dec = jnp.where(hmask, jnp.exp(dt * a), 0.0)
    er = lax.broadcasted_iota(jnp.int32, (_LANES, inner), 0)
    ec = lax.broadcasted_iota(jnp.int32, (_LANES, inner), 1)
    expand = jnp.where((ec // _SSD_HEAD_DIM) == er, 1.0, 0.0)
    dec_x = _dot_hi(dec, expand)
    xdt = xs * _dot_hi(dt, expand)
    for src, dst in ((dec_x, dect_o), (xdt, xdtt_o)):
        pad[...] = jnp.zeros(pad.shape, _F32)
        pad[0:db, :] = src
        for j in range(inner // _LANES):
            dst[j * _LANES:(j + 1) * _LANES, :] = pad[:, j * _LANES:(j + 1) * _LANES].T

    cf = zb_ref.shape[1]
    u = glu_ref[:, :cf] * _sigmoid(glu_ref[:, cf:])
    acc = fb_ref[...] + fw_ref[_CF_KERNEL - 1:_CF_KERNEL, :] * u
    for k in range(_CF_KERNEL - 1):
        acc = acc + fw_ref[k:k + 1, :] * cctx_ref[k]
    v = _silu(_layer_norm(acc, fg_ref[...], fbeta_ref[...]))
    yb_o[...] = (v * _silu(zb_ref[...])).astype(_BF)
    for k in range(_CF_KERNEL - 2):
        cctx_o[k] = cctx_ref[k + 1]
    cctx_o[_CF_KERNEL - 2] = u


def _sample_even_pre(xbc, dt, glu, zb, sctx_t, cctx_t, params):
    db = xbc.shape[0]
    inner = _SSD_HEADS * _SSD_HEAD_DIM
    gn = _SSD_GROUPS * _SSD_STATE
    cf = zb.shape[1]
    ins = [xbc, dt, glu, zb, sctx_t, cctx_t, *params]
    full = lambda s: pl.BlockSpec(s, lambda i: (0,) * len(s))
    out_shapes = [(db, inner), (db, gn), (db, gn), (inner, _LANES), (inner, _LANES), (db, cf),
                  sctx_t.shape, cctx_t.shape]
    out_dtypes = [_F32, _F32, _F32, _F32, _F32, _BF, _F32, _F32]
    return pl.pallas_call(
        _sample_even_pre_kernel,
        grid=(1,),
        in_specs=[full(a.shape) for a in ins],
        out_specs=[full(s) for s in out_shapes],
        out_shape=[jax.ShapeDtypeStruct(s, d) for s, d in zip(out_shapes, out_dtypes)],
        scratch_shapes=[pltpu.VMEM((_LANES, inner), _F32)],
        compiler_params=_cparams("arbitrary"),
        name="sample_even_pre",
    )(*ins)


def _sample_even_rec_kernel(st_ref, dect_ref, xdtt_ref, bm_ref, cm_ref, xs_ref, za_ref, d_ref, ng_ref,
                            st_o, ya_o):
    b = pl.program_id(0)
    inner = st_ref.shape[0]
    half = inner // _SSD_GROUPS
    sel = jnp.where(lax.broadcasted_iota(jnp.int32, (_LANES, _LANES), 0) == b, 1.0, 0.0)
    dec = _dot_hi(dect_ref[...], sel)
    xd = _dot_hi(xdtt_ref[...], sel)
    grp0 = lax.broadcasted_iota(jnp.int32, (inner, 1), 0) < half
    brow = jnp.where(grp0, bm_ref[:, :_SSD_STATE], bm_ref[:, _SSD_STATE:])
    s_new = st_ref[...] * dec + xd * brow
    st_o[...] = s_new
    r8 = lax.broadcasted_iota(jnp.int32, (8, 1), 0)
    c8 = jnp.where(r8 == 0, cm_ref[:, :_SSD_STATE], jnp.where(r8 == 1, cm_ref[:, _SSD_STATE:], 0.0))
    yr = _dot_nt_hi(c8, s_new)
    lane = lax.broadcasted_iota(jnp.int32, (1, inner), 1)
    xs = xs_ref[...]
    y = jnp.where(lane < half, yr[0:1, :], yr[1:2, :]) + d_ref[...] * xs
    ya_o[...] = _group_rmsnorm_gate(y, za_ref[...], ng_ref[...]).astype(_BF)


def _sample_even_rec(state, dect, xdtt, bm, cm, xs, za, d_x, ng):
    db, inner, n = state.shape
    per_b = lambda s: pl.BlockSpec((None,) + s, lambda b: (b, 0, 0))
    gn = bm.shape[1]
    r3 = lambda a: a.reshape(db, 1, a.shape[1])
    return pl.pallas_call(
        _sample_even_rec_kernel,
        grid=(db,),
        in_specs=[per_b((inner, n)), _const_spec(dect.shape), _const_spec(xdtt.shape),
                  per_b((1, gn)), per_b((1, gn)), per_b((1, inner)), per_b((1, inner)),
                  _const_spec(d_x.shape), _const_spec(ng.shape)],
        out_specs=[per_b((inner, n)), per_b((1, inner))],
        out_shape=[jax.ShapeDtypeStruct((db, inner, n), _F32), jax.ShapeDtypeStruct((db, 1, inner), _BF)],
        compiler_params=_cparams("arbitrary"),
        name="sample_even_rec",
    )(state, dect, xdtt, r3(bm), r3(cm), r3(xs), r3(za), d_x, ng)


def _bisect_threshold(count_ge, rmin, rmax, n_adm, c_ge0, c_gt0, k):
    nonneg = c_ge0 >= k
    lo0 = jnp.where(n_adm <= k, rmin, jnp.where(nonneg, 0.0, rmin))
    c0 = jnp.where(n_adm <= k, n_adm, jnp.where(nonneg, c_ge0, n_adm))
    hi0 = jnp.where(c_gt0 >= k, rmax + (rmax - lo0), 0.0)

    def midpoint(lo, hi):
        return lo + (hi - lo) * 0.5

    def open_rows(lo, hi, c_lo, mid):
        return jnp.where(c_lo > k, jnp.where(mid > lo, jnp.where(mid < hi, 1.0, 0.0), 0.0), 0.0)

    def cond(st):
        it, _, _, _, _, todo = st
        return jnp.logical_and(it < _MAX_BISECT, jnp.max(todo) > 0.5)

    def body(st):
        it, lo, hi, c_lo, mid, _ = st
        c = count_ge(mid)
        ge = c >= k
        lo_n = jnp.where(ge, mid, lo)
        hi_n = jnp.where(ge, hi, mid)
        c_n = jnp.where(ge, c, c_lo)
        mid_n = midpoint(lo_n, hi_n)
        return it + 1, lo_n, hi_n, c_n, mid_n, open_rows(lo_n, hi_n, c_n, mid_n)

    mid0 = midpoint(lo0, hi0)
    st = (jnp.int32(0), lo0, hi0, c0, mid0, open_rows(lo0, hi0, c0, mid0))
    _, lo, _, c_lo, _, _ = lax.while_loop(cond, body, st)
    return lo, c_lo


def _fold8(x, op):
    n, c = x.shape
    if n % 64 == 0 and n > 64:
        x = op(x.reshape(n // 64, 64, c), axis=0)
    return op(x.reshape(x.shape[0] // 8, 8, c), axis=0)


def _attn_prompt_kernel(qit_ref, wit_ref, qt_ref, z_ref, ki_ref, kg_ref, vt_ref, o_ref,
                        sc_ref, qis_ref, qg_ref, acc_ref, m_ref, al_ref, s_ref, p_ref, ob_ref,
                        *, tq, tkb, topk):
    i = pl.program_id(1)
    nkb = (i * tq + tq + tkb - 1) // tkb
    qpos = i * tq + lax.broadcasted_iota(jnp.int32, (1, tq), 1)
    kf = jnp.float32(topk)
    inf = jnp.float32(jnp.inf)
    rep = _ATTN_HEADS // _ATTN_KV_HEADS
    sk = _LANES

    qit = qit_ref[...]
    for h in range(_IDX_HEADS):
        qis_ref[:, h * tq:(h + 1) * tq] = qit[h * _IDX_DIM:(h + 1) * _IDX_DIM, :]
    w = wit_ref[...] * (_IDX_HEADS ** -0.5 * _IDX_DIM ** -0.5)
    wrs = [w[_IDX_DIM + h:_IDX_DIM + h + 1, :] for h in range(_IDX_HEADS)]

    def score_body(kb, carry):
        rmin, rmax, cge0, cgt0 = carry
        for jb in range(tkb // sk):
            start = pl.multiple_of(kb * tkb + jb * sk, sk)
            kblk = ki_ref[pl.ds(start, sk), :]
            acc = jnp.zeros((sk, tq), _F32)
            for h in range(_IDX_HEADS):
                s = _dot(kblk, qis_ref[:, h * tq:(h + 1) * tq])
                acc = acc + jnp.maximum(s, 0.0) * wrs[h]
            kpos = start + lax.broadcasted_iota(jnp.int32, (sk, 1), 0)
            adm = kpos <= qpos
            sc = jnp.where(adm, acc, -inf)
            sc_ref[kb, jb * sk:(jb + 1) * sk, :] = sc
            rmax = jnp.maximum(rmax, _fold8(sc, jnp.max))
            rmin = jnp.minimum(rmin, _fold8(jnp.where(adm, acc, inf), jnp.min))
            cge0 = cge0 + _fold8(jnp.where(sc >= 0.0, 1.0, 0.0), jnp.sum)
            cgt0 = cgt0 + _fold8(jnp.where(sc > 0.0, 1.0, 0.0), jnp.sum)
        return rmin, rmax, cge0, cgt0

    zeros8 = jnp.zeros((8, tq), _F32)
    rmin, rmax, cge0, cgt0 = lax.fori_loop(
        0, nkb, score_body,
        (jnp.full((8, tq), inf, _F32), jnp.full((8, tq), -inf, _F32), zeros8, zeros8))
    rmin = jnp.min(rmin, axis=0, keepdims=True)
    rmax = jnp.max(rmax, axis=0, keepdims=True)
    cge0 = jnp.sum(cge0, axis=0, keepdims=True)
    cgt0 = jnp.sum(cgt0, axis=0, keepdims=True)

    def count_ge(x):
        def cbody(kb, cnt):
            return cnt + _fold8(jnp.where(sc_ref[kb] >= x, 1.0, 0.0), jnp.sum)
        cnt = lax.fori_loop(0, nkb, cbody, zeros8)
        return jnp.sum(cnt, axis=0, keepdims=True)

    n_adm = (qpos + 1).astype(_F32)
    lo, c_lo = _bisect_threshold(count_ge, rmin, rmax, n_adm, cge0, cgt0, kf)

    @pl.when(jnp.max(c_lo) > kf)
    def _():
        def gbody(kb, cnt):
            return cnt + _fold8(jnp.where(sc_ref[kb] > lo, 1.0, 0.0), jnp.sum)
        need = kf - jnp.sum(lax.fori_loop(0, nkb, gbody, zeros8), axis=0, keepdims=True)
        tri = jnp.where(lax.broadcasted_iota(jnp.int32, (tkb, tkb), 0)
                        >= lax.broadcasted_iota(jnp.int32, (tkb, tkb), 1), 1.0, 0.0).astype(_BF)

        def tbody(kb, carry):
            s = sc_ref[kb]
            eq = s == lo
            pre = _dot(tri, jnp.where(eq, 1.0, 0.0).astype(_BF))
            drop = jnp.logical_and(eq, (carry + pre) > need)
            sc_ref[kb] = jnp.where(drop, -inf, s)
            return carry + pre[tkb - 1:tkb, :]
        lax.fori_loop(0, nkb, tbody, jnp.zeros((1, tq), _F32))

    qt = qt_ref[...]
    for h in range(_ATTN_HEADS):
        g, r = divmod(h, rep)
        qg_ref[g, :, r * tq:(r + 1) * tq] = qt[h * _ATTN_HEAD_DIM:(h + 1) * _ATTN_HEAD_DIM, :]
    m_ref[...] = jnp.full(m_ref.shape, _NEG, _F32)
    acc_ref[...] = jnp.zeros(acc_ref.shape, _F32)

    def bias_body(kb, carry):
        sc_ref[kb] = jnp.where(sc_ref[kb] >= lo, 0.0, _NEG)
        return carry
    lax.fori_loop(0, nkb, bias_body, 0)

    def attn_body(kb, carry):
        base = pl.multiple_of(kb * tkb, tkb)
        for g in range(_ATTN_KV_HEADS):
            mxs = [None] * rep
            for jb in range(tkb // _LANES):
                rows = slice(jb * _LANES, (jb + 1) * _LANES)
                kblk = kg_ref[g, pl.ds(base + jb * _LANES, _LANES), :]
                bias = sc_ref[kb, rows, :]
                for r in range(rep):
                    cols = slice(r * tq, (r + 1) * tq)
                    s = _dot(kblk, qg_ref[g, :, cols]) + bias
                    s_ref[g, rows, cols] = s
                    part = _fold8(s, jnp.max)
                    mxs[r] = part if mxs[r] is None else jnp.maximum(mxs[r], part)
            mx = jnp.concatenate([jnp.max(m8, axis=0, keepdims=True) for m8 in mxs], axis=1)
            m_old = m_ref[g]
            m_new = jnp.maximum(m_old, mx)
            al_ref[g] = jnp.exp(m_old - m_new)
            m_ref[g] = m_new
        for g in range(_ATTN_KV_HEADS):
            for r in range(rep):
                cols = slice(r * tq, (r + 1) * tq)
                m_new = m_ref[g, :, cols]
                for jb in range(tkb // _LANES):
                    rows = slice(jb * _LANES, (jb + 1) * _LANES)
                    p_ref[g, rows, cols] = jnp.exp(s_ref[g, rows, cols] - m_new).astype(_BF)
                acc_ref[g, :, cols] = (acc_ref[g, :, cols] * al_ref[g, :, cols]
                                       + _dot(vt_ref[kb, g], p_ref[g, :, cols]))
        return carry

    lax.fori_loop(0, nkb, attn_body, 0)
    for h in range(_ATTN_HEADS):
        g, r = divmod(h, rep)
        acc = acc_ref[g, :, r * tq:(r + 1) * tq].T
        o_h = acc / pltpu.roll(acc, _ATTN_HEAD_DIM, 1)
        ob_ref[:, h * _ATTN_HEAD_DIM:(h + 1) * _ATTN_HEAD_DIM] = o_h[:, :_ATTN_HEAD_DIM]
    o_ref[...] = (ob_ref[...] * _silu(z_ref[...])).astype(_BF)


def _attn_prompt(qit, wit, qt, z, ki, kg, vt, bsz, seq, tkb):
    tq = 256
    nq = seq // tq
    nkbt = seq // tkb
    t = bsz * seq
    topk = min(_TOPK_MAX, seq // 4)
    width = z.shape[1]
    rep = _ATTN_HEADS // _ATTN_KV_HEADS
    row = lambda n: pl.BlockSpec((tq, n), lambda b, i: (b * nq + i, 0))
    col = lambda n: pl.BlockSpec((n, tq), lambda b, i: (0, b * nq + i))
    once = dict(pipeline_mode=pl.Buffered(1))
    return pl.pallas_call(
        functools.partial(_attn_prompt_kernel, tq=tq, tkb=tkb, topk=topk),
        grid=(bsz, nq),
        in_specs=[col(qit.shape[0]), col(_LANES), col(qt.shape[0]), row(width),
                  pl.BlockSpec((seq, _IDX_DIM), lambda b, i: (b, 0), **once),
                  pl.BlockSpec((None,) + kg.shape[1:], lambda b, i: (b, 0, 0, 0), **once),
                  pl.BlockSpec((None,) + vt.shape[1:], lambda b, i: (b, 0, 0, 0, 0), **once)],
        out_specs=row(width),
        out_shape=jax.ShapeDtypeStruct((t, width), _BF),
        scratch_shapes=[pltpu.VMEM((nkbt, tkb, tq), _F32),
                        pltpu.VMEM((_IDX_DIM, _IDX_HEADS * tq), _BF),
                        pltpu.VMEM((_ATTN_KV_HEADS, _ATTN_HEAD_DIM, rep * tq), _BF),
                        pltpu.VMEM((_ATTN_KV_HEADS, 2 * _ATTN_HEAD_DIM, rep * tq), _F32),
                        pltpu.VMEM((_ATTN_KV_HEADS, 1, rep * tq), _F32),
                        pltpu.VMEM((_ATTN_KV_HEADS, 1, rep * tq), _F32),
                        pltpu.VMEM((_ATTN_KV_HEADS, tkb, rep * tq), _F32),
                        pltpu.VMEM((_ATTN_KV_HEADS, tkb, rep * tq), _BF),
                        pltpu.VMEM((tq, width), _F32)],
        compiler_params=_cparams("arbitrary", "arbitrary"),
        name="attn_prompt",
    )(qit, wit, qt, z, ki, kg, vt)


def _attn_sample_kernel(pt_ref, qi_ref, w_ref, kin_ref, q_ref, kn_ref, vn_ref,
                        ckit_ref, ckt_ref, cvt_ref, o_ref,
                        kibuf, kbuf, vbuf, sc_ref, sn_ref, lg_ref, sem, *, n_pages, topk):
    b = pl.program_id(0)
    past = n_pages * _PAGE
    kf = jnp.float32(topk)
    inf = jnp.float32(jnp.inf)
    caches = ((ckit_ref, kibuf), (ckt_ref, kbuf), (cvt_ref, vbuf))
    unroll = min(8, n_pages)

    def page_copy(which, page):
        cache_ref, buf = caches[which]
        return pltpu.make_async_copy(cache_ref.at[pt_ref[b, page]], buf.at[page], sem.at[which])

    def start_all(which):
        lax.fori_loop(0, n_pages, lambda p, c: (page_copy(which, p).start(), c)[1], 0)

    def wait_all(which):
        lax.fori_loop(0, n_pages, lambda p, c: (page_copy(which, p).wait(), c)[1], 0)

    for which in range(3):
        start_all(which)

    def total(x):
        return jnp.sum(jnp.sum(x, axis=1, keepdims=True), axis=0, keepdims=True)

    qi8 = qi_ref[...].astype(_BF)
    w8 = w_ref[...] * (_IDX_HEADS ** -0.5 * _IDX_DIM ** -0.5)
    wait_all(0)

    def score_body(p, c):
        s8 = _dot(qi8, kibuf[p].astype(_BF))
        sc_ref[pl.ds(p, 1), :] = jnp.sum(jnp.maximum(s8, 0.0) * w8, axis=0, keepdims=True)
        return c
    lax.fori_loop(0, n_pages, score_body, 0, unroll=unroll)
    kin = kin_ref[...].astype(_BF).astype(_F32)
    sn8 = jnp.sum(qi8.astype(_F32) * kin, axis=1, keepdims=True)
    sn_ref[...] = jnp.sum(jnp.maximum(sn8, 0.0) * w8, axis=0, keepdims=True)
    snew = sn_ref[...]

    if past + 1 > topk:
        sc = sc_ref[...]
        rmax = jnp.maximum(jnp.max(jnp.max(sc, axis=1, keepdims=True), axis=0, keepdims=True), snew)
        rmin = jnp.minimum(jnp.min(jnp.min(sc, axis=1, keepdims=True), axis=0, keepdims=True), snew)
        one = lambda cnd: jnp.where(cnd, 1.0, 0.0)

        def count_ge(x):
            return total(one(sc_ref[...] >= x)) + one(snew >= x)

        lo, c_lo = _bisect_threshold(count_ge, rmin, rmax, jnp.full((1, 1), past + 1.0, _F32),
                                     count_ge(jnp.zeros((1, 1), _F32)),
                                     total(one(sc > 0.0)) + one(snew > 0.0), kf)

        @pl.when(jnp.max(c_lo) > kf)
        def _():
            need = kf - (total(one(sc_ref[...] > lo)) + one(snew > lo))
            tri = jnp.where(lax.broadcasted_iota(jnp.int32, (_LANES, _LANES), 0)
                            <= lax.broadcasted_iota(jnp.int32, (_LANES, _LANES), 1), 1.0, 0.0).astype(_BF)

            def tbody(p, carry):
                s = sc_ref[pl.ds(p, 1), :]
                eq = s == lo
                eq8 = jnp.broadcast_to(one(eq), (8, _LANES)).astype(_BF)
                pre = _dot(eq8, tri)[0:1, :]
                drop = jnp.logical_and(eq, (carry + pre) > need)
                sc_ref[pl.ds(p, 1), :] = jnp.where(drop, -inf, s)
                return carry + pre[:, _LANES - 1:_LANES]
            carry = lax.fori_loop(0, n_pages, tbody, jnp.zeros((1, 1), _F32))
            drop_new = jnp.logical_and(snew == lo, (carry + 1.0) > need)
            sn_ref[...] = jnp.where(drop_new, -inf, snew)
    else:
        lo = jnp.full((1, 1), -inf, _F32)

    kvw = _ATTN_KV_HEADS * _ATTN_HEAD_DIM
    rep = _ATTN_HEADS // _ATTN_KV_HEADS
    rr = lax.broadcasted_iota(jnp.int32, (_ATTN_HEAD_DIM, kvw), 0)
    rc = lax.broadcasted_iota(jnp.int32, (_ATTN_HEAD_DIM, kvw), 1)
    spread = jnp.where((rc % _ATTN_HEAD_DIM) == rr, 1.0, 0.0)
    hr = lax.broadcasted_iota(jnp.int32, (_ATTN_HEADS, kvw), 0)
    hc = lax.broadcasted_iota(jnp.int32, (_ATTN_HEADS, kvw), 1)
    own = (hc // _ATTN_HEAD_DIM) == (hr // rep)
    qbd = jnp.where(own, _dot_hi(q_ref[...], spread), 0.0).astype(_BF)
    kn = kn_ref[...].astype(_BF).astype(_F32)
    lnew = (jnp.sum(qbd.astype(_F32) * kn, axis=1, keepdims=True)
            + jnp.where(sn_ref[...] >= lo, 0.0, _NEG))
    wait_all(1)

    def logit_body(p, mx):
        lg = _dot(qbd, kbuf[p].astype(_BF)) + jnp.where(sc_ref[pl.ds(p, 1), :] >= lo, 0.0, _NEG)
        lg_ref[p] = lg
        return jnp.maximum(mx, lg)
    mx = lax.fori_loop(0, n_pages, logit_body, jnp.full((_ATTN_HEADS, _LANES), _NEG, _F32), unroll=unroll)
    m = jnp.maximum(jnp.max(mx, axis=1, keepdims=True), lnew)
    pn = jnp.exp(lnew - m)
    vn = vn_ref[...].astype(_BF).astype(_F32)
    wait_all(2)

    def pv_body(p, carry):
        psum, out = carry
        pr = jnp.exp(lg_ref[p] - m)
        return psum + pr, out + _dot_nt(pr.astype(_BF), vbuf[p].astype(_BF))
    psum, out = lax.fori_loop(0, n_pages, pv_body,
                              (jnp.zeros((_ATTN_HEADS, _LANES), _F32), pn.astype(_BF).astype(_F32) * vn),
                              unroll=unroll)
    den = jnp.sum(psum, axis=1, keepdims=True) + pn
    out = jnp.where(own, out / den, 0.0)
    gather = jnp.where((lax.broadcasted_iota(jnp.int32, (kvw, _ATTN_HEAD_DIM), 0) % _ATTN_HEAD_DIM)
                       == lax.broadcasted_iota(jnp.int32, (kvw, _ATTN_HEAD_DIM), 1), 1.0, 0.0)
    o_ref[...] = _dot_hi(out, gather)


def _attn_sample(page_table, qi, wi, ki_new, q, k_new, v_new, cache_ki, cache_k, cache_v):
    db, n_pages = page_table.shape
    past = n_pages * _PAGE
    topk = min(_TOPK_MAX, (past + 1) // 4)
    kvw = _ATTN_KV_HEADS * _ATTN_HEAD_DIM
    n_pool = cache_k.shape[0]
    per_b = lambda s: pl.BlockSpec((None,) + s, lambda b, pt: (b,) + (0,) * len(s))
    anyspec = pl.BlockSpec(memory_space=pl.ANY)
    kv_t = lambda c: jnp.transpose(c, (0, 2, 3, 1)).reshape(n_pool, kvw, _PAGE)
    ins = [qi.reshape(db, _IDX_HEADS, _IDX_DIM),
           wi[:, _IDX_DIM:_IDX_DIM + _IDX_HEADS].reshape(db, _IDX_HEADS, 1),
           ki_new.reshape(db, 1, _IDX_DIM),
           q.reshape(db, _ATTN_HEADS, _ATTN_HEAD_DIM),
           k_new.reshape(db, 1, kvw), v_new.reshape(db, 1, kvw),
           jnp.transpose(cache_ki, (0, 2, 1)), kv_t(cache_k), kv_t(cache_v)]
    grid_spec = pltpu.PrefetchScalarGridSpec(
        num_scalar_prefetch=1,
        grid=(db,),
        in_specs=[per_b((_IDX_HEADS, _IDX_DIM)), per_b((_IDX_HEADS, 1)), per_b((1, _IDX_DIM)),
                  per_b((_ATTN_HEADS, _ATTN_HEAD_DIM)), per_b((1, kvw)), per_b((1, kvw)),
                  anyspec, anyspec, anyspec],
        out_specs=per_b((_ATTN_HEADS, _ATTN_HEAD_DIM)),
        scratch_shapes=[pltpu.VMEM((n_pages, _IDX_DIM, _PAGE), _F32),
                        pltpu.VMEM((n_pages, kvw, _PAGE), _F32),
                        pltpu.VMEM((n_pages, kvw, _PAGE), _F32),
                        pltpu.VMEM((n_pages, _PAGE), _F32), pltpu.VMEM((1, 1), _F32),
                        pltpu.VMEM((n_pages, _ATTN_HEADS, _PAGE), _F32),
                        pltpu.SemaphoreType.DMA((3,))],
    )
    o = pl.pallas_call(
        functools.partial(_attn_sample_kernel, n_pages=n_pages, topk=topk),
        grid_spec=grid_spec,
        out_shape=jax.ShapeDtypeStruct((db, _ATTN_HEADS, _ATTN_HEAD_DIM), _F32),
        compiler_params=_cparams("arbitrary"),
        name="attn_sample",
    )(page_table, *ins)
    return o.reshape(db, _ATTN_HEADS * _ATTN_HEAD_DIM)


def _pad_lanes(a, n=_LANES):
    return jnp.pad(a, [(0, 0)] * (a.ndim - 1) + [(0, n - a.shape[-1])])


def _even_weights(w_in, conv_w, conv_b, dt_bias, a_log, d_skip, norm_g, cf_w, cf_b, cf_g, cf_beta, w_out):
    inner = _SSD_HEADS * _SSD_HEAD_DIM
    conv_dim = conv_w.shape[1]
    cf = cf_w.shape[1]
    o1, o2, o3, o4 = inner, inner + conv_dim, inner + conv_dim + _SSD_HEADS, inner + conv_dim + _SSD_HEADS + 2 * cf
    wb = w_in.astype(_BF)
    proj = (wb[:, :o1], wb[:, o1:o2], _pad_lanes(wb[:, o2:o3]), wb[:, o3:o4], wb[:, o4:])
    row = lambda v: v.reshape(1, -1)
    ssd = (conv_w, row(conv_b), _pad_lanes(row(dt_bias)), _pad_lanes(row(a_log)))
    d_x = row(jnp.repeat(d_skip, _SSD_HEAD_DIM))
    cfp = (cf_w, row(cf_b), row(cf_g), row(cf_beta))
    wo = w_out.astype(_BF)
    return proj, ssd, d_x, row(norm_g), cfp, (wo[:inner], wo[inner:])


def _even_layer_prompt(x, wts, ln_g, ln_b, bsz, seq):
    proj, ssd, d_x, ng, cfp, wo = wts
    za, xbc, dt, glu, zb = _even_proj(x, proj)
    ya, yb, ssm, sc, cc = _even_mix_prompt(za, xbc, dt, glu, zb, (*ssd, d_x, ng, *cfp), bsz, seq)
    x_new = _outproj_ln(x, [ya, yb], wo, ln_g, ln_b)
    return x_new, ssm.reshape(bsz, _SSD_HEADS, _SSD_HEAD_DIM, _SSD_STATE), sc, cc


def _even_layer_sample(x, st_ssm, st_sconv, st_cconv, wts, ln_g, ln_b):
    proj, ssd, d_x, ng, cfp, wo = wts
    db = x.shape[0]
    inner = _SSD_HEADS * _SSD_HEAD_DIM
    za, xbc, dt, glu, zb = _even_proj(x, proj)
    xs, bm, cm, dect, xdtt, yb, sctx_n, cctx_n = _sample_even_pre(
        xbc, dt, glu, zb, jnp.swapaxes(st_sconv, 0, 1), jnp.swapaxes(st_cconv, 0, 1), (*ssd, *cfp))
    st_new, ya = _sample_even_rec(st_ssm.reshape(db, inner, _SSD_STATE), dect, xdtt, bm, cm, xs, za, d_x, ng)
    x_new = _outproj_ln(x, [ya.reshape(db, inner), yb], wo, ln_g, ln_b)
    return (x_new, st_new.reshape(st_ssm.shape), jnp.swapaxes(sctx_n, 0, 1), jnp.swapaxes(cctx_n, 0, 1))


def _odd_weights(w_in, w_out):
    aw = _ATTN_HEADS * _ATTN_HEAD_DIM
    kvw = _ATTN_KV_HEADS * _ATTN_HEAD_DIM
    iw = _IDX_HEADS * _IDX_DIM
    o1, o2, o3, o4 = aw, aw + kvw, aw + 2 * kvw, aw + 2 * kvw + iw
    o5 = o4 + _IDX_DIM + _IDX_HEADS
    wb = w_in.astype(_BF)
    ws = (wb[:, :o1], wb[:, o1:o2], wb[:, o2:o3], wb[:, o3:o4], _pad_lanes(wb[:, o4:o5]), wb[:, o5:])
    wts = (ws[0].T, ws[3].T, ws[4].T, ws[2].T)
    return ws, wts, w_out.astype(_BF)


def _odd_layer_prompt(x, wts, ln_g, ln_b, bsz, seq):
    ws, wtr, wo = wts
    tkb = min(512, seq)
    k, v, ki, z, qt, qit, wit, kib, kg, vt = _odd_proj_prompt(x, ws, wtr, bsz, seq, tkb)
    o = _attn_prompt(qit, wit, qt, z, kib, kg, vt, bsz, seq, tkb)
    x_new = _outproj_ln(x, [o], [wo], ln_g, ln_b)
    return (x_new, k.reshape(bsz, seq, _ATTN_KV_HEADS, _ATTN_HEAD_DIM),
            v.reshape(bsz, seq, _ATTN_KV_HEADS, _ATTN_HEAD_DIM), ki.reshape(bsz, seq, _IDX_DIM))


def _odd_layer_sample(x, cache_k, cache_v, cache_ki, page_table, wts, ln_g, ln_b):
    ws, _, wo = wts
    db = x.shape[0]
    q, k, v, qi, ki, wi, z = _odd_proj_sample(x, ws)
    o = _attn_sample(page_table, qi, wi, ki, q, k, v, cache_ki, cache_k, cache_v)
    x_new = _outproj_ln(x, [o], [wo], ln_g, ln_b, z=z)
    return (x_new, k.reshape(db, 1, _ATTN_KV_HEADS, _ATTN_HEAD_DIM),
            v.reshape(db, 1, _ATTN_KV_HEADS, _ATTN_HEAD_DIM), ki.reshape(db, 1, _IDX_DIM))


def kernel(x_prompt, x_sample, state_ssm_l0, state_ssdconv_l0, state_cfconv_l0, cache_k_l1, cache_v_l1, cache_kidx_l1, state_ssm_l2, state_ssdconv_l2, state_cfconv_l2, cache_k_l3, cache_v_l3, cache_kidx_l3, page_table, w_in_even, ssd_conv_w, ssd_conv_b, ssd_dt_bias, ssd_a_log, ssd_d, ssd_norm_g, cf_dw_w, cf_dw_b, cf_ln_g, cf_ln_b, w_out_even, w_in_odd, w_out_odd, ln_g, ln_b):
    bsz, seq, d = x_prompt.shape
    db = x_sample.shape[0]
    ssm_states = (state_ssm_l0, state_ssm_l2)
    sconv_states = (state_ssdconv_l0, state_ssdconv_l2)
    cconv_states = (state_cfconv_l0, state_cfconv_l2)
    k_caches = (cache_k_l1, cache_k_l3)
    v_caches = (cache_v_l1, cache_v_l3)
    ki_caches = (cache_kidx_l1, cache_kidx_l3)
    yp = x_prompt.reshape(bsz * seq, d)
    ys = x_sample.reshape(db, d)
    new_state = []
    for layer in range(_DEPTH):
        j = layer // 2
        g, b = ln_g[layer].reshape(1, d), ln_b[layer].reshape(1, d)
        if layer % 2 == 0:
            wts = _even_weights(w_in_even[j], ssd_conv_w[j], ssd_conv_b[j], ssd_dt_bias[j], ssd_a_log[j],
                                ssd_d[j], ssd_norm_g[j], cf_dw_w[j], cf_dw_b[j], cf_ln_g[j], cf_ln_b[j],
                                w_out_even[j])
            yp, ssm_p, sc_p, cc_p = _even_layer_prompt(yp, wts, g, b, bsz, seq)
            ys, ssm_s, sc_s, cc_s = _even_layer_sample(ys, ssm_states[j], sconv_states[j], cconv_states[j],
                                                       wts, g, b)
            new_state += [ssm_p, ssm_s, sc_p, sc_s, cc_p, cc_s]
        else:
            wts = _odd_weights(w_in_odd[j], w_out_odd[j])
            yp, k_p, v_p, ki_p = _odd_layer_prompt(yp, wts, g, b, bsz, seq)
            ys, k_s, v_s, ki_s = _odd_layer_sample(ys, k_caches[j], v_caches[j], ki_caches[j], page_table,
                                                   wts, g, b)
            new_state += [k_p, k_s, v_p, v_s, ki_p, ki_s]
    return (yp.reshape(bsz, seq, d), ys.reshape(db, 1, d), *new_state)
```

```python
import functools

import jax
import jax.numpy as jnp
from jax import lax
from jax.experimental import pallas as pl
from jax.experimental.pallas import tpu as pltpu

_BF = jnp.bfloat16
_F32 = jnp.float32
_HI = lax.Precision.HIGHEST

_SSD_HEADS = 16
_SSD_HEAD_DIM = 64
_SSD_GROUPS = 2
_SSD_STATE = 128
_SSD_CONV = 4
_CF_KERNEL = 31
_ATTN_HEADS = 16
_ATTN_KV_HEADS = 4
_ATTN_HEAD_DIM = 64
_IDX_HEADS = 8
_IDX_DIM = 64
_TOPK_MAX = 256
_PAGE = 128
_DEPTH = 4
_ALPHA = (2 * _DEPTH) ** 0.25
_EPS = 1e-5

_LANES = 128
_VMEM_LIMIT = 56 * 1024 * 1024
_NEG = -1e30
_MAX_BISECT = 200


def _dot(a, b):
    return jnp.dot(a, b, preferred_element_type=_F32)


def _dot_nt(a, b):
    return lax.dot_general(a, b, (((1,), (1,)), ((), ())), preferred_element_type=_F32)


def _dot_hi(a, b):
    return jnp.dot(a, b, precision=_HI, preferred_element_type=_F32)


def _dot_nt_hi(a, b):
    return lax.dot_general(a, b, (((1,), (1,)), ((), ())), precision=_HI,
                           preferred_element_type=_F32)


def _sigmoid(x):
    return 1.0 / (1.0 + jnp.exp(-x))


def _silu(x):
    return x * _sigmoid(x)


def _softplus(x):
    return jnp.maximum(x, 0.0) + jnp.log1p(jnp.exp(-jnp.abs(x)))


def _layer_norm(x, g, b):
    mu = jnp.mean(x, axis=-1, keepdims=True)
    xc = x - mu
    var = jnp.mean(xc * xc, axis=-1, keepdims=True)
    return xc * lax.rsqrt(var + _EPS) * g + b


def _cparams(*sem):
    return pltpu.CompilerParams(dimension_semantics=sem, vmem_limit_bytes=_VMEM_LIMIT)


def _const_spec(shape):
    return pl.BlockSpec(shape, lambda *_: (0,) * len(shape), pipeline_mode=pl.Buffered(1))


def _even_proj_kernel(x_ref, wza, wxbc, wdt, wglu, wzb, za_o, xbc_o, dt_o, glu_o, zb_o):
    xb = x_ref[...].astype(_BF)
    za_o[...] = _dot(xb, wza[...])
    xbc_o[...] = _dot(xb, wxbc[...])
    dt_o[...] = _dot(xb, wdt[...])
    glu_o[...] = _dot(xb, wglu[...])
    zb_o[...] = _dot(xb, wzb[...])


def _even_proj(x, ws):
    t, d = x.shape
    tm = min(256, t)
    widths = [w.shape[1] for w in ws]
    row = lambda n: pl.BlockSpec((tm, n), lambda i: (i, 0))
    return pl.pallas_call(
        _even_proj_kernel,
        grid=(t // tm,),
        in_specs=[row(d)] + [_const_spec(w.shape) for w in ws],
        out_specs=[row(n) for n in widths],
        out_shape=[jax.ShapeDtypeStruct((t, n), _F32) for n in widths],
        compiler_params=_cparams("arbitrary"),
        name="even_proj",
    )(x, *ws)


def _odd_proj_prompt_kernel(x_ref, wk, wv, wkw, wz, wq_t, wqi_t, wkw_t, wv_t,
                            k_o, v_o, ki_o, z_o, qt_o, qit_o, wit_o, kib_o, kg_o, vt_o):
    xb = x_ref[...].astype(_BF)
    k = _dot(xb, wk[...])
    k_o[...] = k
    v_o[...] = _dot(xb, wv[...])
    kw = _dot(xb, wkw[...])
    ki_o[...] = kw[:, :_IDX_DIM]
    kib_o[...] = kw[:, :_IDX_DIM].astype(_BF)
    z_o[...] = _dot(xb, wz[...])
    qt_o[...] = (_dot_nt(wq_t[...], xb) * (_ATTN_HEAD_DIM ** -0.5)).astype(_BF)
    qit_o[...] = _dot_nt(wqi_t[...], xb).astype(_BF)
    wit_o[...] = _dot_nt(wkw_t[...], xb)
    vt = _dot_nt(wv_t[...], xb)
    ones = jnp.ones((_ATTN_HEAD_DIM, xb.shape[0]), _BF)
    for g in range(_ATTN_KV_HEADS):
        cols = slice(g * _ATTN_HEAD_DIM, (g + 1) * _ATTN_HEAD_DIM)
        kg_o[g] = k[:, cols].astype(_BF)
        vt_o[g, 0:_ATTN_HEAD_DIM, :] = vt[cols, :].astype(_BF)
        vt_o[g, _ATTN_HEAD_DIM:2 * _ATTN_HEAD_DIM, :] = ones


def _odd_proj_prompt(x, ws, wts, bsz, seq, tkb):
    t, d = x.shape
    tm = tkb
    nb = seq // tm
    _, wk, wv, _, wkw, wz = ws
    kvw = wk.shape[1]
    aw, iw = wts[0].shape[0], wts[1].shape[0]
    row = lambda n: pl.BlockSpec((tm, n), lambda i: (i, 0))
    col = lambda n: pl.BlockSpec((n, tm), lambda i: (0, i))
    outs = [
        (jax.ShapeDtypeStruct((t, kvw), _F32), row(kvw)),
        (jax.ShapeDtypeStruct((t, kvw), _F32), row(kvw)),
        (jax.ShapeDtypeStruct((t, _IDX_DIM), _F32), row(_IDX_DIM)),
        (jax.ShapeDtypeStruct((t, wz.shape[1]), _F32), row(wz.shape[1])),
        (jax.ShapeDtypeStruct((aw, t), _BF), col(aw)),
        (jax.ShapeDtypeStruct((iw, t), _BF), col(iw)),
        (jax.ShapeDtypeStruct((_LANES, t), _F32), col(_LANES)),
        (jax.ShapeDtypeStruct((t, _IDX_DIM), _BF), row(_IDX_DIM)),
        (jax.ShapeDtypeStruct((bsz, _ATTN_KV_HEADS, seq, _ATTN_HEAD_DIM), _BF),
         pl.BlockSpec((None, _ATTN_KV_HEADS, tm, _ATTN_HEAD_DIM), lambda i: (i // nb, 0, i % nb, 0))),
        (jax.ShapeDtypeStruct((bsz, nb, _ATTN_KV_HEADS, 2 * _ATTN_HEAD_DIM, tm), _BF),
         pl.BlockSpec((None, None, _ATTN_KV_HEADS, 2 * _ATTN_HEAD_DIM, tm),
                      lambda i: (i // nb, i % nb, 0, 0, 0))),
    ]
    wins = (wk, wv, wkw, wz, *wts)
    return pl.pallas_call(
        _odd_proj_prompt_kernel,
        grid=(t // tm,),
        in_specs=[row(d)] + [_const_spec(w.shape) for w in wins],
        out_specs=[o[1] for o in outs],
        out_shape=[o[0] for o in outs],
        compiler_params=_cparams("arbitrary"),
        name="odd_proj_prompt",
    )(x, *wins)


def _odd_proj_sample_kernel(x_ref, wq, wk, wv, wqi, wkw, wz, q_o, k_o, v_o, qi_o, ki_o, wi_o, z_o):
    xb = x_ref[...].astype(_BF)
    q_o[...] = _dot(xb, wq[...]) * (_ATTN_HEAD_DIM ** -0.5)
    k_o[...] = _dot(xb, wk[...])
    v_o[...] = _dot(xb, wv[...])
    qi_o[...] = _dot(xb, wqi[...])
    kw = _dot(xb, wkw[...])
    ki_o[...] = kw[:, :_IDX_DIM]
    wi_o[...] = kw
    z_o[...] = _dot(xb, wz[...])


def _odd_proj_sample(x, ws):
    t, d = x.shape
    wq, wk, wv, wqi, wkw, wz = ws
    widths = [wq.shape[1], wk.shape[1], wv.shape[1], wqi.shape[1], _IDX_DIM, _LANES, wz.shape[1]]
    full = lambda n: pl.BlockSpec((t, n), lambda i: (0, 0))
    return pl.pallas_call(
        _odd_proj_sample_kernel,
        grid=(1,),
        in_specs=[full(d)] + [_const_spec(w.shape) for w in ws],
        out_specs=[full(n) for n in widths],
        out_shape=[jax.ShapeDtypeStruct((t, n), _F32) for n in widths],
        compiler_params=_cparams("arbitrary"),
        name="odd_proj_sample",
    )(x, *ws)


def _outproj_kernel(*refs, n_in, gated):
    x_ref = refs[0]
    a_refs = refs[1:1 + n_in]
    pos = 1 + n_in
    z_ref = refs[pos] if gated else None
    pos += 1 if gated else 0
    w_refs = refs[pos:pos + n_in]
    g_ref, b_ref, o_ref = refs[pos + n_in:pos + n_in + 3]
    acc = _ALPHA * x_ref[...]
    for a_ref, w_ref in zip(a_refs, w_refs):
        a = a_ref[...]
        if gated:
            a = a.astype(_F32) * _silu(z_ref[...])
        acc = acc + _dot(a.astype(_BF), w_ref[...])
    o_ref[...] = _layer_norm(acc, g_ref[...], b_ref[...])


def _outproj_ln(x, a_list, w_list, g, b, z=None):
    t, d = x.shape
    tm = min(512, t)
    row = lambda n: pl.BlockSpec((tm, n), lambda i: (i, 0))
    gated = z is not None
    ins = [x, *a_list] + ([z] if gated else []) + [*w_list, g, b]
    specs = ([row(d)] + [row(a.shape[1]) for a in a_list] + ([row(z.shape[1])] if gated else [])
             + [_const_spec(w.shape) for w in w_list] + [_const_spec(g.shape), _const_spec(b.shape)])
    return pl.pallas_call(
        functools.partial(_outproj_kernel, n_in=len(a_list), gated=gated),
        grid=(t // tm,),
        in_specs=specs,
        out_specs=row(d),
        out_shape=jax.ShapeDtypeStruct((t, d), _F32),
        compiler_params=_cparams("arbitrary"),
        name="outproj_ln",
    )(*ins)


def _group_rmsnorm_gate(y, za, ng):
    y = y * _silu(za)
    gw = y.shape[1] // _SSD_GROUPS
    parts = []
    for g in range(_SSD_GROUPS):
        yg = y[:, g * gw:(g + 1) * gw]
        ms = jnp.mean(yg * yg, axis=-1, keepdims=True)
        parts.append(yg * lax.rsqrt(ms + _EPS))
    return jnp.concatenate(parts, axis=-1) * ng


def _even_mix_kernel(za_ref, xbc_ref, dt_ref, glu_ref, zb_ref,
                     cw_ref, cb_ref, dtb_ref, alog_ref, d_ref, ng_ref,
                     fw_ref, fb_ref, fg_ref, fbeta_ref,
                     ya_o, yb_o, ssm_o, sc_o, cc_o,
                     xbuf, ubuf, state, ushift, cacc, *, q, nc):
    c = pl.program_id(1)
    inner = _SSD_HEADS * _SSD_HEAD_DIM
    gn = _SSD_GROUPS * _SSD_STATE
    xhalo = 8
    uhalo = 32

    @pl.when(c == 0)
    def _():
        xbuf[0:xhalo, :] = jnp.zeros((xhalo, xbuf.shape[1]), _F32)
        ubuf[0:uhalo, :] = jnp.zeros((uhalo, ubuf.shape[1]), _F32)
        state[...] = jnp.zeros(state.shape, _F32)

    xbuf[xhalo:xhalo + q, :] = xbc_ref[...]
    acc = jnp.broadcast_to(cb_ref[...], (q, xbuf.shape[1]))
    for k in range(_SSD_CONV):
        off = xhalo - (_SSD_CONV - 1) + k
        acc = acc + cw_ref[k:k + 1, :] * xbuf[off:off + q, :]
    xc = _silu(acc)
    xs = xc[:, :inner]
    bm = [xc[:, inner + g * _SSD_STATE: inner + (g + 1) * _SSD_STATE].astype(_BF) for g in range(_SSD_GROUPS)]
    cm = [xc[:, inner + gn + g * _SSD_STATE: inner + gn + (g + 1) * _SSD_STATE].astype(_BF)
          for g in range(_SSD_GROUPS)]

    @pl.when(c == nc - 1)
    def _():
        sc_o[...] = xbuf[xhalo + q - (_SSD_CONV - 1):xhalo + q, :]

    xbuf[0:xhalo, :] = xbuf[q:q + xhalo, :]

    lane = lax.broadcasted_iota(jnp.int32, (1, _LANES), 1)
    hmask = lane < _SSD_HEADS
    dt = jnp.where(hmask, _softplus(dt_ref[...] + dtb_ref[...]), 0.0)
    a = jnp.where(hmask, -jnp.exp(alog_ref[...]), 0.0)
    ri = lax.broadcasted_iota(jnp.int32, (q, q), 0)
    ci = lax.broadcasted_iota(jnp.int32, (q, q), 1)
    trilb = ri >= ci
    a_cum = _dot_hi(trilb.astype(_F32), dt * a)
    a_cum_t = a_cum.T
    dt_t = dt.T
    a_last_t = a_cum_t[:, q - 1:q]
    w_t = dt_t * jnp.exp(a_last_t - a_cum_t)
    ea = jnp.exp(a_cum)
    cd_t = jnp.exp(a_last_t)
    lo_lane = lane < _SSD_HEAD_DIM
    lo_sub = lax.broadcasted_iota(jnp.int32, (_LANES, 1), 0) < _SSD_HEAD_DIM

    cb = [_dot_nt(cm[g], bm[g]) for g in range(_SSD_GROUPS)]
    heads_per_group = _SSD_HEADS // _SSD_GROUPS
    ys = []
    for j in range(_SSD_HEADS // 2):
        h0, h1 = 2 * j, 2 * j + 1
        g = h0 // heads_per_group
        sl = slice(j * _LANES, (j + 1) * _LANES)
        xs2 = xs[:, sl]
        x2 = (xs2 * jnp.where(lo_lane, dt[:, h0:h0 + 1], dt[:, h1:h1 + 1])).astype(_BF)
        l0 = jnp.where(trilb, jnp.exp(a_cum[:, h0:h0 + 1] - a_cum_t[h0:h0 + 1, :]), 0.0)
        l1 = jnp.where(trilb, jnp.exp(a_cum[:, h1:h1 + 1] - a_cum_t[h1:h1 + 1, :]), 0.0)
        y0 = _dot((cb[g] * l0).astype(_BF), x2)
        y1 = _dot((cb[g] * l1).astype(_BF), x2)
        sp = state[sl, :]
        yoff = _dot_nt(cm[g], sp.astype(_BF)) * jnp.where(lo_lane, ea[:, h0:h0 + 1], ea[:, h1:h1 + 1])
        w2 = jnp.where(lo_sub, w_t[h0:h0 + 1, :], w_t[h1:h1 + 1, :])
        s_chunk = _dot((xs2.T * w2).astype(_BF), bm[g])
        cd2 = jnp.where(lo_sub, cd_t[h0:h0 + 1, :], cd_t[h1:h1 + 1, :])
        state[sl, :] = sp * cd2 + s_chunk
        ys.append(jnp.where(lo_lane, y0, y1) + yoff + d_ref[:, sl] * xs2)
    y = jnp.concatenate(ys, axis=-1)
    ya_o[...] = _group_rmsnorm_gate(y, za_ref[...], ng_ref[...]).astype(_BF)

    @pl.when(c == nc - 1)
    def _():
        ssm_o[...] = state[...]

    cf = ubuf.shape[1]
    u = glu_ref[:, :cf] * _sigmoid(glu_ref[:, cf:])
    ubuf[uhalo:uhalo + q, :] = u
    back = 8 * ((_CF_KERNEL - 1) // 8)
    for r in range(1, 8):
        ushift[r - 1] = ubuf[uhalo - back - r:uhalo + q - r, :]
    for cb in range(cf // _LANES):
        cs = slice(cb * _LANES, (cb + 1) * _LANES)
        acc = jnp.broadcast_to(fb_ref[:, cs], (q, _LANES))
        for j in range(_CF_KERNEL):
            r, a = j % 8, j // 8
            k = _CF_KERNEL - 1 - j
            if r == 0:
                win = ubuf[uhalo - 8 * a:uhalo - 8 * a + q, cs]
            else:
                win = ushift[r - 1, back - 8 * a:back - 8 * a + q, cs]
            acc = acc + fw_ref[k:k + 1, cs] * win
        cacc[:, cs] = acc
    v = _silu(_layer_norm(cacc[...], fg_ref[...], fbeta_ref[...]))
    yb_o[...] = (v * _silu(zb_ref[...])).astype(_BF)

    @pl.when(c == nc - 1)
    def _():
        cc_o[...] = ubuf[uhalo + q - (_CF_KERNEL - 1):uhalo + q, :]

    ubuf[0:uhalo, :] = ubuf[q:q + uhalo, :]


def _even_mix_prompt(za, xbc, dt, glu, zb, params, bsz, seq):
    q = 128
    nc = seq // q
    inner = _SSD_HEADS * _SSD_HEAD_DIM
    conv_dim = xbc.shape[1]
    cf = zb.shape[1]
    row = lambda n: pl.BlockSpec((q, n), lambda b, c: (b * nc + c, 0))
    per_b = lambda r, n: pl.BlockSpec((None, r, n), lambda b, c: (b, 0, 0))
    t = bsz * seq
    return pl.pallas_call(
        functools.partial(_even_mix_kernel, q=q, nc=nc),
        grid=(bsz, nc),
        in_specs=[row(inner), row(conv_dim), row(_LANES), row(2 * cf), row(cf)]
                 + [_const_spec(p.shape) for p in params],
        out_specs=[row(inner), row(cf), per_b(inner, _SSD_STATE),
                   per_b(_SSD_CONV - 1, conv_dim), per_b(_CF_KERNEL - 1, cf)],
        out_shape=[jax.ShapeDtypeStruct((t, inner), _BF), jax.ShapeDtypeStruct((t, cf), _BF),
                   jax.ShapeDtypeStruct((bsz, inner, _SSD_STATE), _F32),
                   jax.ShapeDtypeStruct((bsz, _SSD_CONV - 1, conv_dim), _F32),
                   jax.ShapeDtypeStruct((bsz, _CF_KERNEL - 1, cf), _F32)],
        scratch_shapes=[pltpu.VMEM((8 + q, conv_dim), _F32), pltpu.VMEM((32 + q, cf), _F32),
                        pltpu.VMEM((inner, _SSD_STATE), _F32),
                        pltpu.VMEM((7, q + 8 * ((_CF_KERNEL - 1) // 8), cf), _F32),
                        pltpu.VMEM((q, cf), _F32)],
        compiler_params=_cparams("arbitrary", "arbitrary"),
        name="even_mix_prompt",
    )(za, xbc, dt, glu, zb, *params)


def _sample_even_pre_kernel(xbc_ref, dt_ref, glu_ref, zb_ref, sctx_ref, cctx_ref,
                            cw_ref, cb_ref, dtb_ref, alog_ref, fw_ref, fb_ref, fg_ref, fbeta_ref,
                            xs_o, bm_o, cm_o, dect_o, xdtt_o, yb_o, sctx_o, cctx_o, pad):
    db = xbc_ref.shape[0]
    inner = _SSD_HEADS * _SSD_HEAD_DIM
    gn = _SSD_GROUPS * _SSD_STATE
    xbc = xbc_ref[...]
    acc = cb_ref[...] + cw_ref[_SSD_CONV - 1:_SSD_CONV, :] * xbc
    for k in range(_SSD_CONV - 1):
        acc = acc + cw_ref[k:k + 1, :] * sctx_ref[k]
    xc = _silu(acc)
    xs = xc[:, :inner]
    xs_o[...] = xs
    bm_o[...] = xc[:, inner:inner + gn]
    cm_o[...] = xc[:, inner + gn:inner + 2 * gn]
    for k in range(_SSD_CONV - 2):
        sctx_o[k] = sctx_ref[k + 1]
    sctx_o[_SSD_CONV - 2] = xbc

    lane = lax.broadcasted_iota(jnp.int32, (1, _LANES), 1)
    hmask = lane < _SSD_HEADS
    dt = jnp.where(hmask, _softplus(dt_ref[...] + dtb_ref[...]), 0.0)
    a = jnp.where(hmask, -jnp.exp(alog_ref[...]), 0.0)
    dec = jnp.where(hmask, jnp.exp(dt * a), 0.0)
    er = lax.broadcasted_iota(jnp.int32, (_LANES, inner), 0)
    ec = lax.broadcasted_iota(jnp.int32, (_LANES, inner), 1)
    expand = jnp.where((ec // _SSD_HEAD_DIM) == er, 1.0, 0.0)
    dec_x = _dot_hi(dec, expand)
    xdt = xs * _dot_hi(dt, expand)
    for src, dst in ((dec_x, dect_o), (xdt, xdtt_o)):
        pad[...] = jnp.zeros(pad.shape, _F32)
        pad[0:db, :] = src
        for j in range(inner // _LANES):
            dst[j * _LANES:(j + 1) * _LANES, :] = pad[:, j * _LANES:(j + 1) * _LANES].T

    cf = zb_ref.shape[1]
    u = glu_ref[:, :cf] * _sigmoid(glu_ref[:, cf:])
    acc = fb_ref[...] + fw_ref[_CF_KERNEL - 1:_CF_KERNEL, :] * u
    for k in range(_CF_KERNEL - 1):
        acc = acc + fw_ref[k:k + 1, :] * cctx_ref[k]
    v = _silu(_layer_norm(acc, fg_ref[...], fbeta_ref[...]))
    yb_o[...] = (v * _silu(zb_ref[...])).astype(_BF)
    for k in range(_CF_KERNEL - 2):
        cctx_o[k] = cctx_ref[k + 1]
    cctx_o[_CF_KERNEL - 2] = u


def _sample_even_pre(xbc, dt, glu, zb, sctx_t, cctx_t, params):
    db = xbc.shape[0]
    inner = _SSD_HEADS * _SSD_HEAD_DIM
    gn = _SSD_GROUPS * _SSD_STATE
    cf = zb.shape[1]
    ins = [xbc, dt, glu, zb, sctx_t, cctx_t, *params]
    full = lambda s: pl.BlockSpec(s, lambda i: (0,) * len(s))
    out_shapes = [(db, inner), (db, gn), (db, gn), (inner, _LANES), (inner, _LANES), (db, cf),
                  sctx_t.shape, cctx_t.shape]
    out_dtypes = [_F32, _F32, _F32, _F32, _F32, _BF, _F32, _F32]
    return pl.pallas_call(
        _sample_even_pre_kernel,
        grid=(1,),
        in_specs=[full(a.shape) for a in ins],
        out_specs=[full(s) for s in out_shapes],
        out_shape=[jax.ShapeDtypeStruct(s, d) for s, d in zip(out_shapes, out_dtypes)],
        scratch_shapes=[pltpu.VMEM((_LANES, inner), _F32)],
        compiler_params=_cparams("arbitrary"),
        name="sample_even_pre",
    )(*ins)


def _sample_even_rec_kernel(st_ref, dect_ref, xdtt_ref, bm_ref, cm_ref, xs_ref, za_ref, d_ref, ng_ref,
                            st_o, ya_o):
    b = pl.program_id(0)
    inner = st_ref.shape[0]
    half = inner // _SSD_GROUPS
    sel = jnp.where(lax.broadcasted_iota(jnp.int32, (_LANES, _LANES), 0) == b, 1.0, 0.0)
    dec = _dot_hi(dect_ref[...], sel)
    xd = _dot_hi(xdtt_ref[...], sel)
    grp0 = lax.broadcasted_iota(jnp.int32, (inner, 1), 0) < half
    brow = jnp.where(grp0, bm_ref[:, :_SSD_STATE], bm_ref[:, _SSD_STATE:])
    s_new = st_ref[...] * dec + xd * brow
    st_o[...] = s_new
    r8 = lax.broadcasted_iota(jnp.int32, (8, 1), 0)
    c8 = jnp.where(r8 == 0, cm_ref[:, :_SSD_STATE], jnp.where(r8 == 1, cm_ref[:, _SSD_STATE:], 0.0))
    yr = _dot_nt_hi(c8, s_new)
    lane = lax.broadcasted_iota(jnp.int32, (1, inner), 1)
    xs = xs_ref[...]
    y = jnp.where(lane < half, yr[0:1, :], yr[1:2, :]) + d_ref[...] * xs
    ya_o[...] = _group_rmsnorm_gate(y, za_ref[...], ng_ref[...]).astype(_BF)


def _sample_even_rec(state, dect, xdtt, bm, cm, xs, za, d_x, ng):
    db, inner, n = state.shape
    per_b = lambda s: pl.BlockSpec((None,) + s, lambda b: (b, 0, 0))
    gn = bm.shape[1]
    r3 = lambda a: a.reshape(db, 1, a.shape[1])
    return pl.pallas_call(
        _sample_even_rec_kernel,
        grid=(db,),
        in_specs=[per_b((inner, n)), _const_spec(dect.shape), _const_spec(xdtt.shape),
                  per_b((1, gn)), per_b((1, gn)), per_b((1, inner)), per_b((1, inner)),
                  _const_spec(d_x.shape), _const_spec(ng.shape)],
        out_specs=[per_b((inner, n)), per_b((1, inner))],
        out_shape=[jax.ShapeDtypeStruct((db, inner, n), _F32), jax.ShapeDtypeStruct((db, 1, inner), _BF)],
        compiler_params=_cparams("arbitrary"),
        name="sample_even_rec",
    )(state, dect, xdtt, r3(bm), r3(cm), r3(xs), r3(za), d_x, ng)


def _bisect_threshold(count_ge, rmin, rmax, n_adm, c_ge0, c_gt0, k):
    nonneg = c_ge0 >= k
    lo0 = jnp.where(n_adm <= k, rmin, jnp.where(nonneg, 0.0, rmin))
    c0 = jnp.where(n_adm <= k, n_adm, jnp.where(nonneg, c_ge0, n_adm))
    above = jnp.maximum((rmax - lo0) * 2.0 ** -10, rmax * 2.0 ** -20)
    hi0 = jnp.where(c_gt0 >= k, rmax + above, 0.0)

    def midpoint(lo, hi):
        return lo + (hi - lo) * 0.5

    def open_rows(lo, hi, c_lo, mid):
        return jnp.where(c_lo > k, jnp.where(mid > lo, jnp.where(mid < hi, 1.0, 0.0), 0.0), 0.0)

    def cond(st):
        it, _, _, _, _, todo = st
        return jnp.logical_and(it < _MAX_BISECT, jnp.max(todo) > 0.5)

    def body(st):
        it, lo, hi, c_lo, mid, _ = st
        c = count_ge(mid)
        ge = c >= k
        lo_n = jnp.where(ge, mid, lo)
        hi_n = jnp.where(ge, hi, mid)
        c_n = jnp.where(ge, c, c_lo)
        mid_n = midpoint(lo_n, hi_n)
        return it + 1, lo_n, hi_n, c_n, mid_n, open_rows(lo_n, hi_n, c_n, mid_n)

    mid0 = midpoint(lo0, hi0)
    st = (jnp.int32(0), lo0, hi0, c0, mid0, open_rows(lo0, hi0, c0, mid0))
    _, lo, _, c_lo, _, _ = lax.while_loop(cond, body, st)
    return lo, c_lo


def _fold8(x, op):
    n, c = x.shape
    if n % 64 == 0 and n > 64:
        x = op(x.reshape(n // 64, 64, c), axis=0)
    return op(x.reshape(x.shape[0] // 8, 8, c), axis=0)


def _attn_prompt_kernel(qit_ref, wit_ref, qt_ref, z_ref, ki_ref, kg_ref, vt_ref, o_ref,
                        sc_ref, qis_ref, qg_ref, acc_ref, m_ref, al_ref, s_ref, p_ref, ob_ref,
                        *, tq, tkb, topk):
    i = pl.program_id(1)
    nkb = (i * tq + tq + tkb - 1) // tkb
    qpos = i * tq + lax.broadcasted_iota(jnp.int32, (1, tq), 1)
    kf = jnp.float32(topk)
    inf = jnp.float32(jnp.inf)
    rep = _ATTN_HEADS // _ATTN_KV_HEADS
    sk = _LANES

    qit = qit_ref[...]
    for h in range(_IDX_HEADS):
        qis_ref[:, h * tq:(h + 1) * tq] = qit[h * _IDX_DIM:(h + 1) * _IDX_DIM, :]
    w = wit_ref[...] * (_IDX_HEADS ** -0.5 * _IDX_DIM ** -0.5)
    wrs = [w[_IDX_DIM + h:_IDX_DIM + h + 1, :] for h in range(_IDX_HEADS)]

    def score_body(kb, carry):
        rmin, rmax, cge0, cgt0 = carry
        for jb in range(tkb // sk):
            start = pl.multiple_of(kb * tkb + jb * sk, sk)
            kblk = ki_ref[pl.ds(start, sk), :]
            acc = jnp.zeros((sk, tq), _F32)
            for h in range(_IDX_HEADS):
                s = _dot(kblk, qis_ref[:, h * tq:(h + 1) * tq])
                acc = acc + jnp.maximum(s, 0.0) * wrs[h]
            kpos = start + lax.broadcasted_iota(jnp.int32, (sk, 1), 0)
            adm = kpos <= qpos
            sc = jnp.where(adm, acc, -inf)
            sc_ref[kb, jb * sk:(jb + 1) * sk, :] = sc
            rmax = jnp.maximum(rmax, _fold8(sc, jnp.max))
            rmin = jnp.minimum(rmin, _fold8(jnp.where(adm, acc, inf), jnp.min))
            cge0 = cge0 + _fold8(jnp.where(sc >= 0.0, 1.0, 0.0), jnp.sum)
            cgt0 = cgt0 + _fold8(jnp.where(sc > 0.0, 1.0, 0.0), jnp.sum)
        return rmin, rmax, cge0, cgt0

    zeros8 = jnp.zeros((8, tq), _F32)
    rmin, rmax, cge0, cgt0 = lax.fori_loop(
        0, nkb, score_body,
        (jnp.full((8, tq), inf, _F32), jnp.full((8, tq), -inf, _F32), zeros8, zeros8))
    rmin = jnp.min(rmin, axis=0, keepdims=True)
    rmax = jnp.max(rmax, axis=0, keepdims=True)
    cge0 = jnp.sum(cge0, axis=0, keepdims=True)
    cgt0 = jnp.sum(cgt0, axis=0, keepdims=True)

    def count_ge(x):
        def cbody(kb, cnt):
            return cnt + _fold8(jnp.where(sc_ref[kb] >= x, 1.0, 0.0), jnp.sum)
        cnt = lax.fori_loop(0, nkb, cbody, zeros8)
        return jnp.sum(cnt, axis=0, keepdims=True)

    n_adm = (qpos + 1).astype(_F32)
    lo, c_lo = _bisect_threshold(count_ge, rmin, rmax, n_adm, cge0, cgt0, kf)

    @pl.when(jnp.max(c_lo) > kf)
    def _():
        def gbody(kb, cnt):
            return cnt + _fold8(jnp.where(sc_ref[kb] > lo, 1.0, 0.0), jnp.sum)
        need = kf - jnp.sum(lax.fori_loop(0, nkb, gbody, zeros8), axis=0, keepdims=True)
        tri = jnp.where(lax.broadcasted_iota(jnp.int32, (tkb, tkb), 0)
                        >= lax.broadcasted_iota(jnp.int32, (tkb, tkb), 1), 1.0, 0.0).astype(_BF)

        def tbody(kb, carry):
            s = sc_ref[kb]
            eq = s == lo
            pre = _dot(tri, jnp.where(eq, 1.0, 0.0).astype(_BF))
            drop = jnp.logical_and(eq, (carry + pre) > need)
            sc_ref[kb] = jnp.where(drop, -inf, s)
            return carry + pre[tkb - 1:tkb, :]
        lax.fori_loop(0, nkb, tbody, jnp.zeros((1, tq), _F32))

    qt = qt_ref[...]
    for h in range(_ATTN_HEADS):
        g, r = divmod(h, rep)
        qg_ref[g, :, r * tq:(r + 1) * tq] = qt[h * _ATTN_HEAD_DIM:(h + 1) * _ATTN_HEAD_DIM, :]
    m_ref[...] = jnp.full(m_ref.shape, _NEG, _F32)
    acc_ref[...] = jnp.zeros(acc_ref.shape, _F32)

    def bias_body(kb, carry):
        sc_ref[kb] = jnp.where(sc_ref[kb] >= lo, 0.0, _NEG)
        return carry
    lax.fori_loop(0, nkb, bias_body, 0)

    def attn_body(kb, carry):
        base = pl.multiple_of(kb * tkb, tkb)
        for g in range(_ATTN_KV_HEADS):
            mxs = [None] * rep
            for jb in range(tkb // _LANES):
                rows = slice(jb * _LANES, (jb + 1) * _LANES)
                kblk = kg_ref[g, pl.ds(base + jb * _LANES, _LANES), :]
                bias = sc_ref[kb, rows, :]
                for r in range(rep):
                    cols = slice(r * tq, (r + 1) * tq)
                    s = _dot(kblk, qg_ref[g, :, cols]) + bias
                    s_ref[g, rows, cols] = s
                    part = _fold8(s, jnp.max)
                    mxs[r] = part if mxs[r] is None else jnp.maximum(mxs[r], part)
            mx = jnp.concatenate([jnp.max(m8, axis=0, keepdims=True) for m8 in mxs], axis=1)
            m_old = m_ref[g]
            m_new = jnp.maximum(m_old, mx)
            al_ref[g] = jnp.exp(m_old - m_new)
            m_ref[g] = m_new
        for g in range(_ATTN_KV_HEADS):
            for r in range(rep):
                cols = slice(r * tq, (r + 1) * tq)
                m_new = m_ref[g, :, cols]
                for jb in range(tkb // _LANES):
                    rows = slice(jb * _LANES, (jb + 1) * _LANES)
                    p_ref[g, rows, cols] = jnp.exp(s_ref[g, rows, cols] - m_new).astype(_BF)
                acc_ref[g, :, cols] = (acc_ref[g, :, cols] * al_ref[g, :, cols]
                                       + _dot(vt_ref[kb, g], p_ref[g, :, cols]))
        return carry

    lax.fori_loop(0, nkb, attn_body, 0)
    for h in range(_ATTN_HEADS):
        g, r = divmod(h, rep)
        acc = acc_ref[g, :, r * tq:(r + 1) * tq].T
        o_h = acc / pltpu.roll(acc, _ATTN_HEAD_DIM, 1)
        ob_ref[:, h * _ATTN_HEAD_DIM:(h + 1) * _ATTN_HEAD_DIM] = o_h[:, :_ATTN_HEAD_DIM]
    o_ref[...] = (ob_ref[...] * _silu(z_ref[...])).astype(_BF)


def _attn_prompt(qit, wit, qt, z, ki, kg, vt, bsz, seq, tkb):
    tq = 256
    nq = seq // tq
    nkbt = seq // tkb
    t = bsz * seq
    topk = min(_TOPK_MAX, seq // 4)
    width = z.shape[1]
    rep = _ATTN_HEADS // _ATTN_KV_HEADS
    row = lambda n: pl.BlockSpec((tq, n), lambda b, i: (b * nq + i, 0))
    col = lambda n: pl.BlockSpec((n, tq), lambda b, i: (0, b * nq + i))
    once = dict(pipeline_mode=pl.Buffered(1))
    return pl.pallas_call(
        functools.partial(_attn_prompt_kernel, tq=tq, tkb=tkb, topk=topk),
        grid=(bsz, nq),
        in_specs=[col(qit.shape[0]), col(_LANES), col(qt.shape[0]), row(width),
                  pl.BlockSpec((seq, _IDX_DIM), lambda b, i: (b, 0), **once),
                  pl.BlockSpec((None,) + kg.shape[1:], lambda b, i: (b, 0, 0, 0), **once),
                  pl.BlockSpec((None,) + vt.shape[1:], lambda b, i: (b, 0, 0, 0, 0), **once)],
        out_specs=row(width),
        out_shape=jax.ShapeDtypeStruct((t, width), _BF),
        scratch_shapes=[pltpu.VMEM((nkbt, tkb, tq), _F32),
                        pltpu.VMEM((_IDX_DIM, _IDX_HEADS * tq), _BF),
                        pltpu.VMEM((_ATTN_KV_HEADS, _ATTN_HEAD_DIM, rep * tq), _BF),
                        pltpu.VMEM((_ATTN_KV_HEADS, 2 * _ATTN_HEAD_DIM, rep * tq), _F32),
                        pltpu.VMEM((_ATTN_KV_HEADS, 1, rep * tq), _F32),
                        pltpu.VMEM((_ATTN_KV_HEADS, 1, rep * tq), _F32),
                        pltpu.VMEM((_ATTN_KV_HEADS, tkb, rep * tq), _F32),
                        pltpu.VMEM((_ATTN_KV_HEADS, tkb, rep * tq), _BF),
                        pltpu.VMEM((tq, width), _F32)],
        compiler_params=_cparams("arbitrary", "arbitrary"),
        name="attn_prompt",
    )(qit, wit, qt, z, ki, kg, vt)


def _attn_sample_kernel(pt_ref, qi_ref, w_ref, kin_ref, q_ref, kn_ref, vn_ref,
                        ckit_ref, ckt_ref, cvt_ref, o_ref,
                        kibuf, kbuf, vbuf, sc_ref, sn_ref, lg_ref, sem, *, n_pages, topk):
    b = pl.program_id(0)
    past = n_pages * _PAGE
    kf = jnp.float32(topk)
    inf = jnp.float32(jnp.inf)
    caches = ((ckit_ref, kibuf), (ckt_ref, kbuf), (cvt_ref, vbuf))
    unroll = min(8, n_pages)

    def page_copy(which, page):
        cache_ref, buf = caches[which]
        return pltpu.make_async_copy(cache_ref.at[pt_ref[b, page]], buf.at[page], sem.at[which])

    def start_all(which):
        lax.fori_loop(0, n_pages, lambda p, c: (page_copy(which, p).start(), c)[1], 0)

    def wait_all(which):
        lax.fori_loop(0, n_pages, lambda p, c: (page_copy(which, p).wait(), c)[1], 0)

    for which in range(3):
        start_all(which)

    def total(x):
        return jnp.sum(jnp.sum(x, axis=1, keepdims=True), axis=0, keepdims=True)

    qi8 = qi_ref[...].astype(_BF)
    w8 = w_ref[...] * (_IDX_HEADS ** -0.5 * _IDX_DIM ** -0.5)
    wait_all(0)

    def score_body(p, c):
        s8 = _dot(qi8, kibuf[p].astype(_BF))
        sc_ref[pl.ds(p, 1), :] = jnp.sum(jnp.maximum(s8, 0.0) * w8, axis=0, keepdims=True)
        return c
    lax.fori_loop(0, n_pages, score_body, 0, unroll=unroll)
    kin = kin_ref[...].astype(_BF).astype(_F32)
    sn8 = jnp.sum(qi8.astype(_F32) * kin, axis=1, keepdims=True)
    sn_ref[...] = jnp.sum(jnp.maximum(sn8, 0.0) * w8, axis=0, keepdims=True)
    snew = sn_ref[...]

    if past + 1 > topk:
        sc = sc_ref[...]
        rmax = jnp.maximum(jnp.max(jnp.max(sc, axis=1, keepdims=True), axis=0, keepdims=True), snew)
        rmin = jnp.minimum(jnp.min(jnp.min(sc, axis=1, keepdims=True), axis=0, keepdims=True), snew)
        one = lambda cnd: jnp.where(cnd, 1.0, 0.0)

        def count_ge(x):
            return total(one(sc_ref[...] >= x)) + one(snew >= x)

        lo, c_lo = _bisect_threshold(count_ge, rmin, rmax, jnp.full((1, 1), past + 1.0, _F32),
                                     count_ge(jnp.zeros((1, 1), _F32)),
                                     total(one(sc > 0.0)) + one(snew > 0.0), kf)

        @pl.when(jnp.max(c_lo) > kf)
        def _():
            need = kf - (total(one(sc_ref[...] > lo)) + one(snew > lo))
            tri = jnp.where(lax.broadcasted_iota(jnp.int32, (_LANES, _LANES), 0)
                            <= lax.broadcasted_iota(jnp.int32, (_LANES, _LANES), 1), 1.0, 0.0).astype(_BF)

            def tbody(p, carry):
                s = sc_ref[pl.ds(p, 1), :]
                eq = s == lo
                eq8 = jnp.broadcast_to(one(eq), (8, _LANES)).astype(_BF)
                pre = _dot(eq8, tri)[0:1, :]
                drop = jnp.logical_and(eq, (carry + pre) > need)
                sc_ref[pl.ds(p, 1), :] = jnp.where(drop, -inf, s)
                return carry + pre[:, _LANES - 1:_LANES]
            carry = lax.fori_loop(0, n_pages, tbody, jnp.zeros((1, 1), _F32))
            drop_new = jnp.logical_and(snew == lo, (carry + 1.0) > need)
            sn_ref[...] = jnp.where(drop_new, -inf, snew)
    else:
        lo = jnp.full((1, 1), -inf, _F32)

    kvw = _ATTN_KV_HEADS * _ATTN_HEAD_DIM
    rep = _ATTN_HEADS // _ATTN_KV_HEADS
    rr = lax.broadcasted_iota(jnp.int32, (_ATTN_HEAD_DIM, kvw), 0)
    rc = lax.broadcasted_iota(jnp.int32, (_ATTN_HEAD_DIM, kvw), 1)
    spread = jnp.where((rc % _ATTN_HEAD_DIM) == rr, 1.0, 0.0)
    hr = lax.broadcasted_iota(jnp.int32, (_ATTN_HEADS, kvw), 0)
    hc = lax.broadcasted_iota(jnp.int32, (_ATTN_HEADS, kvw), 1)
    own = (hc // _ATTN_HEAD_DIM) == (hr // rep)
    qbd = jnp.where(own, _dot_hi(q_ref[...], spread), 0.0).astype(_BF)
    kn = kn_ref[...].astype(_BF).astype(_F32)
    lnew = (jnp.sum(qbd.astype(_F32) * kn, axis=1, keepdims=True)
            + jnp.where(sn_ref[...] >= lo, 0.0, _NEG))
    wait_all(1)

    def logit_body(p, mx):
        lg = _dot(qbd, kbuf[p].astype(_BF)) + jnp.where(sc_ref[pl.ds(p, 1), :] >= lo, 0.0, _NEG)
        lg_ref[p] = lg
        return jnp.maximum(mx, lg)
    mx = lax.fori_loop(0, n_pages, logit_body, jnp.full((_ATTN_HEADS, _LANES), _NEG, _F32), unroll=unroll)
    m = jnp.maximum(jnp.max(mx, axis=1, keepdims=True), lnew)
    pn = jnp.exp(lnew - m)
    vn = vn_ref[...].astype(_BF).astype(_F32)
    wait_all(2)

    def pv_body(p, carry):
        psum, out = carry
        pr = jnp.exp(lg_ref[p] - m)
        return psum + pr, out + _dot_nt(pr.astype(_BF), vbuf[p].astype(_BF))
    psum, out = lax.fori_loop(0, n_pages, pv_body,
                              (jnp.zeros((_ATTN_HEADS, _LANES), _F32), pn.astype(_BF).astype(_F32) * vn),
                              unroll=unroll)
    den = jnp.sum(psum, axis=1, keepdims=True) + pn
    out = jnp.where(own, out / den, 0.0)
    gather = jnp.where((lax.broadcasted_iota(jnp.int32, (kvw, _ATTN_HEAD_DIM), 0) % _ATTN_HEAD_DIM)
                       == lax.broadcasted_iota(jnp.int32, (kvw, _ATTN_HEAD_DIM), 1), 1.0, 0.0)
    o_ref[...] = _dot_hi(out, gather)


def _attn_sample(page_table, qi, wi, ki_new, q, k_new, v_new, cache_ki, cache_k, cache_v):
    db, n_pages = page_table.shape
    past = n_pages * _PAGE
    topk = min(_TOPK_MAX, (past + 1) // 4)
    kvw = _ATTN_KV_HEADS * _ATTN_HEAD_DIM
    n_pool = cache_k.shape[0]
    per_b = lambda s: pl.BlockSpec((None,) + s, lambda b, pt: (b,) + (0,) * len(s))
    anyspec = pl.BlockSpec(memory_space=pl.ANY)
    kv_t = lambda c: jnp.transpose(c, (0, 2, 3, 1)).reshape(n_pool, kvw, _PAGE)
    ins = [qi.reshape(db, _IDX_HEADS, _IDX_DIM),
           wi[:, _IDX_DIM:_IDX_DIM + _IDX_HEADS].reshape(db, _IDX_HEADS, 1),
           ki_new.reshape(db, 1, _IDX_DIM),
           q.reshape(db, _ATTN_HEADS, _ATTN_HEAD_DIM),
           k_new.reshape(db, 1, kvw), v_new.reshape(db, 1, kvw),
           jnp.transpose(cache_ki, (0, 2, 1)), kv_t(cache_k), kv_t(cache_v)]
    grid_spec = pltpu.PrefetchScalarGridSpec(
        num_scalar_prefetch=1,
        grid=(db,),
        in_specs=[per_b((_IDX_HEADS, _IDX_DIM)), per_b((_IDX_HEADS, 1)), per_b((1, _IDX_DIM)),
                  per_b((_ATTN_HEADS, _ATTN_HEAD_DIM)), per_b((1, kvw)), per_b((1, kvw)),
                  anyspec, anyspec, anyspec],
        out_specs=per_b((_ATTN_HEADS, _ATTN_HEAD_DIM)),
        scratch_shapes=[pltpu.VMEM((n_pages, _IDX_DIM, _PAGE), _F32),
                        pltpu.VMEM((n_pages, kvw, _PAGE), _F32),
                        pltpu.VMEM((n_pages, kvw, _PAGE), _F32),
                        pltpu.VMEM((n_pages, _PAGE), _F32), pltpu.VMEM((1, 1), _F32),
                        pltpu.VMEM((n_pages, _ATTN_HEADS, _PAGE), _F32),
                        pltpu.SemaphoreType.DMA((3,))],
    )
    o = pl.pallas_call(
        functools.partial(_attn_sample_kernel, n_pages=n_pages, topk=topk),
        grid_spec=grid_spec,
        out_shape=jax.ShapeDtypeStruct((db, _ATTN_HEADS, _ATTN_HEAD_DIM), _F32),
        compiler_params=_cparams("arbitrary"),
        name="attn_sample",
    )(page_table, *ins)
    return o.reshape(db, _ATTN_HEADS * _ATTN_HEAD_DIM)


def _pad_lanes(a, n=_LANES):
    return jnp.pad(a, [(0, 0)] * (a.ndim - 1) + [(0, n - a.shape[-1])])


def _even_weights(w_in, conv_w, conv_b, dt_bias, a_log, d_skip, norm_g, cf_w, cf_b, cf_g, cf_beta, w_out):
    inner = _SSD_HEADS * _SSD_HEAD_DIM
    conv_dim = conv_w.shape[1]
    cf = cf_w.shape[1]
    o1, o2, o3, o4 = inner, inner + conv_dim, inner + conv_dim + _SSD_HEADS, inner + conv_dim + _SSD_HEADS + 2 * cf
    wb = w_in.astype(_BF)
    proj = (wb[:, :o1], wb[:, o1:o2], _pad_lanes(wb[:, o2:o3]), wb[:, o3:o4], wb[:, o4:])
    row = lambda v: v.reshape(1, -1)
    ssd = (conv_w, row(conv_b), _pad_lanes(row(dt_bias)), _pad_lanes(row(a_log)))
    d_x = row(jnp.repeat(d_skip, _SSD_HEAD_DIM))
    cfp = (cf_w, row(cf_b), row(cf_g), row(cf_beta))
    wo = w_out.astype(_BF)
    return proj, ssd, d_x, row(norm_g), cfp, (wo[:inner], wo[inner:])


def _even_layer_prompt(x, wts, ln_g, ln_b, bsz, seq):
    proj, ssd, d_x, ng, cfp, wo = wts
    za, xbc, dt, glu, zb = _even_proj(x, proj)
    ya, yb, ssm, sc, cc = _even_mix_prompt(za, xbc, dt, glu, zb, (*ssd, d_x, ng, *cfp), bsz, seq)
    x_new = _outproj_ln(x, [ya, yb], wo, ln_g, ln_b)
    return x_new, ssm.reshape(bsz, _SSD_HEADS, _SSD_HEAD_DIM, _SSD_STATE), sc, cc


def _even_layer_sample(x, st_ssm, st_sconv, st_cconv, wts, ln_g, ln_b):
    proj, ssd, d_x, ng, cfp, wo = wts
    db = x.shape[0]
    inner = _SSD_HEADS * _SSD_HEAD_DIM
    za, xbc, dt, glu, zb = _even_proj(x, proj)
    xs, bm, cm, dect, xdtt, yb, sctx_n, cctx_n = _sample_even_pre(
        xbc, dt, glu, zb, jnp.swapaxes(st_sconv, 0, 1), jnp.swapaxes(st_cconv, 0, 1), (*ssd, *cfp))
    st_new, ya = _sample_even_rec(st_ssm.reshape(db, inner, _SSD_STATE), dect, xdtt, bm, cm, xs, za, d_x, ng)
    x_new = _outproj_ln(x, [ya.reshape(db, inner), yb], wo, ln_g, ln_b)
    return (x_new, st_new.reshape(st_ssm.shape), jnp.swapaxes(sctx_n, 0, 1), jnp.swapaxes(cctx_n, 0, 1))


def _odd_weights(w_in, w_out):
    aw = _ATTN_HEADS * _ATTN_HEAD_DIM
    kvw = _ATTN_KV_HEADS * _ATTN_HEAD_DIM
    iw = _IDX_HEADS * _IDX_DIM
    o1, o2, o3, o4 = aw, aw + kvw, aw + 2 * kvw, aw + 2 * kvw + iw
    o5 = o4 + _IDX_DIM + _IDX_HEADS
    wb = w_in.astype(_BF)
    ws = (wb[:, :o1], wb[:, o1:o2], wb[:, o2:o3], wb[:, o3:o4], _pad_lanes(wb[:, o4:o5]), wb[:, o5:])
    wts = (ws[0].T, ws[3].T, ws[4].T, ws[2].T)
    return ws, wts, w_out.astype(_BF)


def _odd_layer_prompt(x, wts, ln_g, ln_b, bsz, seq):
    ws, wtr, wo = wts
    tkb = min(512, seq)
    k, v, ki, z, qt, qit, wit, kib, kg, vt = _odd_proj_prompt(x, ws, wtr, bsz, seq, tkb)
    o = _attn_prompt(qit, wit, qt, z, kib, kg, vt, bsz, seq, tkb)
    x_new = _outproj_ln(x, [o], [wo], ln_g, ln_b)
    return (x_new, k.reshape(bsz, seq, _ATTN_KV_HEADS, _ATTN_HEAD_DIM),
            v.reshape(bsz, seq, _ATTN_KV_HEADS, _ATTN_HEAD_DIM), ki.reshape(bsz, seq, _IDX_DIM))


def _odd_layer_sample(x, cache_k, cache_v, cache_ki, page_table, wts, ln_g, ln_b):
    ws, _, wo = wts
    db = x.shape[0]
    q, k, v, qi, ki, wi, z = _odd_proj_sample(x, ws)
    o = _attn_sample(page_table, qi, wi, ki, q, k, v, cache_ki, cache_k, cache_v)
    x_new = _outproj_ln(x, [o], [wo], ln_g, ln_b, z=z)
    return (x_new, k.reshape(db, 1, _ATTN_KV_HEADS, _ATTN_HEAD_DIM),
            v.reshape(db, 1, _ATTN_KV_HEADS, _ATTN_HEAD_DIM), ki.reshape(db, 1, _IDX_DIM))


def kernel(x_prompt, x_sample, state_ssm_l0, state_ssdconv_l0, state_cfconv_l0, cache_k_l1, cache_v_l1, cache_kidx_l1, state_ssm_l2, state_ssdconv_l2, state_cfconv_l2, cache_k_l3, cache_v_l3, cache_kidx_l3, page_table, w_in_even, ssd_conv_w, ssd_conv_b, ssd_dt_bias, ssd_a_log, ssd_d, ssd_norm_g, cf_dw_w, cf_dw_b, cf_ln_g, cf_ln_b, w_out_even, w_in_odd, w_out_odd, ln_g, ln_b):
    bsz, seq, d = x_prompt.shape
    db = x_sample.shape[0]
    ssm_states = (state_ssm_l0, state_ssm_l2)
    sconv_states = (state_ssdconv_l0, state_ssdconv_l2)
    cconv_states = (state_cfconv_l0, state_cfconv_l2)
    k_caches = (cache_k_l1, cache_k_l3)
    v_caches = (cache_v_l1, cache_v_l3)
    ki_caches = (cache_kidx_l1, cache_kidx_l3)
    yp = x_prompt.reshape(bsz * seq, d)
    ys = x_sample.reshape(db, d)
    new_state = []
    for layer in range(_DEPTH):
        j = layer // 2
        g, b = ln_g[layer].reshape(1, d), ln_b[layer].reshape(1, d)
        if layer % 2 == 0:
            wts = _even_weights(w_in_even[j], ssd_conv_w[j], ssd_conv_b[j], ssd_dt_bias[j], ssd_a_log[j],
                                ssd_d[j], ssd_norm_g[j], cf_dw_w[j], cf_dw_b[j], cf_ln_g[j], cf_ln_b[j],
                                w_out_even[j])
            yp, ssm_p, sc_p, cc_p = _even_layer_prompt(yp, wts, g, b, bsz, seq)
            ys, ssm_s, sc_s, cc_s = _even_layer_sample(ys, ssm_states[j], sconv_states[j], cconv_states[j],
                                                       wts, g, b)
            new_state += [ssm_p, ssm_s, sc_p, sc_s, cc_p, cc_s]
        else:
            wts = _odd_weights(w_in_odd[j], w_out_odd[j])
            yp, k_p, v_p, ki_p = _odd_layer_prompt(yp, wts, g, b, bsz, seq)
            ys, k_s, v_s, ki_s = _odd_layer_sample(ys, k_caches[j], v_caches[j], ki_caches[j], page_table,
                                                   wts, g, b)
            new_state += [k_p, k_s, v_p, v_s, ki_p, ki_s]
    return (yp.reshape(bsz, seq, d), ys.reshape(db, 1, d), *new_state)
```

```python
import functools

import jax
import jax.numpy as jnp
from jax import lax
from jax.experimental import pallas as pl
from jax.experimental.pallas import tpu as pltpu

_BF = jnp.bfloat16
_F32 = jnp.float32
_HI = lax.Precision.HIGHEST

_SSD_HEADS = 16
_SSD_HEAD_DIM = 64
_SSD_GROUPS = 2
_SSD_STATE = 128
_SSD_CONV = 4
_CF_KERNEL = 31
_ATTN_HEADS = 16
_ATTN_KV_HEADS = 4
_ATTN_HEAD_DIM = 64
_IDX_HEADS = 8
_IDX_DIM = 64
_TOPK_MAX = 256
_PAGE = 128
_DEPTH = 4
_ALPHA = (2 * _DEPTH) ** 0.25
_EPS = 1e-5
_LOG2E = 1.4426950408889634

_LANES = 128
_VMEM_LIMIT = 56 * 1024 * 1024
_NEG = -1e30
_MAX_BISECT = 200


def _dot(a, b):
    return jnp.dot(a, b, preferred_element_type=_F32)


def _dot_nt(a, b):
    return lax.dot_general(a, b, (((1,), (1,)), ((), ())), preferred_element_type=_F32)


def _dot_hi(a, b):
    return jnp.dot(a, b, precision=_HI, preferred_element_type=_F32)


def _dot_nt_hi(a, b):
    return lax.dot_general(a, b, (((1,), (1,)), ((), ())), precision=_HI,
                           preferred_element_type=_F32)


def _sigmoid(x):
    return 1.0 / (1.0 + jnp.exp(-x))


def _silu(x):
    return x * _sigmoid(x)


def _softplus(x):
    return jnp.maximum(x, 0.0) + jnp.log1p(jnp.exp(-jnp.abs(x)))


def _layer_norm(x, g, b):
    mu = jnp.mean(x, axis=-1, keepdims=True)
    xc = x - mu
    var = jnp.mean(xc * xc, axis=-1, keepdims=True)
    return xc * lax.rsqrt(var + _EPS) * g + b


def _cparams(*sem):
    return pltpu.CompilerParams(dimension_semantics=sem, vmem_limit_bytes=_VMEM_LIMIT)


def _const_spec(shape):
    return pl.BlockSpec(shape, lambda *_: (0,) * len(shape), pipeline_mode=pl.Buffered(1))


def _even_proj_kernel(x_ref, wza, wxbc, wdt, wglu, wzb, za_o, xbc_o, dt_o, glu_o, zb_o):
    xb = x_ref[...].astype(_BF)
    za_o[...] = _dot(xb, wza[...])
    xbc_o[...] = _dot(xb, wxbc[...])
    dt_o[...] = _dot(xb, wdt[...])
    glu_o[...] = _dot(xb, wglu[...])
    zb_o[...] = _dot(xb, wzb[...])


def _even_proj(x, ws):
    t, d = x.shape
    tm = min(256, t)
    widths = [w.shape[1] for w in ws]
    row = lambda n: pl.BlockSpec((tm, n), lambda i: (i, 0))
    return pl.pallas_call(
        _even_proj_kernel,
        grid=(t // tm,),
        in_specs=[row(d)] + [_const_spec(w.shape) for w in ws],
        out_specs=[row(n) for n in widths],
        out_shape=[jax.ShapeDtypeStruct((t, n), _F32) for n in widths],
        compiler_params=_cparams("arbitrary"),
        name="even_proj",
    )(x, *ws)


def _odd_proj_prompt_kernel(x_ref, wk, wv, wkw, wz, wq_t, wqi_t, wkw_t, wv_t,
                            k_o, v_o, ki_o, z_o, qt_o, qit_o, wit_o, kib_o, kg_o, vt_o):
    xb = x_ref[...].astype(_BF)
    k = _dot(xb, wk[...])
    k_o[...] = k
    v_o[...] = _dot(xb, wv[...])
    kw = _dot(xb, wkw[...])
    ki_o[...] = kw[:, :_IDX_DIM]
    kib_o[...] = kw[:, :_IDX_DIM].astype(_BF)
    z_o[...] = _dot(xb, wz[...])
    qt_o[...] = (_dot_nt(wq_t[...], xb) * (_ATTN_HEAD_DIM ** -0.5 * _LOG2E)).astype(_BF)
    qit_o[...] = _dot_nt(wqi_t[...], xb).astype(_BF)
    wit_o[...] = _dot_nt(wkw_t[...], xb)
    vt = _dot_nt(wv_t[...], xb)
    ones = jnp.ones((_ATTN_HEAD_DIM, xb.shape[0]), _BF)
    for g in range(_ATTN_KV_HEADS):
        cols = slice(g * _ATTN_HEAD_DIM, (g + 1) * _ATTN_HEAD_DIM)
        kg_o[g] = k[:, cols].astype(_BF)
        vt_o[g, 0:_ATTN_HEAD_DIM, :] = vt[cols, :].astype(_BF)
        vt_o[g, _ATTN_HEAD_DIM:2 * _ATTN_HEAD_DIM, :] = ones


def _odd_proj_prompt(x, ws, wts, bsz, seq, tkb):
    t, d = x.shape
    tm = tkb
    nb = seq // tm
    _, wk, wv, _, wkw, wz = ws
    kvw = wk.shape[1]
    aw, iw = wts[0].shape[0], wts[1].shape[0]
    row = lambda n: pl.BlockSpec((tm, n), lambda i: (i, 0))
    col = lambda n: pl.BlockSpec((n, tm), lambda i: (0, i))
    outs = [
        (jax.ShapeDtypeStruct((t, kvw), _F32), row(kvw)),
        (jax.ShapeDtypeStruct((t, kvw), _F32), row(kvw)),
        (jax.ShapeDtypeStruct((t, _IDX_DIM), _F32), row(_IDX_DIM)),
        (jax.ShapeDtypeStruct((t, wz.shape[1]), _F32), row(wz.shape[1])),
        (jax.ShapeDtypeStruct((aw, t), _BF), col(aw)),
        (jax.ShapeDtypeStruct((iw, t), _BF), col(iw)),
        (jax.ShapeDtypeStruct((_LANES, t), _F32), col(_LANES)),
        (jax.ShapeDtypeStruct((t, _IDX_DIM), _BF), row(_IDX_DIM)),
        (jax.ShapeDtypeStruct((bsz, _ATTN_KV_HEADS, seq, _ATTN_HEAD_DIM), _BF),
         pl.BlockSpec((None, _ATTN_KV_HEADS, tm, _ATTN_HEAD_DIM), lambda i: (i // nb, 0, i % nb, 0))),
        (jax.ShapeDtypeStruct((bsz, nb, _ATTN_KV_HEADS, 2 * _ATTN_HEAD_DIM, tm), _BF),
         pl.BlockSpec((None, None, _ATTN_KV_HEADS, 2 * _ATTN_HEAD_DIM, tm),
                      lambda i: (i // nb, i % nb, 0, 0, 0))),
    ]
    wins = (wk, wv, wkw, wz, *wts)
    return pl.pallas_call(
        _odd_proj_prompt_kernel,
        grid=(t // tm,),
        in_specs=[row(d)] + [_const_spec(w.shape) for w in wins],
        out_specs=[o[1] for o in outs],
        out_shape=[o[0] for o in outs],
        compiler_params=_cparams("arbitrary"),
        name="odd_proj_prompt",
    )(x, *wins)


def _odd_proj_sample_kernel(x_ref, wq, wk, wv, wqi, wkw, wz, q_o, k_o, v_o, qi_o, ki_o, wi_o, z_o):
    xb = x_ref[...].astype(_BF)
    q_o[...] = _dot(xb, wq[...]) * (_ATTN_HEAD_DIM ** -0.5)
    k_o[...] = _dot(xb, wk[...])
    v_o[...] = _dot(xb, wv[...])
    qi_o[...] = _dot(xb, wqi[...])
    kw = _dot(xb, wkw[...])
    ki_o[...] = kw[:, :_IDX_DIM]
    wi_o[...] = kw
    z_o[...] = _dot(xb, wz[...])


def _odd_proj_sample(x, ws):
    t, d = x.shape
    wq, wk, wv, wqi, wkw, wz = ws
    widths = [wq.shape[1], wk.shape[1], wv.shape[1], wqi.shape[1], _IDX_DIM, _LANES, wz.shape[1]]
    full = lambda n: pl.BlockSpec((t, n), lambda i: (0, 0))
    return pl.pallas_call(
        _odd_proj_sample_kernel,
        grid=(1,),
        in_specs=[full(d)] + [_const_spec(w.shape) for w in ws],
        out_specs=[full(n) for n in widths],
        out_shape=[jax.ShapeDtypeStruct((t, n), _F32) for n in widths],
        compiler_params=_cparams("arbitrary"),
        name="odd_proj_sample",
    )(x, *ws)


def _outproj_kernel(*refs, n_in, gated):
    x_ref = refs[0]
    a_refs = refs[1:1 + n_in]
    pos = 1 + n_in
    z_ref = refs[pos] if gated else None
    pos += 1 if gated else 0
    w_refs = refs[pos:pos + n_in]
    g_ref, b_ref, o_ref = refs[pos + n_in:pos + n_in + 3]
    acc = _ALPHA * x_ref[...]
    for a_ref, w_ref in zip(a_refs, w_refs):
        a = a_ref[...]
        if gated:
            a = a.astype(_F32) * _silu(z_ref[...])
        acc = acc + _dot(a.astype(_BF), w_ref[...])
    o_ref[...] = _layer_norm(acc, g_ref[...], b_ref[...])


def _outproj_ln(x, a_list, w_list, g, b, z=None):
    t, d = x.shape
    tm = min(512, t)
    row = lambda n: pl.BlockSpec((tm, n), lambda i: (i, 0))
    gated = z is not None
    ins = [x, *a_list] + ([z] if gated else []) + [*w_list, g, b]
    specs = ([row(d)] + [row(a.shape[1]) for a in a_list] + ([row(z.shape[1])] if gated else [])
             + [_const_spec(w.shape) for w in w_list] + [_const_spec(g.shape), _const_spec(b.shape)])
    return pl.pallas_call(
        functools.partial(_outproj_kernel, n_in=len(a_list), gated=gated),
        grid=(t // tm,),
        in_specs=specs,
        out_specs=row(d),
        out_shape=jax.ShapeDtypeStruct((t, d), _F32),
        compiler_params=_cparams("arbitrary"),
        name="outproj_ln",
    )(*ins)


def _group_rmsnorm_gate(y, za, ng):
    y = y * _silu(za)
    gw = y.shape[1] // _SSD_GROUPS
    parts = []
    for g in range(_SSD_GROUPS):
        yg = y[:, g * gw:(g + 1) * gw]
        ms = jnp.mean(yg * yg, axis=-1, keepdims=True)
        parts.append(yg * lax.rsqrt(ms + _EPS))
    return jnp.concatenate(parts, axis=-1) * ng


def _even_mix_kernel(za_ref, xbc_ref, dt_ref, glu_ref, zb_ref,
                     cw_ref, cb_ref, dtb_ref, alog_ref, d_ref, ng_ref,
                     fw_ref, fb_ref, fg_ref, fbeta_ref,
                     ya_o, yb_o, ssm_o, sc_o, cc_o,
                     xbuf, ubuf, state, ushift, cacc, *, q, nc):
    c = pl.program_id(1)
    inner = _SSD_HEADS * _SSD_HEAD_DIM
    gn = _SSD_GROUPS * _SSD_STATE
    xhalo = 8
    uhalo = 32

    @pl.when(c == 0)
    def _():
        xbuf[0:xhalo, :] = jnp.zeros((xhalo, xbuf.shape[1]), _F32)
        ubuf[0:uhalo, :] = jnp.zeros((uhalo, ubuf.shape[1]), _F32)
        state[...] = jnp.zeros(state.shape, _F32)

    xbuf[xhalo:xhalo + q, :] = xbc_ref[...]
    acc = jnp.broadcast_to(cb_ref[...], (q, xbuf.shape[1]))
    for k in range(_SSD_CONV):
        off = xhalo - (_SSD_CONV - 1) + k
        acc = acc + cw_ref[k:k + 1, :] * xbuf[off:off + q, :]
    xc = _silu(acc)
    xs = xc[:, :inner]
    bm = [xc[:, inner + g * _SSD_STATE: inner + (g + 1) * _SSD_STATE].astype(_BF) for g in range(_SSD_GROUPS)]
    cm = [xc[:, inner + gn + g * _SSD_STATE: inner + gn + (g + 1) * _SSD_STATE].astype(_BF)
          for g in range(_SSD_GROUPS)]

    @pl.when(c == nc - 1)
    def _():
        sc_o[...] = xbuf[xhalo + q - (_SSD_CONV - 1):xhalo + q, :]

    xbuf[0:xhalo, :] = xbuf[q:q + xhalo, :]

    lane = lax.broadcasted_iota(jnp.int32, (1, _LANES), 1)
    hmask = lane < _SSD_HEADS
    dt = jnp.where(hmask, _softplus(dt_ref[...] + dtb_ref[...]), 0.0)
    a = jnp.where(hmask, -jnp.exp(alog_ref[...]), 0.0)
    ri = lax.broadcasted_iota(jnp.int32, (q, q), 0)
    ci = lax.broadcasted_iota(jnp.int32, (q, q), 1)
    trilb = ri >= ci
    a_cum = _dot_hi(trilb.astype(_F32), dt * a)
    a_cum_t = a_cum.T
    dt_t = dt.T
    a_last_t = a_cum_t[:, q - 1:q]
    w_t = dt_t * jnp.exp(a_last_t - a_cum_t)
    ea = jnp.exp(a_cum)
    cd_t = jnp.exp(a_last_t)
    lo_lane = lane < _SSD_HEAD_DIM
    lo_sub = lax.broadcasted_iota(jnp.int32, (_LANES, 1), 0) < _SSD_HEAD_DIM

    cb = [_dot_nt(cm[g], bm[g]) for g in range(_SSD_GROUPS)]
    heads_per_group = _SSD_HEADS // _SSD_GROUPS
    ys = []
    for j in range(_SSD_HEADS // 2):
        h0, h1 = 2 * j, 2 * j + 1
        g = h0 // heads_per_group
        sl = slice(j * _LANES, (j + 1) * _LANES)
        xs2 = xs[:, sl]
        x2 = (xs2 * jnp.where(lo_lane, dt[:, h0:h0 + 1], dt[:, h1:h1 + 1])).astype(_BF)
        l0 = jnp.where(trilb, jnp.exp(a_cum[:, h0:h0 + 1] - a_cum_t[h0:h0 + 1, :]), 0.0)
        l1 = jnp.where(trilb, jnp.exp(a_cum[:, h1:h1 + 1] - a_cum_t[h1:h1 + 1, :]), 0.0)
        y0 = _dot((cb[g] * l0).astype(_BF), x2)
        y1 = _dot((cb[g] * l1).astype(_BF), x2)
        sp = state[sl, :]
        yoff = _dot_nt(cm[g], sp.astype(_BF)) * jnp.where(lo_lane, ea[:, h0:h0 + 1], ea[:, h1:h1 + 1])
        w2 = jnp.where(lo_sub, w_t[h0:h0 + 1, :], w_t[h1:h1 + 1, :])
        s_chunk = _dot((xs2.T * w2).astype(_BF), bm[g])
        cd2 = jnp.where(lo_sub, cd_t[h0:h0 + 1, :], cd_t[h1:h1 + 1, :])
        state[sl, :] = sp * cd2 + s_chunk
        ys.append(jnp.where(lo_lane, y0, y1) + yoff + d_ref[:, sl] * xs2)
    y = jnp.concatenate(ys, axis=-1)
    ya_o[...] = _group_rmsnorm_gate(y, za_ref[...], ng_ref[...]).astype(_BF)

    @pl.when(c == nc - 1)
    def _():
        ssm_o[...] = state[...]

    cf = ubuf.shape[1]
    u = glu_ref[:, :cf] * _sigmoid(glu_ref[:, cf:])
    ubuf[uhalo:uhalo + q, :] = u
    back = 8 * ((_CF_KERNEL - 1) // 8)
    for r in range(1, 8):
        ushift[r - 1] = ubuf[uhalo - back - r:uhalo + q - r, :]
    for cb in range(cf // _LANES):
        cs = slice(cb * _LANES, (cb + 1) * _LANES)
        acc = jnp.broadcast_to(fb_ref[:, cs], (q, _LANES))
        for j in range(_CF_KERNEL):
            r, a = j % 8, j // 8
            k = _CF_KERNEL - 1 - j
            if r == 0:
                win = ubuf[uhalo - 8 * a:uhalo - 8 * a + q, cs]
            else:
                win = ushift[r - 1, back - 8 * a:back - 8 * a + q, cs]
            acc = acc + fw_ref[k:k + 1, cs] * win
        cacc[:, cs] = acc
    v = _silu(_layer_norm(cacc[...], fg_ref[...], fbeta_ref[...]))
    yb_o[...] = (v * _silu(zb_ref[...])).astype(_BF)

    @pl.when(c == nc - 1)
    def _():
        cc_o[...] = ubuf[uhalo + q - (_CF_KERNEL - 1):uhalo + q, :]

    ubuf[0:uhalo, :] = ubuf[q:q + uhalo, :]


def _even_mix_prompt(za, xbc, dt, glu, zb, params, bsz, seq):
    q = 128
    nc = seq // q
    inner = _SSD_HEADS * _SSD_HEAD_DIM
    conv_dim = xbc.shape[1]
    cf = zb.shape[1]
    row = lambda n: pl.BlockSpec((q, n), lambda b, c: (b * nc + c, 0))
    per_b = lambda r, n: pl.BlockSpec((None, r, n), lambda b, c: (b, 0, 0))
    t = bsz * seq
    return pl.pallas_call(
        functools.partial(_even_mix_kernel, q=q, nc=nc),
        grid=(bsz, nc),
        in_specs=[row(inner), row(conv_dim), row(_LANES), row(2 * cf), row(cf)]
                 + [_const_spec(p.shape) for p in params],
        out_specs=[row(inner), row(cf), per_b(inner, _SSD_STATE),
                   per_b(_SSD_CONV - 1, conv_dim), per_b(_CF_KERNEL - 1, cf)],
        out_shape=[jax.ShapeDtypeStruct((t, inner), _BF), jax.ShapeDtypeStruct((t, cf), _BF),
                   jax.ShapeDtypeStruct((bsz, inner, _SSD_STATE), _F32),
                   jax.ShapeDtypeStruct((bsz, _SSD_CONV - 1, conv_dim), _F32),
                   jax.ShapeDtypeStruct((bsz, _CF_KERNEL - 1, cf), _F32)],
        scratch_shapes=[pltpu.VMEM((8 + q, conv_dim), _F32), pltpu.VMEM((32 + q, cf), _F32),
                        pltpu.VMEM((inner, _SSD_STATE), _F32),
                        pltpu.VMEM((7, q + 8 * ((_CF_KERNEL - 1) // 8), cf), _F32),
                        pltpu.VMEM((q, cf), _F32)],
        compiler_params=_cparams("arbitrary", "arbitrary"),
        name="even_mix_prompt",
    )(za, xbc, dt, glu, zb, *params)


def _sample_even_pre_kernel(xbc_ref, dt_ref, glu_ref, zb_ref, sctx_ref, cctx_ref,
                            cw_ref, cb_ref, dtb_ref, alog_ref, fw_ref, fb_ref, fg_ref, fbeta_ref,
                            xs_o, bm_o, cm_o, dect_o, xdtt_o, yb_o, sctx_o, cctx_o, pad):
    db = xbc_ref.shape[0]
    inner = _SSD_HEADS * _SSD_HEAD_DIM
    gn = _SSD_GROUPS * _SSD_STATE
    xbc = xbc_ref[...]
    acc = cb_ref[...] + cw_ref[_SSD_CONV - 1:_SSD_CONV, :] * xbc
    for k in range(_SSD_CONV - 1):
        acc = acc + cw_ref[k:k + 1, :] * sctx_ref[k]
    xc = _silu(acc)
    xs = xc[:, :inner]
    xs_o[...] = xs
    bm_o[...] = xc[:, inner:inner + gn]
    cm_o[...] = xc[:, inner + gn:inner + 2 * gn]
    for k in range(_SSD_CONV - 2):
        sctx_o[k] = sctx_ref[k + 1]
    sctx_o[_SSD_CONV - 2] = xbc

    lane = lax.broadcasted_iota(jnp.int32, (1, _LANES), 1)
    hmask = lane < _SSD_HEADS
    dt = jnp.where(hmask, _softplus(dt_ref[...] + dtb_ref[...]), 0.0)
    a = jnp.where(hmask, -jnp.exp(alog_ref[...]), 0.0)
    dec = jnp.where(hmask, jnp.exp(dt * a), 0.0)
    er = lax.broadcasted_iota(jnp.int32, (_LANES, inner), 0)
    ec = lax.broadcasted_iota(jnp.int32, (_LANES, inner), 1)
    expand = jnp.where((ec // _SSD_HEAD_DIM) == er, 1.0, 0.0)
    dec_x = _dot_hi(dec, expand)
    xdt = xs * _dot_hi(dt, expand)
    for src, dst in ((dec_x, dect_o), (xdt, xdtt_o)):
        pad[...] = jnp.zeros(pad.shape, _F32)
        pad[0:db, :] = src
        for j in range(inner // _LANES):
            dst[j * _LANES:(j + 1) * _LANES, :] = pad[:, j * _LANES:(j + 1) * _LANES].T

    cf = zb_ref.shape[1]
    u = glu_ref[:, :cf] * _sigmoid(glu_ref[:, cf:])
    acc = fb_ref[...] + fw_ref[_CF_KERNEL - 1:_CF_KERNEL, :] * u
    for k in range(_CF_KERNEL - 1):
        acc = acc + fw_ref[k:k + 1, :] * cctx_ref[k]
    v = _silu(_layer_norm(acc, fg_ref[...], fbeta_ref[...]))
    yb_o[...] = (v * _silu(zb_ref[...])).astype(_BF)
    for k in range(_CF_KERNEL - 2):
        cctx_o[k] = cctx_ref[k + 1]
    cctx_o[_CF_KERNEL - 2] = u


def _sample_even_pre(xbc, dt, glu, zb, sctx_t, cctx_t, params):
    db = xbc.shape[0]
    inner = _SSD_HEADS * _SSD_HEAD_DIM
    gn = _SSD_GROUPS * _SSD_STATE
    cf = zb.shape[1]
    ins = [xbc, dt, glu, zb, sctx_t, cctx_t, *params]
    full = lambda s: pl.BlockSpec(s, lambda i: (0,) * len(s))
    out_shapes = [(db, inner), (db, gn), (db, gn), (inner, _LANES), (inner, _LANES), (db, cf),
                  sctx_t.shape, cctx_t.shape]
    out_dtypes = [_F32, _F32, _F32, _F32, _F32, _BF, _F32, _F32]
    return pl.pallas_call(
        _sample_even_pre_kernel,
        grid=(1,),
        in_specs=[full(a.shape) for a in ins],
        out_specs=[full(s) for s in out_shapes],
        out_shape=[jax.ShapeDtypeStruct(s, d) for s, d in zip(out_shapes, out_dtypes)],
        scratch_shapes=[pltpu.VMEM((_LANES, inner), _F32)],
        compiler_params=_cparams("arbitrary"),
        name="sample_even_pre",
    )(*ins)


def _sample_even_rec_kernel(st_ref, dect_ref, xdtt_ref, bm_ref, cm_ref, xs_ref, za_ref, d_ref, ng_ref,
                            st_o, ya_o):
    b = pl.program_id(0)
    inner = st_ref.shape[0]
    half = inner // _SSD_GROUPS
    sel = jnp.where(lax.broadcasted_iota(jnp.int32, (_LANES, _LANES), 0) == b, 1.0, 0.0)
    dec = _dot_hi(dect_ref[...], sel)
    xd = _dot_hi(xdtt_ref[...], sel)
    grp0 = lax.broadcasted_iota(jnp.int32, (inner, 1), 0) < half
    brow = jnp.where(grp0, bm_ref[:, :_SSD_STATE], bm_ref[:, _SSD_STATE:])
    s_new = st_ref[...] * dec + xd * brow
    st_o[...] = s_new
    r8 = lax.broadcasted_iota(jnp.int32, (8, 1), 0)
    c8 = jnp.where(r8 == 0, cm_ref[:, :_SSD_STATE], jnp.where(r8 == 1, cm_ref[:, _SSD_STATE:], 0.0))
    yr = _dot_nt_hi(c8, s_new)
    lane = lax.broadcasted_iota(jnp.int32, (1, inner), 1)
    xs = xs_ref[...]
    y = jnp.where(lane < half, yr[0:1, :], yr[1:2, :]) + d_ref[...] * xs
    ya_o[...] = _group_rmsnorm_gate(y, za_ref[...], ng_ref[...]).astype(_BF)


def _sample_even_rec(state, dect, xdtt, bm, cm, xs, za, d_x, ng):
    db, inner, n = state.shape
    per_b = lambda s: pl.BlockSpec((None,) + s, lambda b: (b, 0, 0))
    gn = bm.shape[1]
    r3 = lambda a: a.reshape(db, 1, a.shape[1])
    return pl.pallas_call(
        _sample_even_rec_kernel,
        grid=(db,),
        in_specs=[per_b((inner, n)), _const_spec(dect.shape), _const_spec(xdtt.shape),
                  per_b((1, gn)), per_b((1, gn)), per_b((1, inner)), per_b((1, inner)),
                  _const_spec(d_x.shape), _const_spec(ng.shape)],
        out_specs=[per_b((inner, n)), per_b((1, inner))],
        out_shape=[jax.ShapeDtypeStruct((db, inner, n), _F32), jax.ShapeDtypeStruct((db, 1, inner), _BF)],
        compiler_params=_cparams("arbitrary"),
        name="sample_even_rec",
    )(state, dect, xdtt, r3(bm), r3(cm), r3(xs), r3(za), d_x, ng)


def _bisect_threshold(count_ge, rmin, rmax, n_adm, c_ge0, c_gt0, k):
    nonneg = c_ge0 >= k
    lo0 = jnp.where(n_adm <= k, rmin, jnp.where(nonneg, 0.0, rmin))
    c0 = jnp.where(n_adm <= k, n_adm, jnp.where(nonneg, c_ge0, n_adm))
    above = jnp.maximum((rmax - lo0) * 2.0 ** -10, rmax * 2.0 ** -20)
    hi0 = jnp.where(c_gt0 >= k, rmax + above, 0.0)

    def midpoint(lo, hi):
        return lo + (hi - lo) * 0.5

    def open_rows(lo, hi, c_lo, mid):
        return jnp.where(c_lo > k, jnp.where(mid > lo, jnp.where(mid < hi, 1.0, 0.0), 0.0), 0.0)

    def cond(st):
        it, _, _, _, _, todo = st
        return jnp.logical_and(it < _MAX_BISECT, jnp.max(todo) > 0.5)

    def body(st):
        it, lo, hi, c_lo, mid, _ = st
        c = count_ge(mid)
        ge = c >= k
        lo_n = jnp.where(ge, mid, lo)
        hi_n = jnp.where(ge, hi, mid)
        c_n = jnp.where(ge, c, c_lo)
        mid_n = midpoint(lo_n, hi_n)
        return it + 1, lo_n, hi_n, c_n, mid_n, open_rows(lo_n, hi_n, c_n, mid_n)

    mid0 = midpoint(lo0, hi0)
    st = (jnp.int32(0), lo0, hi0, c0, mid0, open_rows(lo0, hi0, c0, mid0))
    _, lo, _, c_lo, _, _ = lax.while_loop(cond, body, st)
    return lo, c_lo


def _fold8(x, op):
    n, c = x.shape
    if n % 64 == 0 and n > 64:
        x = op(x.reshape(n // 64, 64, c), axis=0)
    return op(x.reshape(x.shape[0] // 8, 8, c), axis=0)


def _attn_prompt_kernel(qit_ref, wit_ref, qt_ref, z_ref, ki_ref, kg_ref, vt_ref, o_ref,
                        sc_ref, qis_ref, qg_ref, acc_ref, m_ref, al_ref, s_ref, p_ref, ob_ref,
                        *, tq, tkb, topk):
    i = pl.program_id(1)
    nkb = (i * tq + tq + tkb - 1) // tkb
    qpos = i * tq + lax.broadcasted_iota(jnp.int32, (1, tq), 1)
    kf = jnp.float32(topk)
    inf = jnp.float32(jnp.inf)
    rep = _ATTN_HEADS // _ATTN_KV_HEADS
    sk = _LANES

    qit = qit_ref[...]
    for h in range(_IDX_HEADS):
        qis_ref[:, h * tq:(h + 1) * tq] = qit[h * _IDX_DIM:(h + 1) * _IDX_DIM, :]
    w = wit_ref[...] * (_IDX_HEADS ** -0.5 * _IDX_DIM ** -0.5)
    wrs = [w[_IDX_DIM + h:_IDX_DIM + h + 1, :] for h in range(_IDX_HEADS)]

    def score_body(kb, carry):
        rmin, rmax, cge0, cgt0 = carry
        for jb in range(tkb // sk):
            start = pl.multiple_of(kb * tkb + jb * sk, sk)
            kblk = ki_ref[pl.ds(start, sk), :]
            acc = jnp.zeros((sk, tq), _F32)
            for h in range(_IDX_HEADS):
                s = _dot(kblk, qis_ref[:, h * tq:(h + 1) * tq])
                acc = acc + jnp.maximum(s, 0.0) * wrs[h]
            kpos = start + lax.broadcasted_iota(jnp.int32, (sk, 1), 0)
            adm = kpos <= qpos
            sc = jnp.where(adm, acc, -inf)
            sc_ref[kb, jb * sk:(jb + 1) * sk, :] = sc
            rmax = jnp.maximum(rmax, _fold8(sc, jnp.max))
            rmin = jnp.minimum(rmin, _fold8(jnp.where(adm, acc, inf), jnp.min))
            cge0 = cge0 + _fold8(jnp.where(sc >= 0.0, 1.0, 0.0), jnp.sum)
            cgt0 = cgt0 + _fold8(jnp.where(sc > 0.0, 1.0, 0.0), jnp.sum)
        return rmin, rmax, cge0, cgt0

    zeros8 = jnp.zeros((8, tq), _F32)
    rmin, rmax, cge0, cgt0 = lax.fori_loop(
        0, nkb, score_body,
        (jnp.full((8, tq), inf, _F32), jnp.full((8, tq), -inf, _F32), zeros8, zeros8))
    rmin = jnp.min(rmin, axis=0, keepdims=True)
    rmax = jnp.max(rmax, axis=0, keepdims=True)
    cge0 = jnp.sum(cge0, axis=0, keepdims=True)
    cgt0 = jnp.sum(cgt0, axis=0, keepdims=True)

    def count_ge(x):
        def cbody(kb, cnt):
            return cnt + _fold8(jnp.where(sc_ref[kb] >= x, 1.0, 0.0), jnp.sum)
        cnt = lax.fori_loop(0, nkb, cbody, zeros8)
        return jnp.sum(cnt, axis=0, keepdims=True)

    n_adm = (qpos + 1).astype(_F32)
    lo, c_lo = _bisect_threshold(count_ge, rmin, rmax, n_adm, cge0, cgt0, kf)

    @pl.when(jnp.max(c_lo) > kf)
    def _():
        def gbody(kb, cnt):
            return cnt + _fold8(jnp.where(sc_ref[kb] > lo, 1.0, 0.0), jnp.sum)
        need = kf - jnp.sum(lax.fori_loop(0, nkb, gbody, zeros8), axis=0, keepdims=True)
        tri = jnp.where(lax.broadcasted_iota(jnp.int32, (tkb, tkb), 0)
                        >= lax.broadcasted_iota(jnp.int32, (tkb, tkb), 1), 1.0, 0.0).astype(_BF)

        def tbody(kb, carry):
            s = sc_ref[kb]
            eq = s == lo
            pre = _dot(tri, jnp.where(eq, 1.0, 0.0).astype(_BF))
            drop = jnp.logical_and(eq, (carry + pre) > need)
            sc_ref[kb] = jnp.where(drop, -inf, s)
            return carry + pre[tkb - 1:tkb, :]
        lax.fori_loop(0, nkb, tbody, jnp.zeros((1, tq), _F32))

    qt = qt_ref[...]
    for h in range(_ATTN_HEADS):
        g, r = divmod(h, rep)
        qg_ref[g, :, r * tq:(r + 1) * tq] = qt[h * _ATTN_HEAD_DIM:(h + 1) * _ATTN_HEAD_DIM, :]
    m_ref[...] = jnp.full(m_ref.shape, _NEG, _F32)
    acc_ref[...] = jnp.zeros(acc_ref.shape, _F32)

    def bias_body(kb, carry):
        sc_ref[kb] = jnp.where(sc_ref[kb] >= lo, 0.0, _NEG)
        return carry
    lax.fori_loop(0, nkb, bias_body, 0)

    def attn_body(kb, carry):
        base = pl.multiple_of(kb * tkb, tkb)
        for g in range(_ATTN_KV_HEADS):
            mxs = [None] * rep
            for jb in range(tkb // _LANES):
                rows = slice(jb * _LANES, (jb + 1) * _LANES)
                kblk = kg_ref[g, pl.ds(base + jb * _LANES, _LANES), :]
                bias = sc_ref[kb, rows, :]
                for r in range(rep):
                    cols = slice(r * tq, (r + 1) * tq)
                    s = _dot(kblk, qg_ref[g, :, cols]) + bias
                    s_ref[g, rows, cols] = s
                    part = _fold8(s, jnp.max)
                    mxs[r] = part if mxs[r] is None else jnp.maximum(mxs[r], part)
            mx = jnp.concatenate([jnp.max(m8, axis=0, keepdims=True) for m8 in mxs], axis=1)
            m_old = m_ref[g]
            m_new = jnp.maximum(m_old, mx)
            al_ref[g] = jnp.exp2(m_old - m_new)
            m_ref[g] = m_new
        for g in range(_ATTN_KV_HEADS):
            for r in range(rep):
                cols = slice(r * tq, (r + 1) * tq)
                m_new = m_ref[g, :, cols]
                for jb in range(tkb // _LANES):
                    rows = slice(jb * _LANES, (jb + 1) * _LANES)
                    p_ref[g, rows, cols] = jnp.exp2(s_ref[g, rows, cols] - m_new).astype(_BF)
                acc_ref[g, :, cols] = (acc_ref[g, :, cols] * al_ref[g, :, cols]
                                       + _dot(vt_ref[kb, g], p_ref[g, :, cols]))
        return carry

    lax.fori_loop(0, nkb, attn_body, 0)
    for h in range(_ATTN_HEADS):
        g, r = divmod(h, rep)
        acc = acc_ref[g, :, r * tq:(r + 1) * tq].T
        o_h = acc / pltpu.roll(acc, _ATTN_HEAD_DIM, 1)
        ob_ref[:, h * _ATTN_HEAD_DIM:(h + 1) * _ATTN_HEAD_DIM] = o_h[:, :_ATTN_HEAD_DIM]
    o_ref[...] = (ob_ref[...] * _silu(z_ref[...])).astype(_BF)


def _attn_prompt(qit, wit, qt, z, ki, kg, vt, bsz, seq, tkb):
    tq = 256
    nq = seq // tq
    nkbt = seq // tkb
    t = bsz * seq
    topk = min(_TOPK_MAX, seq // 4)
    width = z.shape[1]
    rep = _ATTN_HEADS // _ATTN_KV_HEADS
    row = lambda n: pl.BlockSpec((tq, n), lambda b, i: (b * nq + i, 0))
    col = lambda n: pl.BlockSpec((n, tq), lambda b, i: (0, b * nq + i))
    once = dict(pipeline_mode=pl.Buffered(1))
    return pl.pallas_call(
        functools.partial(_attn_prompt_kernel, tq=tq, tkb=tkb, topk=topk),
        grid=(bsz, nq),
        in_specs=[col(qit.shape[0]), col(_LANES), col(qt.shape[0]), row(width),
                  pl.BlockSpec((seq, _IDX_DIM), lambda b, i: (b, 0), **once),
                  pl.BlockSpec((None,) + kg.shape[1:], lambda b, i: (b, 0, 0, 0), **once),
                  pl.BlockSpec((None,) + vt.shape[1:], lambda b, i: (b, 0, 0, 0, 0), **once)],
        out_specs=row(width),
        out_shape=jax.ShapeDtypeStruct((t, width), _BF),
        scratch_shapes=[pltpu.VMEM((nkbt, tkb, tq), _F32),
                        pltpu.VMEM((_IDX_DIM, _IDX_HEADS * tq), _BF),
                        pltpu.VMEM((_ATTN_KV_HEADS, _ATTN_HEAD_DIM, rep * tq), _BF),
                        pltpu.VMEM((_ATTN_KV_HEADS, 2 * _ATTN_HEAD_DIM, rep * tq), _F32),
                        pltpu.VMEM((_ATTN_KV_HEADS, 1, rep * tq), _F32),
                        pltpu.VMEM((_ATTN_KV_HEADS, 1, rep * tq), _F32),
                        pltpu.VMEM((_ATTN_KV_HEADS, tkb, rep * tq), _F32),
                        pltpu.VMEM((_ATTN_KV_HEADS, tkb, rep * tq), _BF),
                        pltpu.VMEM((tq, width), _F32)],
        compiler_params=_cparams("arbitrary", "arbitrary"),
        name="attn_prompt",
    )(qit, wit, qt, z, ki, kg, vt)


def _attn_sample_kernel(pt_ref, qi_ref, w_ref, kin_ref, q_ref, kn_ref, vn_ref,
                        ckit_ref, ckt_ref, cvt_ref, o_ref,
                        kibuf, kbuf, vbuf, sc_ref, sn_ref, lg_ref, sem, *, n_pages, topk):
    b = pl.program_id(0)
    past = n_pages * _PAGE
    kf = jnp.float32(topk)
    inf = jnp.float32(jnp.inf)
    caches = ((ckit_ref, kibuf), (ckt_ref, kbuf), (cvt_ref, vbuf))
    unroll = min(8, n_pages)

    def page_copy(which, page):
        cache_ref, buf = caches[which]
        return pltpu.make_async_copy(cache_ref.at[pt_ref[b, page]], buf.at[page], sem.at[which])

    def start_all(which):
        lax.fori_loop(0, n_pages, lambda p, c: (page_copy(which, p).start(), c)[1], 0)

    def wait_all(which):
        lax.fori_loop(0, n_pages, lambda p, c: (page_copy(which, p).wait(), c)[1], 0)

    for which in range(3):
        start_all(which)

    def total(x):
        return jnp.sum(jnp.sum(x, axis=1, keepdims=True), axis=0, keepdims=True)

    qi8 = qi_ref[...].astype(_BF)
    w8 = w_ref[...] * (_IDX_HEADS ** -0.5 * _IDX_DIM ** -0.5)
    wait_all(0)

    def score_body(p, c):
        s8 = _dot(qi8, kibuf[p].astype(_BF))
        sc_ref[pl.ds(p, 1), :] = jnp.sum(jnp.maximum(s8, 0.0) * w8, axis=0, keepdims=True)
        return c
    lax.fori_loop(0, n_pages, score_body, 0, unroll=unroll)
    kin = kin_ref[...].astype(_BF).astype(_F32)
    sn8 = jnp.sum(qi8.astype(_F32) * kin, axis=1, keepdims=True)
    sn_ref[...] = jnp.sum(jnp.maximum(sn8, 0.0) * w8, axis=0, keepdims=True)
    snew = sn_ref[...]

    if past + 1 > topk:
        sc = sc_ref[...]
        rmax = jnp.maximum(jnp.max(jnp.max(sc, axis=1, keepdims=True), axis=0, keepdims=True), snew)
        rmin = jnp.minimum(jnp.min(jnp.min(sc, axis=1, keepdims=True), axis=0, keepdims=True), snew)
        one = lambda cnd: jnp.where(cnd, 1.0, 0.0)

        def count_ge(x):
            return total(one(sc_ref[...] >= x)) + one(snew >= x)

        lo, c_lo = _bisect_threshold(count_ge, rmin, rmax, jnp.full((1, 1), past + 1.0, _F32),
                                     count_ge(jnp.zeros((1, 1), _F32)),
                                     total(one(sc > 0.0)) + one(snew > 0.0), kf)

        @pl.when(jnp.max(c_lo) > kf)
        def _():
            need = kf - (total(one(sc_ref[...] > lo)) + one(snew > lo))
            tri = jnp.where(lax.broadcasted_iota(jnp.int32, (_LANES, _LANES), 0)
                            <= lax.broadcasted_iota(jnp.int32, (_LANES, _LANES), 1), 1.0, 0.0).astype(_BF)

            def tbody(p, carry):
                s = sc_ref[pl.ds(p, 1), :]
                eq = s == lo
                eq8 = jnp.broadcast_to(one(eq), (8, _LANES)).astype(_BF)
                pre = _dot(eq8, tri)[0:1, :]
                drop = jnp.logical_and(eq, (carry + pre) > need)
                sc_ref[pl.ds(p, 1), :] = jnp.where(drop, -inf, s)
                return carry + pre[:, _LANES - 1:_LANES]
            carry = lax.fori_loop(0, n_pages, tbody, jnp.zeros((1, 1), _F32))
            drop_new = jnp.logical_and(snew == lo, (carry + 1.0) > need)
            sn_ref[...] = jnp.where(drop_new, -inf, snew)
    else:
        lo = jnp.full((1, 1), -inf, _F32)

    kvw = _ATTN_KV_HEADS * _ATTN_HEAD_DIM
    rep = _ATTN_HEADS // _ATTN_KV_HEADS
    rr = lax.broadcasted_iota(jnp.int32, (_ATTN_HEAD_DIM, kvw), 0)
    rc = lax.broadcasted_iota(jnp.int32, (_ATTN_HEAD_DIM, kvw), 1)
    spread = jnp.where((rc % _ATTN_HEAD_DIM) == rr, 1.0, 0.0)
    hr = lax.broadcasted_iota(jnp.int32, (_ATTN_HEADS, kvw), 0)
    hc = lax.broadcasted_iota(jnp.int32, (_ATTN_HEADS, kvw), 1)
    own = (hc // _ATTN_HEAD_DIM) == (hr // rep)
    qbd = jnp.where(own, _dot_hi(q_ref[...], spread), 0.0).astype(_BF)
    kn = kn_ref[...].astype(_BF).astype(_F32)
    lnew = (jnp.sum(qbd.astype(_F32) * kn, axis=1, keepdims=True)
            + jnp.where(sn_ref[...] >= lo, 0.0, _NEG))
    wait_all(1)

    def logit_body(p, mx):
        lg = _dot(qbd, kbuf[p].astype(_BF)) + jnp.where(sc_ref[pl.ds(p, 1), :] >= lo, 0.0, _NEG)
        lg_ref[p] = lg
        return jnp.maximum(mx, lg)
    mx = lax.fori_loop(0, n_pages, logit_body, jnp.full((_ATTN_HEADS, _LANES), _NEG, _F32), unroll=unroll)
    m = jnp.maximum(jnp.max(mx, axis=1, keepdims=True), lnew)
    pn = jnp.exp(lnew - m)
    vn = vn_ref[...].astype(_BF).astype(_F32)
    wait_all(2)

    def pv_body(p, carry):
        psum, out = carry
        pr = jnp.exp(lg_ref[p] - m)
        return psum + pr, out + _dot_nt(pr.astype(_BF), vbuf[p].astype(_BF))
    psum, out = lax.fori_loop(0, n_pages, pv_body,
                              (jnp.zeros((_ATTN_HEADS, _LANES), _F32), pn.astype(_BF).astype(_F32) * vn),
                              unroll=unroll)
    den = jnp.sum(psum, axis=1, keepdims=True) + pn
    out = jnp.where(own, out / den, 0.0)
    gather = jnp.where((lax.broadcasted_iota(jnp.int32, (kvw, _ATTN_HEAD_DIM), 0) % _ATTN_HEAD_DIM)
                       == lax.broadcasted_iota(jnp.int32, (kvw, _ATTN_HEAD_DIM), 1), 1.0, 0.0)
    o_ref[...] = _dot_hi(out, gather)


def _attn_sample(page_table, qi, wi, ki_new, q, k_new, v_new, cache_ki, cache_k, cache_v):
    db, n_pages = page_table.shape
    past = n_pages * _PAGE
    topk = min(_TOPK_MAX, (past + 1) // 4)
    kvw = _ATTN_KV_HEADS * _ATTN_HEAD_DIM
    n_pool = cache_k.shape[0]
    per_b = lambda s: pl.BlockSpec((None,) + s, lambda b, pt: (b,) + (0,) * len(s))
    anyspec = pl.BlockSpec(memory_space=pl.ANY)
    kv_t = lambda c: jnp.transpose(c, (0, 2, 3, 1)).reshape(n_pool, kvw, _PAGE)
    ins = [qi.reshape(db, _IDX_HEADS, _IDX_DIM),
           wi[:, _IDX_DIM:_IDX_DIM + _IDX_HEADS].reshape(db, _IDX_HEADS, 1),
           ki_new.reshape(db, 1, _IDX_DIM),
           q.reshape(db, _ATTN_HEADS, _ATTN_HEAD_DIM),
           k_new.reshape(db, 1, kvw), v_new.reshape(db, 1, kvw),
           jnp.transpose(cache_ki, (0, 2, 1)), kv_t(cache_k), kv_t(cache_v)]
    grid_spec = pltpu.PrefetchScalarGridSpec(
        num_scalar_prefetch=1,
        grid=(db,),
        in_specs=[per_b((_IDX_HEADS, _IDX_DIM)), per_b((_IDX_HEADS, 1)), per_b((1, _IDX_DIM)),
                  per_b((_ATTN_HEADS, _ATTN_HEAD_DIM)), per_b((1, kvw)), per_b((1, kvw)),
                  anyspec, anyspec, anyspec],
        out_specs=per_b((_ATTN_HEADS, _ATTN_HEAD_DIM)),
        scratch_shapes=[pltpu.VMEM((n_pages, _IDX_DIM, _PAGE), _F32),
                        pltpu.VMEM((n_pages, kvw, _PAGE), _F32),
                        pltpu.VMEM((n_pages, kvw, _PAGE), _F32),
                        pltpu.VMEM((n_pages, _PAGE), _F32), pltpu.VMEM((1, 1), _F32),
                        pltpu.VMEM((n_pages, _ATTN_HEADS, _PAGE), _F32),
                        pltpu.SemaphoreType.DMA((3,))],
    )
    o = pl.pallas_call(
        functools.partial(_attn_sample_kernel, n_pages=n_pages, topk=topk),
        grid_spec=grid_spec,
        out_shape=jax.ShapeDtypeStruct((db, _ATTN_HEADS, _ATTN_HEAD_DIM), _F32),
        compiler_params=_cparams("arbitrary"),
        name="attn_sample",
    )(page_table, *ins)
    return o.reshape(db, _ATTN_HEADS * _ATTN_HEAD_DIM)


def _pad_lanes(a, n=_LANES):
    return jnp.pad(a, [(0, 0)] * (a.ndim - 1) + [(0, n - a.shape[-1])])


def _even_weights(w_in, conv_w, conv_b, dt_bias, a_log, d_skip, norm_g, cf_w, cf_b, cf_g, cf_beta, w_out):
    inner = _SSD_HEADS * _SSD_HEAD_DIM
    conv_dim = conv_w.shape[1]
    cf = cf_w.shape[1]
    o1, o2, o3, o4 = inner, inner + conv_dim, inner + conv_dim + _SSD_HEADS, inner + conv_dim + _SSD_HEADS + 2 * cf
    wb = w_in.astype(_BF)
    proj = (wb[:, :o1], wb[:, o1:o2], _pad_lanes(wb[:, o2:o3]), wb[:, o3:o4], wb[:, o4:])
    row = lambda v: v.reshape(1, -1)
    ssd = (conv_w, row(conv_b), _pad_lanes(row(dt_bias)), _pad_lanes(row(a_log)))
    d_x = row(jnp.repeat(d_skip, _SSD_HEAD_DIM))
    cfp = (cf_w, row(cf_b), row(cf_g), row(cf_beta))
    wo = w_out.astype(_BF)
    return proj, ssd, d_x, row(norm_g), cfp, (wo[:inner], wo[inner:])


def _even_layer_prompt(x, wts, ln_g, ln_b, bsz, seq):
    proj, ssd, d_x, ng, cfp, wo = wts
    za, xbc, dt, glu, zb = _even_proj(x, proj)
    ya, yb, ssm, sc, cc = _even_mix_prompt(za, xbc, dt, glu, zb, (*ssd, d_x, ng, *cfp), bsz, seq)
    x_new = _outproj_ln(x, [ya, yb], wo, ln_g, ln_b)
    return x_new, ssm.reshape(bsz, _SSD_HEADS, _SSD_HEAD_DIM, _SSD_STATE), sc, cc


def _even_layer_sample(x, st_ssm, st_sconv, st_cconv, wts, ln_g, ln_b):
    proj, ssd, d_x, ng, cfp, wo = wts
    db = x.shape[0]
    inner = _SSD_HEADS * _SSD_HEAD_DIM
    za, xbc, dt, glu, zb = _even_proj(x, proj)
    xs, bm, cm, dect, xdtt, yb, sctx_n, cctx_n = _sample_even_pre(
        xbc, dt, glu, zb, jnp.swapaxes(st_sconv, 0, 1), jnp.swapaxes(st_cconv, 0, 1), (*ssd, *cfp))
    st_new, ya = _sample_even_rec(st_ssm.reshape(db, inner, _SSD_STATE), dect, xdtt, bm, cm, xs, za, d_x, ng)
    x_new = _outproj_ln(x, [ya.reshape(db, inner), yb], wo, ln_g, ln_b)
    return (x_new, st_new.reshape(st_ssm.shape), jnp.swapaxes(sctx_n, 0, 1), jnp.swapaxes(cctx_n, 0, 1))


def _odd_weights(w_in, w_out):
    aw = _ATTN_HEADS * _ATTN_HEAD_DIM
    kvw = _ATTN_KV_HEADS * _ATTN_HEAD_DIM
    iw = _IDX_HEADS * _IDX_DIM
    o1, o2, o3, o4 = aw, aw + kvw, aw + 2 * kvw, aw + 2 * kvw + iw
    o5 = o4 + _IDX_DIM + _IDX_HEADS
    wb = w_in.astype(_BF)
    ws = (wb[:, :o1], wb[:, o1:o2], wb[:, o2:o3], wb[:, o3:o4], _pad_lanes(wb[:, o4:o5]), wb[:, o5:])
    wts = (ws[0].T, ws[3].T, ws[4].T, ws[2].T)
    return ws, wts, w_out.astype(_BF)


def _odd_layer_prompt(x, wts, ln_g, ln_b, bsz, seq):
    ws, wtr, wo = wts
    tkb = min(512, seq)
    k, v, ki, z, qt, qit, wit, kib, kg, vt = _odd_proj_prompt(x, ws, wtr, bsz, seq, tkb)
    o = _attn_prompt(qit, wit, qt, z, kib, kg, vt, bsz, seq, tkb)
    x_new = _outproj_ln(x, [o], [wo], ln_g, ln_b)
    return (x_new, k.reshape(bsz, seq, _ATTN_KV_HEADS, _ATTN_HEAD_DIM),
            v.reshape(bsz, seq, _ATTN_KV_HEADS, _ATTN_HEAD_DIM), ki.reshape(bsz, seq, _IDX_DIM))


def _odd_layer_sample(x, cache_k, cache_v, cache_ki, page_table, wts, ln_g, ln_b):
    ws, _, wo = wts
    db = x.shape[0]
    q, k, v, qi, ki, wi, z = _odd_proj_sample(x, ws)
    o = _attn_sample(page_table, qi, wi, ki, q, k, v, cache_ki, cache_k, cache_v)
    x_new = _outproj_ln(x, [o], [wo], ln_g, ln_b, z=z)
    return (x_new, k.reshape(db, 1, _ATTN_KV_HEADS, _ATTN_HEAD_DIM),
            v.reshape(db, 1, _ATTN_KV_HEADS, _ATTN_HEAD_DIM), ki.reshape(db, 1, _IDX_DIM))


def kernel(x_prompt, x_sample, state_ssm_l0, state_ssdconv_l0, state_cfconv_l0, cache_k_l1, cache_v_l1, cache_kidx_l1, state_ssm_l2, state_ssdconv_l2, state_cfconv_l2, cache_k_l3, cache_v_l3, cache_kidx_l3, page_table, w_in_even, ssd_conv_w, ssd_conv_b, ssd_dt_bias, ssd_a_log, ssd_d, ssd_norm_g, cf_dw_w, cf_dw_b, cf_ln_g, cf_ln_b, w_out_even, w_in_odd, w_out_odd, ln_g, ln_b):
    bsz, seq, d = x_prompt.shape
    db = x_sample.shape[0]
    ssm_states = (state_ssm_l0, state_ssm_l2)
    sconv_states = (state_ssdconv_l0, state_ssdconv_l2)
    cconv_states = (state_cfconv_l0, state_cfconv_l2)
    k_caches = (cache_k_l1, cache_k_l3)
    v_caches = (cache_v_l1, cache_v_l3)
    ki_caches = (cache_kidx_l1, cache_kidx_l3)
    yp = x_prompt.reshape(bsz * seq, d)
    ys = x_sample.reshape(db, d)
    new_state = []
    for layer in range(_DEPTH):
        j = layer // 2
        g, b = ln_g[layer].reshape(1, d), ln_b[layer].reshape(1, d)
        if layer % 2 == 0:
            wts = _even_weights(w_in_even[j], ssd_conv_w[j], ssd_conv_b[j], ssd_dt_bias[j], ssd_a_log[j],
                                ssd_d[j], ssd_norm_g[j], cf_dw_w[j], cf_dw_b[j], cf_ln_g[j], cf_ln_b[j],
                                w_out_even[j])
            yp, ssm_p, sc_p, cc_p = _even_layer_prompt(yp, wts, g, b, bsz, seq)
            ys, ssm_s, sc_s, cc_s = _even_layer_sample(ys, ssm_states[j], sconv_states[j], cconv_states[j],
                                                       wts, g, b)
            new_state += [ssm_p, ssm_s, sc_p, sc_s, cc_p, cc_s]
        else:
            wts = _odd_weights(w_in_odd[j], w_out_odd[j])
            yp, k_p, v_p, ki_p = _odd_layer_prompt(yp, wts, g, b, bsz, seq)
            ys, k_s, v_s, ki_s = _odd_layer_sample(ys, k_caches[j], v_caches[j], ki_caches[j], page_table,
                                                   wts, g, b)
            new_state += [k_p, k_s, v_p, v_s, ki_p, ki_s]
    return (yp.reshape(bsz, seq, d), ys.reshape(db, 1, d), *new_state)
```
